```python
import math
import jax
import jax.numpy as jnp
from jax import lax
import numpy as np

D_MODEL = 1024
BATCH = 16
SEQ = 2048
DEPTH = 1

HY_WIDTH = 768
HY_ORDER = 2
HY_SHORT_CONV = 3
HY_EMB_DIM = 33
HY_FILTER_HIDDEN = 64
HY_FAST_DECAY = 0.3
HY_SLOW_DECAY = 1.5
HY_DECAY_TARGET = 1e-2
HY_MOD_SHIFT = 0.0
HEAD_DIM = 64
HEADS_PER_GROUP = 8
DILATED_GROUPS = ((128, 1), (512, 4), (2048, 16))
N_GROUPS = 3
AT_HEADS = N_GROUPS * HEADS_PER_GROUP
AT_QKV = AT_HEADS * HEAD_DIM
AT_WIDTH = HEADS_PER_GROUP * HEAD_DIM
BAND_BLOCK = 64
NORM_EPS = 1e-6
NEG_INF = -1e30
IN_COLS = 3 * HY_WIDTH + HY_WIDTH + 3 * AT_QKV + AT_WIDTH + 2 * D_MODEL

kernel_name = 'hybrid_hyena_dilated_attn_encoder_block'


def _rmsnorm(x, g):
    xf = x.astype(jnp.float32)
    y = xf * lax.rsqrt(jnp.mean(xf * xf, axis=-1, keepdims=True) + NORM_EPS)
    return (y * g.astype(jnp.float32)).astype(x.dtype)


def _alibi_slopes(n):
    return 2.0 ** (-8.0 * jnp.arange(1, n + 1, dtype=jnp.float32) / n)


def _short_conv(z, w, b):
    L = z.shape[1]
    p = HY_SHORT_CONV // 2
    zp = jnp.pad(z, ((0, 0), (p, p), (0, 0)))
    out = b
    for j in range(HY_SHORT_CONV):
        out = out + zp[:, j:j + L] * w[j]
    return out


def _hyena_filters(L, w1, b1, w2, b2, w3, b3, w4, freq):
    f32 = jnp.float32
    t = jnp.linspace(0.0, 1.0, L, dtype=f32)[:, None]
    bands = (HY_EMB_DIM - 1) // 2
    w = 2.0 * math.pi * jnp.arange(L, dtype=f32)[:, None] / L
    f = jnp.linspace(1e-4, bands - 1, bands, dtype=f32)[None, :]
    z = jnp.concatenate([t, jnp.cos(f * w), -jnp.sin(f * w)], axis=-1)
    fr = freq.astype(f32)
    h = jnp.sin(fr * (z @ w1.astype(f32) + b1.astype(f32)))
    h = jnp.sin(fr * (h @ w2.astype(f32) + b2.astype(f32)))
    h = jnp.sin(fr * (h @ w3.astype(f32) + b3.astype(f32)))
    h = (h @ w4.astype(f32)).reshape(L, HY_ORDER, 2, HY_WIDTH)
    max_decay = math.log(HY_DECAY_TARGET) / HY_FAST_DECAY
    min_decay = math.log(HY_DECAY_TARGET) / HY_SLOW_DECAY
    deltas = jnp.linspace(min_decay, max_decay, HY_WIDTH, dtype=f32)
    decay = jnp.exp(-t * jnp.abs(deltas)[None, :])
    h = h * (decay + HY_MOD_SHIFT)[:, None, None, :]
    h_fwd, h_bwd = h[:, :, 0], h[:, :, 1]
    k_circ = jnp.concatenate([h_fwd, jnp.zeros((1, HY_ORDER, HY_WIDTH), f32), h_bwd[1:][::-1]], axis=0)
    return jnp.fft.rfft(k_circ, axis=0)


def _fft_conv(u, k_f, skip):
    L = u.shape[1]
    U = jnp.fft.rfft(u, n=2 * L, axis=1)
    y = jnp.fft.irfft(U * k_f[None], n=2 * L, axis=1)[:, :L]
    return y + u * skip.astype(jnp.float32)


def _hyena(z_in, conv_w, conv_b, w1, b1, w2, b2, w3, b3, w4, freq, skip):
    L = z_in.shape[1]
    zc = _short_conv(z_in, conv_w, conv_b).astype(jnp.float32)
    v, x1, x2 = jnp.split(zc, 3, axis=-1)
    k_f = _hyena_filters(L, w1, b1, w2, b2, w3, b3, w4, freq)
    y = x1 * _fft_conv(v, k_f[:, 0], skip[0])
    y = x2 * _fft_conv(y, k_f[:, 1], skip[1])
    return y


def _band_group(q, k, v, window, dilation, slopes):
    B, H, S, Dh = q.shape
    half = window // (2 * dilation)
    n = S // dilation
    nb = -(-n // BAND_BLOCK)
    pad = nb * BAND_BLOCK - n

    def to_blocks(t):
        t = t.reshape(B, H, n, dilation, Dh).transpose(0, 1, 3, 2, 4)
        t = jnp.pad(t, ((0, 0), (0, 0), (0, 0), (0, pad), (0, 0)))
        return t.reshape(B, H, dilation, nb, BAND_BLOCK, Dh)

    def band(t):
        tp = jnp.pad(t, ((0, 0), (0, 0), (0, 0), (1, 1), (0, 0), (0, 0)))
        return jnp.concatenate([tp[:, :, :, :-2], tp[:, :, :, 1:-1], tp[:, :, :, 2:]], axis=4)

    qb = to_blocks(q)
    kw = band(to_blocks(k))
    vw = band(to_blocks(v))
    qi = jnp.arange(nb)[:, None] * BAND_BLOCK + jnp.arange(BAND_BLOCK)[None, :]
    ki = (jnp.arange(nb)[:, None] - 1) * BAND_BLOCK + jnp.arange(3 * BAND_BLOCK)[None, :]
    rel = qi[:, :, None] - ki[:, None, :]
    valid = (jnp.abs(rel) <= half) & (ki[:, None, :] >= 0) & (ki[:, None, :] < n)
    dist = (dilation * jnp.abs(rel)).astype(jnp.float32)
    s = jnp.einsum('bhrnqd,bhrnkd->bhrnqk', qb, kw) * (HEAD_DIM ** -0.5)
    s = s - slopes[None, :, None, None, None, None] * dist[None, None, None]
    s = jnp.where(valid[None, None, None], s, NEG_INF)
    m = jnp.max(s, axis=-1)
    p = jnp.exp(s - m[..., None])
    den = jnp.sum(p, axis=-1)
    o = jnp.einsum('bhrnqk,bhrnkd->bhrnqd', p, vw) / den[..., None]
    lse = m + jnp.log(den)
    o = o.reshape(B, H, dilation, nb * BAND_BLOCK, Dh)[:, :, :, :n].transpose(0, 1, 3, 2, 4).reshape(B, H, S, Dh)
    lse = lse.reshape(B, H, dilation, nb * BAND_BLOCK)[:, :, :, :n].transpose(0, 1, 3, 2).reshape(B, H, S)
    return o, lse


def _dilated_attention(q, k, v, slopes):
    outs, lses = [], []
    for g, (window, dilation) in enumerate(DILATED_GROUPS):
        o, lse = _band_group(q[:, :, g].transpose(0, 2, 1, 3), k[:, :, g].transpose(0, 2, 1, 3),
                             v[:, :, g].transpose(0, 2, 1, 3), window, dilation, slopes)
        outs.append(o)
        lses.append(lse)
    wts = jax.nn.softmax(jnp.stack(lses, axis=0), axis=0)
    return jnp.sum(wts[..., None] * jnp.stack(outs, axis=0), axis=0)


def setup_inputs(seed: int = 0) -> dict:
    key = jax.random.key(seed)
    ks = jax.random.split(key, 20)
    f32 = jnp.float32

    def nrm(k, shape, scale):
        return jax.random.normal(k, shape, f32) * scale

    return {
        'x': nrm(ks[0], (BATCH, SEQ, D_MODEL), 1.0),
        'norm_g': 1.0 + nrm(ks[1], (DEPTH, D_MODEL), 0.05),
        'w_in': nrm(ks[2], (DEPTH, D_MODEL, IN_COLS), D_MODEL ** -0.5),
        'b_in': nrm(ks[3], (DEPTH, IN_COLS), 0.02),
        'conv_w': nrm(ks[4], (DEPTH, HY_SHORT_CONV, 3 * HY_WIDTH), HY_SHORT_CONV ** -0.5),
        'conv_b': nrm(ks[5], (DEPTH, 3 * HY_WIDTH), 0.02),
        'hf_w1': nrm(ks[6], (DEPTH, HY_EMB_DIM, HY_FILTER_HIDDEN), HY_EMB_DIM ** -0.5),
        'hf_b1': nrm(ks[7], (DEPTH, HY_FILTER_HIDDEN), 0.02),
        'hf_w2': nrm(ks[8], (DEPTH, HY_FILTER_HIDDEN, HY_FILTER_HIDDEN), HY_FILTER_HIDDEN ** -0.5),
        'hf_b2': nrm(ks[9], (DEPTH, HY_FILTER_HIDDEN), 0.02),
        'hf_w3': nrm(ks[10], (DEPTH, HY_FILTER_HIDDEN, HY_FILTER_HIDDEN), HY_FILTER_HIDDEN ** -0.5),
        'hf_b3': nrm(ks[11], (DEPTH, HY_FILTER_HIDDEN), 0.02),
        'hf_w4': nrm(ks[12], (DEPTH, HY_FILTER_HIDDEN, HY_ORDER * 2 * HY_WIDTH), 0.02),
        'hf_freq': 1.0 + nrm(ks[13], (DEPTH, HY_FILTER_HIDDEN), 0.05),
        'hy_skip': nrm(ks[14], (DEPTH, HY_ORDER, HY_WIDTH), 0.1),
        'q_norm_g': 1.0 + nrm(ks[15], (DEPTH, HEAD_DIM), 0.05),
        'k_norm_g': 1.0 + nrm(ks[16], (DEPTH, HEAD_DIM), 0.05),
        'w_hy_out': nrm(ks[17], (DEPTH, HY_WIDTH, D_MODEL), HY_WIDTH ** -0.5),
        'w_at_out': nrm(ks[18], (DEPTH, AT_WIDTH, D_MODEL), AT_WIDTH ** -0.5),
        'w_out': nrm(ks[19], (DEPTH, D_MODEL, D_MODEL), D_MODEL ** -0.5),
    }


def reference(x, norm_g, w_in, b_in, conv_w, conv_b, hf_w1, hf_b1, hf_w2, hf_b2, hf_w3, hf_b3, hf_w4,
              hf_freq, hy_skip, q_norm_g, k_norm_g, w_hy_out, w_at_out, w_out):
    B, S, _ = x.shape
    f32 = jnp.float32
    slopes = _alibi_slopes(HEADS_PER_GROUP)
    o1 = 3 * HY_WIDTH
    o2 = o1 + HY_WIDTH
    o3 = o2 + 3 * AT_QKV
    o4 = o3 + AT_WIDTH
    o5 = o4 + D_MODEL
    for i in range(DEPTH):
        xn = _rmsnorm(x, norm_g[i])
        proj = xn @ w_in[i] + b_in[i]
        hy_in, hy_gate = proj[..., :o1], proj[..., o1:o2]
        qkv, at_gate = proj[..., o2:o3], proj[..., o3:o4]
        mg_h, mg_a = proj[..., o4:o5], proj[..., o5:]
        y_h = _hyena(hy_in, conv_w[i], conv_b[i], hf_w1[i], hf_b1[i], hf_w2[i], hf_b2[i], hf_w3[i], hf_b3[i],
                     hf_w4[i], hf_freq[i], hy_skip[i])
        u_h = (y_h * jax.nn.silu(hy_gate.astype(f32))).astype(x.dtype) @ w_hy_out[i]
        q, k, v = jnp.split(qkv, 3, axis=-1)
        q = _rmsnorm(q.reshape(B, S, N_GROUPS, HEADS_PER_GROUP, HEAD_DIM), q_norm_g[i]).astype(f32)
        k = _rmsnorm(k.reshape(B, S, N_GROUPS, HEADS_PER_GROUP, HEAD_DIM), k_norm_g[i]).astype(f32)
        v = v.reshape(B, S, N_GROUPS, HEADS_PER_GROUP, HEAD_DIM).astype(f32)
        o = _dilated_attention(q, k, v, slopes).transpose(0, 2, 1, 3).reshape(B, S, AT_WIDTH)
        u_a = (o * jax.nn.silu(at_gate.astype(f32))).astype(x.dtype) @ w_at_out[i]
        merged = jax.nn.sigmoid(mg_h) * u_h + jax.nn.sigmoid(mg_a) * u_a
        x = x + merged @ w_out[i]
    return x
```

```python
import functools
import math

import jax
import jax.numpy as jnp
import numpy as np
from jax import lax
from jax.experimental import pallas as pl
from jax.experimental.pallas import tpu as pltpu

D_MODEL = 1024
HY_WIDTH = 768
HY_ORDER = 2
HY_SHORT_CONV = 3
HY_EMB_DIM = 33
HY_FILTER_HIDDEN = 64
HY_FAST_DECAY = 0.3
HY_SLOW_DECAY = 1.5
HY_DECAY_TARGET = 1e-2
HY_MOD_SHIFT = 0.0
HEAD_DIM = 64
HEADS_PER_GROUP = 8
DILATED_GROUPS = ((128, 1), (512, 4), (2048, 16))
N_GROUPS = 3
AT_QKV = N_GROUPS * HEADS_PER_GROUP * HEAD_DIM
AT_WIDTH = HEADS_PER_GROUP * HEAD_DIM
NORM_EPS = 1e-6
NEG_INF = -1e30

O_HY = 0
O_HGATE = 3 * HY_WIDTH
O_QKV = O_HGATE + HY_WIDTH
O_AGATE = O_QKV + 3 * AT_QKV
O_MG = O_AGATE + AT_WIDTH
IN_COLS = O_MG + 2 * D_MODEL

LANES = 128
MXU_DIM = 256
VMEM_LIMIT = 56 * 1024 * 1024

HY_CT = 256
HY_RC = 512
HALO = 8
AT_RC = 512
EMB_PAD = 128
Q_TILE = 128
PAIR = 2 * HEAD_DIM

f32 = jnp.float32
bf16 = jnp.bfloat16


def _dot(a, b):
    return jnp.dot(a, b, preferred_element_type=f32)


def _const_spec(shape, index_map):
    return pl.BlockSpec(shape, index_map, pipeline_mode=pl.Buffered(1))


@functools.lru_cache(maxsize=None)
def _dft_table(L):
    n = 2 * L
    ft = (np.arange(L, dtype=np.int64)[:, None] * np.arange(L, dtype=np.int64)[None, :]) % n
    ang = ft.astype(np.float64) * (2.0 * np.pi / n)
    return np.concatenate([np.cos(ang), np.sin(ang)], axis=0).astype(np.float32)


@functools.lru_cache(maxsize=None)
def _filter_embedding(L):
    t = np.linspace(0.0, 1.0, L)[:, None]
    bands = (HY_EMB_DIM - 1) // 2
    w = 2.0 * np.pi * np.arange(L)[:, None] / L
    f = np.linspace(1e-4, bands - 1, bands)[None, :]
    z = np.concatenate([t, np.cos(f * w), -np.sin(f * w)], axis=-1)
    zp = np.zeros((L, EMB_PAD), np.float64)
    zp[:, :HY_EMB_DIM] = z
    return zp.astype(np.float32)


@functools.lru_cache(maxsize=None)
def _decay_rates():
    max_decay = math.log(HY_DECAY_TARGET) / HY_FAST_DECAY
    min_decay = math.log(HY_DECAY_TARGET) / HY_SLOW_DECAY
    return np.abs(np.linspace(min_decay, max_decay, HY_WIDTH))[None, :].astype(np.float32)


@functools.lru_cache(maxsize=None)
def _attn_bias(n, dilation, window):
    half = window // (2 * dilation)
    tq = min(Q_TILE, n)
    w = min(2 * Q_TILE, n)
    slopes = 2.0 ** (-8.0 * np.arange(1, HEADS_PER_GROUP + 1) / HEADS_PER_GROUP)
    offs = sorted({min(max(q0 - half, 0), n - w) - q0 for q0 in range(0, n, tq)}, reverse=True)
    out = np.zeros((len(offs), HEADS_PER_GROUP // 2, 2 * tq, w), np.float32)
    for ci, off in enumerate(offs):
        rel = np.arange(tq)[:, None] - (off + np.arange(w))[None, :]
        valid = np.abs(rel) <= half
        dist = dilation * np.abs(rel)
        for h in range(HEADS_PER_GROUP):
            b = np.where(valid, -slopes[h] * dist, NEG_INF)
            out[ci, h // 2, (h % 2) * tq:(h % 2 + 1) * tq] = b
    return out, tuple(-o for o in offs)


def _prenorm_kernel(x_ref, g_ref, o_ref):
    x = x_ref[...]
    ms = jnp.mean(x * x, axis=-1, keepdims=True)
    o_ref[...] = (x * lax.rsqrt(ms + NORM_EPS) * g_ref[...]).astype(o_ref.dtype)


def _prenorm(x2, g):
    rows = x2.shape[0]
    tm = 1024
    return pl.pallas_call(
        _prenorm_kernel,
        grid=(rows // tm,),
        in_specs=[pl.BlockSpec((tm, D_MODEL), lambda i: (i, 0)),
                  pl.BlockSpec((1, D_MODEL), lambda i: (0, 0))],
        out_specs=pl.BlockSpec((tm, D_MODEL), lambda i: (i, 0)),
        out_shape=jax.ShapeDtypeStruct((rows, D_MODEL), bf16),
        compiler_params=pltpu.CompilerParams(dimension_semantics=("arbitrary",)),
        name="prenorm",
    )(x2, g)


def _filters_kernel(z_ref, w1_ref, b1_ref, w2_ref, b2_ref, w3_ref, b3_ref, fr_ref, w4f_ref, w4b_ref,
                    t_ref, rate_ref, tab_ref, kr_ref, ki_ref, kn_ref, h3_ref):
    L = z_ref.shape[0]
    hi = lax.Precision.HIGHEST

    @pl.when((pl.program_id(0) == 0) & (pl.program_id(1) == 0))
    def _():
        fr = fr_ref[...]
        h = jnp.sin(fr * (jnp.dot(z_ref[...], w1_ref[...], precision=hi, preferred_element_type=f32)
                          + b1_ref[...]))
        h = jnp.sin(fr * (jnp.dot(h, w2_ref[...], precision=hi, preferred_element_type=f32) + b2_ref[...]))
        h = jnp.sin(fr * (jnp.dot(h, w3_ref[...], precision=hi, preferred_element_type=f32) + b3_ref[...]))
        h3_ref[...] = h

    h3 = h3_ref[...]
    decay = jnp.exp(-t_ref[...] * rate_ref[...]) + HY_MOD_SHIFT
    hf = jnp.dot(h3, w4f_ref[...], precision=hi, preferred_element_type=f32) * decay
    hb = jnp.dot(h3, w4b_ref[...], precision=hi, preferred_element_type=f32) * decay
    hb0 = hb[0:1, :]
    hs = hf + hb
    hd = hb - hf
    n = 2 * L
    kr = _dot(tab_ref[0:L, :], hs.astype(bf16)) - hb0
    ki = _dot(tab_ref[L:2 * L, :], hd.astype(bf16))
    row = lax.broadcasted_iota(jnp.int32, kr.shape, 0)
    wgt = jnp.where(row == 0, 1.0 / n, 2.0 / n).astype(f32)
    kr_ref[0] = kr * wgt
    ki_ref[0] = ki * wgt
    sgn = jnp.where((row & 1) == 1, -1.0, 1.0).astype(f32)
    kn_ref[0] = (jnp.sum(hs * sgn, axis=0, keepdims=True) - hb0) * (1.0 / n)


def _filters(L, tab, w1, b1, w2, b2, w3, b3, w4, freq):
    z = jnp.asarray(_filter_embedding(L))
    t = jnp.asarray(np.linspace(0.0, 1.0, L)[:, None].astype(np.float32))
    rate = jnp.asarray(_decay_rates())
    w1p = jnp.zeros((EMB_PAD, HY_FILTER_HIDDEN), f32).at[:HY_EMB_DIM].set(w1.astype(f32))
    nct = HY_WIDTH // HY_CT
    row = lambda a: a.astype(f32).reshape(1, -1)
    full = lambda shape: pl.BlockSpec(shape, lambda o, j: (0,) * len(shape))
    H = HY_FILTER_HIDDEN
    kspec = pl.BlockSpec((1, L, HY_CT), lambda o, j: (o, 0, j))
    return pl.pallas_call(
        _filters_kernel,
        grid=(HY_ORDER, nct),
        in_specs=[full((L, EMB_PAD)), full((EMB_PAD, H)), full((1, H)), full((H, H)), full((1, H)),
                  full((H, H)), full((1, H)), full((1, H)),
                  pl.BlockSpec((H, HY_CT), lambda o, j: (0, 2 * nct * o + j)),
                  pl.BlockSpec((H, HY_CT), lambda o, j: (0, 2 * nct * o + nct + j)),
                  full((L, 1)),
                  pl.BlockSpec((1, HY_CT), lambda o, j: (0, j)),
                  _const_spec((2 * L, L), lambda o, j: (0, 0))],
        out_specs=[kspec, kspec, pl.BlockSpec((1, 1, HY_CT), lambda o, j: (o, 0, j))],
        out_shape=[jax.ShapeDtypeStruct((HY_ORDER, L, HY_WIDTH), f32),
                   jax.ShapeDtypeStruct((HY_ORDER, L, HY_WIDTH), f32),
                   jax.ShapeDtypeStruct((HY_ORDER, 1, HY_WIDTH), f32)],
        scratch_shapes=[pltpu.VMEM((L, H), f32)],
        compiler_params=pltpu.CompilerParams(dimension_semantics=("arbitrary", "arbitrary"),
                                             vmem_limit_bytes=VMEM_LIMIT),
        name="hyena_filters",
    )(z, w1p, row(b1), w2.astype(f32), row(b2), w3.astype(f32), row(b3), row(freq),
      w4.astype(f32), w4.astype(f32), t, rate, tab)


def _hyena_kernel(xn_ref, wv_ref, wx1_ref, wx2_ref, wg_ref, bv_ref, bx1_ref, bx2_ref, bg_ref,
                  cwv_ref, cwx1_ref, cwx2_ref, cbv_ref, cbx1_ref, cbx2_ref,
                  tab_ref, kr_ref, ki_ref, kn_ref, skip_ref, o_ref,
                  z_ref, u_ref, ub_ref, pq_ref, x_ref):
    L = xn_ref.shape[1]
    chunks = [slice(r, r + HY_RC) for r in range(0, L, HY_RC)]
    row = lax.broadcasted_iota(jnp.int32, (HY_RC, HY_CT), 0)
    sgn = jnp.where((row & 1) == 1, -1.0, 1.0).astype(f32)
    z_ref[0:HALO] = jnp.zeros((HALO, HY_CT), f32)
    z_ref[L + HALO:L + 2 * HALO] = jnp.zeros((HALO, HY_CT), f32)

    def proj_conv(dst_ref, w_ref, b_ref, cw_ref, cb_ref):
        for c in chunks:
            z_ref[HALO + c.start:HALO + c.stop] = _dot(xn_ref[0, c, :], w_ref[...]) + b_ref[...]
        for c in chunks:
            r = HALO + c.start
            dst_ref[c] = (cb_ref[...] + z_ref[r - 1:r - 1 + HY_RC] * cw_ref[0:1, :]
                          + z_ref[r:r + HY_RC] * cw_ref[1:2, :] + z_ref[r + 1:r + 1 + HY_RC] * cw_ref[2:3, :])

    def long_conv(o):
        u_nyq = jnp.zeros((1, HY_CT), f32)
        for c in chunks:
            u = u_ref[c]
            ub_ref[c] = u.astype(bf16)
            u_nyq = u_nyq + jnp.sum(u * sgn, axis=0, keepdims=True)
        for c in chunks:
            s = slice(L + c.start, L + c.stop)
            a = _dot(tab_ref[c, :], ub_ref[...])
            b = _dot(tab_ref[s, :], ub_ref[...])
            kr, ki = kr_ref[o, c, :], ki_ref[o, c, :]
            pq_ref[c] = (a * kr + b * ki).astype(bf16)
            pq_ref[s] = (b * kr - a * ki).astype(bf16)
        nyq = u_nyq * kn_ref[o]
        for c in chunks:
            s = slice(L + c.start, L + c.stop)
            y = _dot(tab_ref[c, :], pq_ref[0:L]) + _dot(tab_ref[s, :], pq_ref[L:2 * L])
            y = y + sgn * nyq + u_ref[c] * skip_ref[o:o + 1, :]
            u_ref[c] = x_ref[c] * y

    proj_conv(u_ref, wv_ref, bv_ref, cwv_ref, cbv_ref)
    proj_conv(x_ref, wx1_ref, bx1_ref, cwx1_ref, cbx1_ref)
    long_conv(0)
    proj_conv(x_ref, wx2_ref, bx2_ref, cwx2_ref, cbx2_ref)
    long_conv(1)
    for c in chunks:
        g = _dot(xn_ref[0, c, :], wg_ref[...]) + bg_ref[...]
        o_ref[0, c, :] = (u_ref[c] * (g * jax.nn.sigmoid(g))).astype(o_ref.dtype)


def _hyena(xn3, w_in_b, b_in, conv_w, conv_b, tab, kr, ki, kn, skip):
    B, L, _ = xn3.shape
    nct = HY_WIDTH // HY_CT
    hg = O_HGATE // HY_CT

    def col(k):
        return lambda j, b: (0, k * nct + j)

    wspec = lambda k: _const_spec((D_MODEL, HY_CT), col(k))
    bspec = lambda k: pl.BlockSpec((1, HY_CT), col(k))
    cwspec = lambda k: pl.BlockSpec((HY_SHORT_CONV, HY_CT), col(k))
    kspec = _const_spec((HY_ORDER, L, HY_CT), lambda j, b: (0, 0, j))
    return pl.pallas_call(
        _hyena_kernel,
        grid=(nct, B),
        in_specs=[pl.BlockSpec((1, L, D_MODEL), lambda j, b: (b, 0, 0)),
                  wspec(0), wspec(1), wspec(2), _const_spec((D_MODEL, HY_CT), lambda j, b: (0, hg + j)),
                  bspec(0), bspec(1), bspec(2), pl.BlockSpec((1, HY_CT), lambda j, b: (0, hg + j)),
                  cwspec(0), cwspec(1), cwspec(2), bspec(0), bspec(1), bspec(2),
                  _const_spec((2 * L, L), lambda j, b: (0, 0)),
                  kspec, kspec,
                  pl.BlockSpec((HY_ORDER, 1, HY_CT), lambda j, b: (0, 0, j)),
                  pl.BlockSpec((HY_ORDER, HY_CT), lambda j, b: (0, j))],
        out_specs=pl.BlockSpec((1, L, HY_CT), lambda j, b: (b, 0, j)),
        out_shape=jax.ShapeDtypeStruct((B, L, HY_WIDTH), bf16),
        scratch_shapes=[pltpu.VMEM((L + 2 * HALO, HY_CT), f32), pltpu.VMEM((L, HY_CT), f32),
                        pltpu.VMEM((L, HY_CT), bf16), pltpu.VMEM((2 * L, HY_CT), bf16),
                        pltpu.VMEM((L, HY_CT), f32)],
        compiler_params=pltpu.CompilerParams(dimension_semantics=("arbitrary", "arbitrary"),
                                             vmem_limit_bytes=VMEM_LIMIT),
        name="hyena_mixer",
    )(xn3, w_in_b, w_in_b, w_in_b, w_in_b, b_in, b_in, b_in, b_in,
      conv_w, conv_w, conv_w, conv_b, conv_b, conv_b, tab, kr, ki, kn, skip)


def _attn_kernel(xn_ref, wq_ref, wk_ref, wv_ref, bq_ref, bk_ref, bv_ref, gq_ref, gk_ref, hsum_ref,
                 bias_ref, o_ref, lse_ref, qe_ref, qo_ref, ks_ref, vx_ref, *, n, cases):
    tq = min(Q_TILE, n)
    w = min(2 * Q_TILE, n)
    npair = HEADS_PER_GROUP // 2
    rc = min(AT_RC, n)

    def normed(x, w_ref, b_ref, g_ref):
        z = _dot(x, w_ref[...]) + b_ref[...]
        ssq = _dot((z * z).astype(bf16), hsum_ref[...])
        return z * lax.rsqrt(ssq * (1.0 / HEAD_DIM) + NORM_EPS) * g_ref[...]

    lane = lax.broadcasted_iota(jnp.int32, (rc, HEADS_PER_GROUP * HEAD_DIM), 1)
    even = (lane & HEAD_DIM) == 0
    for r in range(0, n, rc):
        rows = slice(r, r + rc)
        x = xn_ref[0, rows, :]
        q = normed(x, wq_ref, bq_ref, gq_ref) * (HEAD_DIM ** -0.5)
        qe_ref[rows] = jnp.where(even, q, 0.0).astype(bf16)
        qo_ref[rows] = jnp.where(even, 0.0, q).astype(bf16)
        ks_ref[rows] = normed(x, wk_ref, bk_ref, gk_ref).astype(bf16)
        v = (_dot(x, wv_ref[...]) + bv_ref[...]).astype(bf16)
        for p in range(npair):
            vx_ref[p, rows, 0:PAIR] = v[:, p * PAIR:(p + 1) * PAIR]
            vx_ref[p, rows, PAIR:2 * PAIR] = jnp.ones((rc, PAIR), bf16)

    lane_t = lax.broadcasted_iota(jnp.int32, (tq, PAIR), 1)
    first = lane_t < HEAD_DIM
    half_keys = (w - tq) // 2

    def tile(t, carry):
        q0 = pl.multiple_of(t * tq, tq)
        k0 = jnp.clip(q0 - half_keys, 0, n - w)
        case = (q0 - k0) // HEAD_DIM if len(cases) > 1 else 0
        k0 = pl.multiple_of(k0, HEAD_DIM)
        for p in range(npair):
            cols = slice(p * PAIR, (p + 1) * PAIR)
            qq = jnp.concatenate([qe_ref[pl.ds(q0, tq), cols], qo_ref[pl.ds(q0, tq), cols]], axis=0)
            kk = ks_ref[pl.ds(k0, w), cols]
            s = lax.dot_general(qq, kk, (((1,), (1,)), ((), ())), preferred_element_type=f32)
            s = s + bias_ref[case, p]
            m = jnp.max(s, axis=-1, keepdims=True)
            pr = jnp.exp(s - m).astype(bf16)
            ov = _dot(pr, vx_ref[p, pl.ds(k0, w), :])
            acc, den = ov[:, 0:PAIR], ov[:, PAIR:2 * PAIR]
            on = acc / den
            ls = m + jnp.log(den)
            o_ref[0, pl.ds(q0, tq), cols] = jnp.where(first, on[0:tq], on[tq:2 * tq])
            lse_ref[0, pl.ds(q0, tq), cols] = jnp.where(first, ls[0:tq], ls[tq:2 * tq])
        return carry

    lax.fori_loop(0, n // tq, tile, 0)


def _attention_group(xn3, g, w_in_b, b_in, gq, gk, hsum):
    B, L, _ = xn3.shape
    window, d = DILATED_GROUPS[g]
    n = L // d
    bias_np, cases = _attn_bias(n, d, window)
    assert cases == tuple(HEAD_DIM * i for i in range(len(cases)))
    bias = jnp.asarray(bias_np)
    xs = xn3.reshape(B, n, d * D_MODEL)
    gw = HEADS_PER_GROUP * HEAD_DIM
    qc, kc, vc = (O_QKV + i * AT_QKV + g * gw for i in range(3))

    def wspec(c0):
        return _const_spec((D_MODEL, gw), lambda b, r: (0, c0 // gw))

    def bspec(c0):
        return pl.BlockSpec((1, gw), lambda b, r: (0, c0 // gw))

    vec = pl.BlockSpec((1, gw), lambda b, r: (0, 0))
    ospec = pl.BlockSpec((1, n, gw), lambda b, r: (b, 0, r))
    out = jax.ShapeDtypeStruct((B, n, d * gw), f32)
    o, lse = pl.pallas_call(
        functools.partial(_attn_kernel, n=n, cases=cases),
        grid=(B, d),
        in_specs=[pl.BlockSpec((1, n, D_MODEL), lambda b, r: (b, 0, r)),
                  wspec(qc), wspec(kc), wspec(vc), bspec(qc), bspec(kc), bspec(vc), vec, vec,
                  _const_spec((gw, gw), lambda b, r: (0, 0)),
                  _const_spec(bias.shape, lambda b, r: (0, 0, 0, 0))],
        out_specs=[ospec, ospec],
        out_shape=[out, out],
        scratch_shapes=[pltpu.VMEM((n, gw), bf16), pltpu.VMEM((n, gw), bf16), pltpu.VMEM((n, gw), bf16),
                        pltpu.VMEM((HEADS_PER_GROUP // 2, n, 2 * PAIR), bf16)],
        compiler_params=pltpu.CompilerParams(dimension_semantics=("arbitrary", "arbitrary"),
                                             vmem_limit_bytes=VMEM_LIMIT),
        name=f"dilated_attention_g{g}",
    )(xs, w_in_b, w_in_b, w_in_b, b_in, b_in, b_in, gq, gk, hsum, bias)
    return o.reshape(B * L, gw), lse.reshape(B * L, gw)


def _final_kernel(x_ref, xn_ref, gh_ref, o0_ref, o1_ref, o2_ref, l0_ref, l1_ref, l2_ref,
                  wg_ref, bg_ref, why_ref, wat_ref, wout_ref, out_ref):
    l0, l1, l2 = l0_ref[...], l1_ref[...], l2_ref[...]
    m = jnp.maximum(jnp.maximum(l0, l1), l2)
    e0, e1, e2 = jnp.exp(l0 - m), jnp.exp(l1 - m), jnp.exp(l2 - m)
    o = (e0 * o0_ref[...] + e1 * o1_ref[...] + e2 * o2_ref[...]) / (e0 + e1 + e2)
    gates = _dot(xn_ref[...], wg_ref[...]) + bg_ref[...]
    ag = gates[:, 0:AT_WIDTH]
    mg_h = gates[:, AT_WIDTH:AT_WIDTH + D_MODEL]
    mg_a = gates[:, AT_WIDTH + D_MODEL:]
    u_a = _dot((o * (ag * jax.nn.sigmoid(ag))).astype(bf16), wat_ref[...])
    u_h = _dot(gh_ref[...], why_ref[...])
    merged = jax.nn.sigmoid(mg_h) * u_h + jax.nn.sigmoid(mg_a) * u_a
    out_ref[...] = x_ref[...] + _dot(merged.astype(bf16), wout_ref[...])


def _final(x2, xn2, gh2, os, ls, wg, bg, why, wat, wout):
    rows = x2.shape[0]
    tm = 256
    rspec = lambda c: pl.BlockSpec((tm, c), lambda i: (i, 0))
    cspec = lambda a: _const_spec(a.shape, lambda i: (0, 0))
    return pl.pallas_call(
        _final_kernel,
        grid=(rows // tm,),
        in_specs=[rspec(D_MODEL), rspec(D_MODEL), rspec(HY_WIDTH)] + [rspec(AT_WIDTH)] * 6
                 + [cspec(wg), cspec(bg), cspec(why), cspec(wat), cspec(wout)],
        out_specs=rspec(D_MODEL),
        out_shape=jax.ShapeDtypeStruct((rows, D_MODEL), f32),
        compiler_params=pltpu.CompilerParams(dimension_semantics=("arbitrary",),
                                             vmem_limit_bytes=VMEM_LIMIT),
        name="merge_output",
    )(x2, xn2, gh2, *os, *ls, wg, bg, why, wat, wout)


def _layer(x, norm_g, w_in, b_in, conv_w, conv_b, hf_w1, hf_b1, hf_w2, hf_b2, hf_w3, hf_b3, hf_w4,
           hf_freq, hy_skip, q_norm_g, k_norm_g, w_hy_out, w_at_out, w_out):
    B, L, D = x.shape
    x2 = x.reshape(B * L, D)
    tab = jnp.asarray(_dft_table(L)).astype(bf16)
    w_in_b = w_in.astype(bf16)
    b_in2 = b_in.astype(f32).reshape(1, IN_COLS)

    xn2 = _prenorm(x2, norm_g.astype(f32).reshape(1, D))
    xn3 = xn2.reshape(B, L, D)

    kr, ki, kn = _filters(L, tab, hf_w1, hf_b1, hf_w2, hf_b2, hf_w3, hf_b3, hf_w4, hf_freq)
    gh = _hyena(xn3, w_in_b, b_in2, conv_w.astype(f32), conv_b.astype(f32).reshape(1, -1),
                tab, kr, ki, kn, hy_skip.astype(f32))

    gq = jnp.tile(q_norm_g.astype(f32), HEADS_PER_GROUP).reshape(1, -1)
    gk = jnp.tile(k_norm_g.astype(f32), HEADS_PER_GROUP).reshape(1, -1)
    head = np.arange(AT_WIDTH) // HEAD_DIM
    hsum = jnp.asarray((head[:, None] == head[None, :]).astype(np.float32)).astype(bf16)
    os, ls = [], []
    for g in range(N_GROUPS):
        o, lse = _attention_group(xn3, g, w_in_b, b_in2, gq, gk, hsum)
        os.append(o)
        ls.append(lse)

    out = _final(x2, xn2, gh.reshape(B * L, HY_WIDTH), os, ls,
                 w_in_b[:, O_AGATE:], b_in2[:, O_AGATE:],
                 w_hy_out.astype(bf16), w_at_out.astype(bf16), w_out.astype(bf16))
    return out.reshape(B, L, D)


def kernel(x, norm_g, w_in, b_in, conv_w, conv_b, hf_w1, hf_b1, hf_w2, hf_b2, hf_w3, hf_b3, hf_w4,
           hf_freq, hy_skip, q_norm_g, k_norm_g, w_hy_out, w_at_out, w_out):
    depth = norm_g.shape[0]
    for i in range(depth):
        x = _layer(x, norm_g[i], w_in[i], b_in[i], conv_w[i], conv_b[i], hf_w1[i], hf_b1[i], hf_w2[i],
                   hf_b2[i], hf_w3[i], hf_b3[i], hf_w4[i], hf_freq[i], hy_skip[i], q_norm_g[i],
                   k_norm_g[i], w_hy_out[i], w_at_out[i], w_out[i])
    return x
```

```python
import functools
import math

import jax
import jax.numpy as jnp
import numpy as np
from jax import lax
from jax.experimental import pallas as pl
from jax.experimental.pallas import tpu as pltpu

D_MODEL = 1024
HY_WIDTH = 768
HY_ORDER = 2
HY_SHORT_CONV = 3
HY_EMB_DIM = 33
HY_FILTER_HIDDEN = 64
HY_FAST_DECAY = 0.3
HY_SLOW_DECAY = 1.5
HY_DECAY_TARGET = 1e-2
HY_MOD_SHIFT = 0.0
HEAD_DIM = 64
HEADS_PER_GROUP = 8
DILATED_GROUPS = ((128, 1), (512, 4), (2048, 16))
N_GROUPS = 3
AT_QKV = N_GROUPS * HEADS_PER_GROUP * HEAD_DIM
AT_WIDTH = HEADS_PER_GROUP * HEAD_DIM
NORM_EPS = 1e-6
NEG_INF = -1e30

O_HY = 0
O_HGATE = 3 * HY_WIDTH
O_QKV = O_HGATE + HY_WIDTH
O_AGATE = O_QKV + 3 * AT_QKV
O_MG = O_AGATE + AT_WIDTH
IN_COLS = O_MG + 2 * D_MODEL

LANES = 128
MXU_DIM = 256
VMEM_LIMIT = 56 * 1024 * 1024

HY_CT = 256
HY_RC = 512
HALO = 8
AT_RC = 512
EMB_PAD = 128
Q_TILE = 128
BAND_HALF = 64
GROUP_ORDER = (1, 2, 0)
PAIR = 2 * HEAD_DIM

f32 = jnp.float32
bf16 = jnp.bfloat16


def _dot(a, b):
    return jnp.dot(a, b, preferred_element_type=f32)


def _const_spec(shape, index_map):
    return pl.BlockSpec(shape, index_map, pipeline_mode=pl.Buffered(1))


@functools.lru_cache(maxsize=None)
def _dft_table(L):
    n = 2 * L
    ft = (np.arange(L, dtype=np.int64)[:, None] * np.arange(L, dtype=np.int64)[None, :]) % n
    ang = ft.astype(np.float64) * (2.0 * np.pi / n)
    return np.concatenate([np.cos(ang), np.sin(ang)], axis=0).astype(np.float32)


@functools.lru_cache(maxsize=None)
def _filter_embedding(L):
    t = np.linspace(0.0, 1.0, L)[:, None]
    bands = (HY_EMB_DIM - 1) // 2
    w = 2.0 * np.pi * np.arange(L)[:, None] / L
    f = np.linspace(1e-4, bands - 1, bands)[None, :]
    z = np.concatenate([t, np.cos(f * w), -np.sin(f * w)], axis=-1)
    zp = np.zeros((L, EMB_PAD), np.float64)
    zp[:, :HY_EMB_DIM] = z
    return zp.astype(np.float32)


@functools.lru_cache(maxsize=None)
def _decay_rates():
    max_decay = math.log(HY_DECAY_TARGET) / HY_FAST_DECAY
    min_decay = math.log(HY_DECAY_TARGET) / HY_SLOW_DECAY
    return np.abs(np.linspace(min_decay, max_decay, HY_WIDTH))[None, :].astype(np.float32)


def _alibi_slope(h):
    return 2.0 ** (-8.0 * (h + 1) / HEADS_PER_GROUP)


@functools.lru_cache(maxsize=None)
def _attn_dist(n, dilation, window):
    half = window // (2 * dilation)
    assert half == BAND_HALF
    tq = min(Q_TILE, n)
    w = min(2 * Q_TILE, n)
    masked = NEG_INF / _alibi_slope(HEADS_PER_GROUP - 1)
    offs = sorted({q0 - min(max(q0 - half, 0), n - w) for q0 in range(0, n, tq)})
    assert offs == [BAND_HALF * i for i in range(len(offs))]
    out = np.zeros((len(offs), tq, w), np.float32)
    for ci, off in enumerate(offs):
        rel = np.arange(tq)[:, None] + off - np.arange(w)[None, :]
        out[ci] = np.where(np.abs(rel) <= half, -dilation * np.abs(rel), masked)
    return out


def _prenorm_kernel(x_ref, g_ref, o_ref, ot_ref):
    x = x_ref[...]
    ms = jnp.mean(x * x, axis=-1, keepdims=True)
    xn = x * lax.rsqrt(ms + NORM_EPS) * g_ref[...]
    o_ref[...] = xn.astype(o_ref.dtype)
    for c in range(D_MODEL // LANES):
        ot_ref[0, c] = xn[:, c * LANES:(c + 1) * LANES]


def _prenorm(x2, g, B, L):
    rows = x2.shape[0]
    tm = 1024
    per_b = L // tm
    nt = D_MODEL // LANES
    return pl.pallas_call(
        _prenorm_kernel,
        grid=(rows // tm,),
        in_specs=[pl.BlockSpec((tm, D_MODEL), lambda i: (i, 0)),
                  pl.BlockSpec((1, D_MODEL), lambda i: (0, 0))],
        out_specs=[pl.BlockSpec((tm, D_MODEL), lambda i: (i, 0)),
                   pl.BlockSpec((1, nt, tm, LANES), lambda i: (i // per_b, 0, i % per_b, 0))],
        out_shape=[jax.ShapeDtypeStruct((rows, D_MODEL), bf16),
                   jax.ShapeDtypeStruct((B, nt, L, LANES), f32)],
        compiler_params=pltpu.CompilerParams(dimension_semantics=("arbitrary",),
                                             vmem_limit_bytes=VMEM_LIMIT),
        name="prenorm",
    )(x2, g)


def _filters_kernel(z_ref, w1_ref, b1_ref, w2_ref, b2_ref, w3_ref, b3_ref, fr_ref, w4f_ref, w4b_ref,
                    t_ref, rate_ref, tab_ref, kr_ref, ki_ref, kn_ref, h3_ref):
    L = z_ref.shape[0]
    hi = lax.Precision.HIGHEST

    @pl.when((pl.program_id(0) == 0) & (pl.program_id(1) == 0))
    def _():
        fr = fr_ref[...]
        h = jnp.sin(fr * (jnp.dot(z_ref[...], w1_ref[...], precision=hi, preferred_element_type=f32)
                          + b1_ref[...]))
        h = jnp.sin(fr * (jnp.dot(h, w2_ref[...], precision=hi, preferred_element_type=f32) + b2_ref[...]))
        h = jnp.sin(fr * (jnp.dot(h, w3_ref[...], precision=hi, preferred_element_type=f32) + b3_ref[...]))
        h3_ref[...] = h

    h3 = h3_ref[...]
    decay = jnp.exp(-t_ref[...] * rate_ref[...]) + HY_MOD_SHIFT
    hf = jnp.dot(h3, w4f_ref[...], precision=hi, preferred_element_type=f32) * decay
    hb = jnp.dot(h3, w4b_ref[...], precision=hi, preferred_element_type=f32) * decay
    hb0 = hb[0:1, :]
    hs = hf + hb
    hd = hb - hf
    n = 2 * L
    kr = _dot(tab_ref[0:L, :], hs.astype(bf16)) - hb0
    ki = _dot(tab_ref[L:2 * L, :], hd.astype(bf16))
    row = lax.broadcasted_iota(jnp.int32, kr.shape, 0)
    wgt = jnp.where(row == 0, 1.0 / n, 2.0 / n).astype(f32)
    kr_ref[0] = kr * wgt
    ki_ref[0] = ki * wgt
    sgn = jnp.where((row & 1) == 1, -1.0, 1.0).astype(f32)
    kn_ref[0] = (jnp.sum(hs * sgn, axis=0, keepdims=True) - hb0) * (1.0 / n)


def _filters(L, tab, w1, b1, w2, b2, w3, b3, w4, freq):
    z = jnp.asarray(_filter_embedding(L))
    t = jnp.asarray(np.linspace(0.0, 1.0, L)[:, None].astype(np.float32))
    rate = jnp.asarray(_decay_rates())
    w1p = jnp.zeros((EMB_PAD, HY_FILTER_HIDDEN), f32).at[:HY_EMB_DIM].set(w1.astype(f32))
    nct = HY_WIDTH // HY_CT
    row = lambda a: a.astype(f32).reshape(1, -1)
    full = lambda shape: pl.BlockSpec(shape, lambda o, j: (0,) * len(shape))
    H = HY_FILTER_HIDDEN
    kspec = pl.BlockSpec((1, L, HY_CT), lambda o, j: (o, 0, j))
    return pl.pallas_call(
        _filters_kernel,
        grid=(HY_ORDER, nct),
        in_specs=[full((L, EMB_PAD)), full((EMB_PAD, H)), full((1, H)), full((H, H)), full((1, H)),
                  full((H, H)), full((1, H)), full((1, H)),
                  pl.BlockSpec((H, HY_CT), lambda o, j: (0, 2 * nct * o + j)),
                  pl.BlockSpec((H, HY_CT), lambda o, j: (0, 2 * nct * o + nct + j)),
                  full((L, 1)),
                  pl.BlockSpec((1, HY_CT), lambda o, j: (0, j)),
                  _const_spec((2 * L, L), lambda o, j: (0, 0))],
        out_specs=[kspec, kspec, pl.BlockSpec((1, 1, HY_CT), lambda o, j: (o, 0, j))],
        out_shape=[jax.ShapeDtypeStruct((HY_ORDER, L, HY_WIDTH), f32),
                   jax.ShapeDtypeStruct((HY_ORDER, L, HY_WIDTH), f32),
                   jax.ShapeDtypeStruct((HY_ORDER, 1, HY_WIDTH), f32)],
        scratch_shapes=[pltpu.VMEM((L, H), f32)],
        compiler_params=pltpu.CompilerParams(dimension_semantics=("arbitrary", "arbitrary"),
                                             vmem_limit_bytes=VMEM_LIMIT),
        name="hyena_filters",
    )(z, w1p, row(b1), w2.astype(f32), row(b2), w3.astype(f32), row(b3), row(freq),
      w4.astype(f32), w4.astype(f32), t, rate, tab)


def _hyena_kernel(xn_ref, wv_ref, wx1_ref, wx2_ref, wg_ref, bv_ref, bx1_ref, bx2_ref, bg_ref,
                  cwv_ref, cwx1_ref, cwx2_ref, cbv_ref, cbx1_ref, cbx2_ref,
                  tab_ref, kr_ref, ki_ref, kn_ref, skip_ref, o_ref,
                  z_ref, u_ref, ub_ref, pq_ref, x_ref):
    L = xn_ref.shape[1]
    chunks = [slice(r, r + HY_RC) for r in range(0, L, HY_RC)]
    row = lax.broadcasted_iota(jnp.int32, (HY_RC, HY_CT), 0)
    sgn = jnp.where((row & 1) == 1, -1.0, 1.0).astype(f32)
    z_ref[0:HALO] = jnp.zeros((HALO, HY_CT), f32)
    z_ref[L + HALO:L + 2 * HALO] = jnp.zeros((HALO, HY_CT), f32)

    def proj_conv(dst_ref, w_ref, b_ref, cw_ref, cb_ref):
        for c in chunks:
            z_ref[HALO + c.start:HALO + c.stop] = _dot(xn_ref[0, c, :], w_ref[...]) + b_ref[...]
        for c in chunks:
            r = HALO + c.start
            dst_ref[c] = (cb_ref[...] + z_ref[r - 1:r - 1 + HY_RC] * cw_ref[0:1, :]
                          + z_ref[r:r + HY_RC] * cw_ref[1:2, :] + z_ref[r + 1:r + 1 + HY_RC] * cw_ref[2:3, :])

    def long_conv(o):
        u_nyq = jnp.zeros((1, HY_CT), f32)
        for c in chunks:
            u = u_ref[c]
            ub_ref[c] = u.astype(bf16)
            u_nyq = u_nyq + jnp.sum(u * sgn, axis=0, keepdims=True)
        for c in chunks:
            s = slice(L + c.start, L + c.stop)
            a = _dot(tab_ref[c, :], ub_ref[...])
            b = _dot(tab_ref[s, :], ub_ref[...])
            kr, ki = kr_ref[o, c, :], ki_ref[o, c, :]
            pq_ref[c] = (a * kr + b * ki).astype(bf16)
            pq_ref[s] = (b * kr - a * ki).astype(bf16)
        nyq = u_nyq * kn_ref[o]
        for c in chunks:
            s = slice(L + c.start, L + c.stop)
            y = _dot(tab_ref[c, :], pq_ref[0:L]) + _dot(tab_ref[s, :], pq_ref[L:2 * L])
            y = y + sgn * nyq + u_ref[c] * skip_ref[o:o + 1, :]
            u_ref[c] = x_ref[c] * y

    proj_conv(u_ref, wv_ref, bv_ref, cwv_ref, cbv_ref)
    proj_conv(x_ref, wx1_ref, bx1_ref, cwx1_ref, cbx1_ref)
    long_conv(0)
    proj_conv(x_ref, wx2_ref, bx2_ref, cwx2_ref, cbx2_ref)
    long_conv(1)
    for c in chunks:
        g = _dot(xn_ref[0, c, :], wg_ref[...]) + bg_ref[...]
        o_ref[0, c, :] = (u_ref[c] * (g * jax.nn.sigmoid(g))).astype(o_ref.dtype)


def _hyena(xn3, w_in_b, b_in, conv_w, conv_b, tab, kr, ki, kn, skip):
    B, L, _ = xn3.shape
    nct = HY_WIDTH // HY_CT
    hg = O_HGATE // HY_CT

    def col(k):
        return lambda j, b: (0, k * nct + j)

    wspec = lambda k: _const_spec((D_MODEL, HY_CT), col(k))
    bspec = lambda k: pl.BlockSpec((1, HY_CT), col(k))
    cwspec = lambda k: pl.BlockSpec((HY_SHORT_CONV, HY_CT), col(k))
    kspec = _const_spec((HY_ORDER, L, HY_CT), lambda j, b: (0, 0, j))
    return pl.pallas_call(
        _hyena_kernel,
        grid=(nct, B),
        in_specs=[pl.BlockSpec((1, L, D_MODEL), lambda j, b: (b, 0, 0)),
                  wspec(0), wspec(1), wspec(2), _const_spec((D_MODEL, HY_CT), lambda j, b: (0, hg + j)),
                  bspec(0), bspec(1), bspec(2), pl.BlockSpec((1, HY_CT), lambda j, b: (0, hg + j)),
                  cwspec(0), cwspec(1), cwspec(2), bspec(0), bspec(1), bspec(2),
                  _const_spec((2 * L, L), lambda j, b: (0, 0)),
                  kspec, kspec,
                  pl.BlockSpec((HY_ORDER, 1, HY_CT), lambda j, b: (0, 0, j)),
                  pl.BlockSpec((HY_ORDER, HY_CT), lambda j, b: (0, j))],
        out_specs=pl.BlockSpec((1, L, HY_CT), lambda j, b: (b, 0, j)),
        out_shape=jax.ShapeDtypeStruct((B, L, HY_WIDTH), bf16),
        scratch_shapes=[pltpu.VMEM((L + 2 * HALO, HY_CT), f32), pltpu.VMEM((L, HY_CT), f32),
                        pltpu.VMEM((L, HY_CT), bf16), pltpu.VMEM((2 * L, HY_CT), bf16),
                        pltpu.VMEM((L, HY_CT), f32)],
        compiler_params=pltpu.CompilerParams(dimension_semantics=("arbitrary", "arbitrary"),
                                             vmem_limit_bytes=VMEM_LIMIT),
        name="hyena_mixer",
    )(xn3, w_in_b, w_in_b, w_in_b, w_in_b, b_in, b_in, b_in, b_in,
      conv_w, conv_w, conv_w, conv_b, conv_b, conv_b, tab, kr, ki, kn, skip)


def _attn_kernel(xt_ref, wq_ref, wk_ref, wv_ref, bq_ref, bk_ref, bv_ref, wag_ref, bag_ref,
                 gq_ref, gk_ref, hsum_ref, d0_ref, d1_ref, d2_ref, o_ref,
                 xp_ref, qs_ref, ks_ref, vs_ref, acc_ref, mx_ref, den_ref):
    L = xt_ref.shape[2]
    gw = HEADS_PER_GROUP * HEAD_DIM
    npair = HEADS_PER_GROUP // 2
    dist_refs = (d0_ref, d1_ref, d2_ref)
    tq = Q_TILE
    first = lax.broadcasted_iota(jnp.int32, (tq, PAIR), 1) < HEAD_DIM
    nt_dims = (((1,), (1,)), ((), ()))

    def normed(x, w_ref, b_ref, g_ref, cols):
        z = _dot(x, w_ref[:, cols]) + b_ref[:, cols]
        ssq = _dot((z * z).astype(bf16), hsum_ref[...])
        return z * lax.rsqrt(ssq * (1.0 / HEAD_DIM) + NORM_EPS) * g_ref[...]

    for gi, g in enumerate(GROUP_ORDER):
        _, d = DILATED_GROUPS[g]
        n = L // d
        w = min(2 * tq, n)
        per_class = n // tq
        dist_ref = dist_refs[g]
        cols = slice(g * gw, (g + 1) * gw)

        def gather(r, carry, d=d, n=n):
            dst = pl.ds(pl.multiple_of(r * n, n), n)
            src = pl.ds(r, n, stride=d) if d > 1 else pl.ds(0, n)
            for c in range(D_MODEL // LANES):
                xp_ref[dst, c * LANES:(c + 1) * LANES] = xt_ref[0, c, src, :].astype(bf16)
            return carry

        if d > 1:
            lax.fori_loop(0, d, gather, 0)
        else:
            gather(0, 0)

        for r0 in range(0, L, AT_RC):
            rows = slice(r0, r0 + AT_RC)
            x = xp_ref[rows, :]
            qs_ref[rows] = (normed(x, wq_ref, bq_ref, gq_ref, cols) * (HEAD_DIM ** -0.5)).astype(bf16)
            ks_ref[rows] = normed(x, wk_ref, bk_ref, gk_ref, cols).astype(bf16)
            vs_ref[rows] = (_dot(x, wv_ref[:, cols]) + bv_ref[:, cols]).astype(bf16)

        def tile(idx, carry, gi=gi, d=d, n=n, w=w, per_class=per_class, dist_ref=dist_ref):
            r = idx // per_class
            t = idx % per_class
            q0 = pl.multiple_of(idx * tq, tq)
            koff = jnp.clip(t * tq - BAND_HALF, 0, n - w)
            dist = dist_ref[(t * tq - koff) // BAND_HALF]
            k0 = pl.multiple_of(r * n + koff, BAND_HALF)
            nat = pl.ds(t * tq * d + r, tq, stride=d) if d > 1 else pl.ds(q0, tq)
            for p in range(npair):
                pc = slice(p * PAIR, (p + 1) * PAIR)
                q = qs_ref[pl.ds(q0, tq), pc]
                zero = jnp.zeros_like(q)
                qq = jnp.concatenate([jnp.where(first, q, zero), jnp.where(first, zero, q)], axis=0)
                s = lax.dot_general(qq, ks_ref[pl.ds(k0, w), pc], nt_dims, preferred_element_type=f32)
                s = s + jnp.concatenate([dist * _alibi_slope(2 * p), dist * _alibi_slope(2 * p + 1)], axis=0)
                m = jnp.max(s, axis=-1, keepdims=True)
                pr = jnp.exp(s - m).astype(bf16)
                rhs = jnp.concatenate([vs_ref[pl.ds(k0, w), pc], jnp.ones((w, PAIR), bf16)], axis=1)
                ov = _dot(pr, rhs)
                num = jnp.where(first, ov[0:tq, 0:PAIR], ov[tq:2 * tq, 0:PAIR])
                den = jnp.where(first, ov[0:tq, PAIR:2 * PAIR], ov[tq:2 * tq, PAIR:2 * PAIR])
                mb = jnp.where(first, m[0:tq], m[tq:2 * tq])
                if gi == 0:
                    acc_ref[p, nat, :] = num
                    den_ref[p, nat, :] = den
                    mx_ref[p, nat, :] = mb
                else:
                    m_old = mx_ref[p, nat, :]
                    m_new = jnp.maximum(m_old, mb)
                    a = jnp.exp(m_old - m_new)
                    b = jnp.exp(mb - m_new)
                    acc_ref[p, nat, :] = acc_ref[p, nat, :] * a + num * b
                    den_ref[p, nat, :] = den_ref[p, nat, :] * a + den * b
                    mx_ref[p, nat, :] = m_new
            return carry

        lax.fori_loop(0, L // tq, tile, 0)

    for r0 in range(0, L, AT_RC):
        rows = slice(r0, r0 + AT_RC)
        ag = _dot(xp_ref[rows, :], wag_ref[...]) + bag_ref[...]
        o = jnp.concatenate([acc_ref[p, rows, :] / den_ref[p, rows, :] for p in range(npair)], axis=1)
        o_ref[0, rows, :] = (o * (ag * jax.nn.sigmoid(ag))).astype(o_ref.dtype)


def _attention(xt, w_in_b, b_in, gq, gk, hsum):
    B, nt, L, _ = xt.shape
    assert DILATED_GROUPS[GROUP_ORDER[-1]][1] == 1 and L % AT_RC == 0
    assert all(L % (Q_TILE * d) == 0 for _, d in DILATED_GROUPS)
    gw = HEADS_PER_GROUP * HEAD_DIM
    npair = HEADS_PER_GROUP // 2
    dists = [jnp.asarray(_attn_dist(L // d, d, window)) for window, d in DILATED_GROUPS]
    qb, agb = O_QKV // AT_QKV, O_AGATE // gw
    wspec = lambda k: _const_spec((D_MODEL, AT_QKV), lambda b: (0, qb + k))
    bspec = lambda k: pl.BlockSpec((1, AT_QKV), lambda b: (0, qb + k))
    vec = pl.BlockSpec((1, gw), lambda b: (0, 0))
    acc = pltpu.VMEM((npair, L, PAIR), f32)
    return pl.pallas_call(
        _attn_kernel,
        grid=(B,),
        in_specs=[_const_spec((1, nt, L, LANES), lambda b: (b, 0, 0, 0)),
                  wspec(0), wspec(1), wspec(2), bspec(0), bspec(1), bspec(2),
                  _const_spec((D_MODEL, gw), lambda b: (0, agb)), pl.BlockSpec((1, gw), lambda b: (0, agb)),
                  vec, vec, _const_spec((gw, gw), lambda b: (0, 0))]
                 + [_const_spec(t.shape, lambda b: (0, 0, 0)) for t in dists],
        out_specs=pl.BlockSpec((1, L, gw), lambda b: (b, 0, 0)),
        out_shape=jax.ShapeDtypeStruct((B, L, gw), bf16),
        scratch_shapes=[pltpu.VMEM((L, D_MODEL), bf16), pltpu.VMEM((L, gw), bf16), pltpu.VMEM((L, gw), bf16),
                        pltpu.VMEM((L, gw), bf16), acc, acc, acc],
        compiler_params=pltpu.CompilerParams(dimension_semantics=("arbitrary",),
                                             vmem_limit_bytes=VMEM_LIMIT),
        name="dilated_attention",
    )(xt, w_in_b, w_in_b, w_in_b, b_in, b_in, b_in, w_in_b, b_in, gq, gk, hsum, *dists)


def _final_kernel(x_ref, xn_ref, gh_ref, ga_ref, wg_ref, bg_ref, why_ref, wat_ref, wout_ref, out_ref):
    gates = _dot(xn_ref[...], wg_ref[...]) + bg_ref[...]
    u_h = _dot(gh_ref[...], why_ref[...])
    u_a = _dot(ga_ref[...], wat_ref[...])
    merged = jax.nn.sigmoid(gates[:, 0:D_MODEL]) * u_h + jax.nn.sigmoid(gates[:, D_MODEL:]) * u_a
    out_ref[...] = x_ref[...] + _dot(merged.astype(bf16), wout_ref[...])


def _final(x2, xn2, gh2, ga2, wg, bg, why, wat, wout):
    rows = x2.shape[0]
    tm = 512
    rspec = lambda c: pl.BlockSpec((tm, c), lambda i: (i, 0))
    cspec = lambda a: _const_spec(a.shape, lambda i: (0, 0))
    return pl.pallas_call(
        _final_kernel,
        grid=(rows // tm,),
        in_specs=[rspec(D_MODEL), rspec(D_MODEL), rspec(HY_WIDTH), rspec(AT_WIDTH),
                  cspec(wg), cspec(bg), cspec(why), cspec(wat), cspec(wout)],
        out_specs=rspec(D_MODEL),
        out_shape=jax.ShapeDtypeStruct((rows, D_MODEL), f32),
        compiler_params=pltpu.CompilerParams(dimension_semantics=("arbitrary",),
                                             vmem_limit_bytes=VMEM_LIMIT),
        name="merge_output",
    )(x2, xn2, gh2, ga2, wg, bg, why, wat, wout)


def _layer(x, norm_g, w_in, b_in, conv_w, conv_b, hf_w1, hf_b1, hf_w2, hf_b2, hf_w3, hf_b3, hf_w4,
           hf_freq, hy_skip, q_norm_g, k_norm_g, w_hy_out, w_at_out, w_out):
    B, L, D = x.shape
    x2 = x.reshape(B * L, D)
    tab = jnp.asarray(_dft_table(L)).astype(bf16)
    w_in_b = w_in.astype(bf16)
    b_in2 = b_in.astype(f32).reshape(1, IN_COLS)

    xn2, xt = _prenorm(x2, norm_g.astype(f32).reshape(1, D), B, L)
    xn3 = xn2.reshape(B, L, D)

    kr, ki, kn = _filters(L, tab, hf_w1, hf_b1, hf_w2, hf_b2, hf_w3, hf_b3, hf_w4, hf_freq)
    gh = _hyena(xn3, w_in_b, b_in2, conv_w.astype(f32), conv_b.astype(f32).reshape(1, -1),
                tab, kr, ki, kn, hy_skip.astype(f32))

    gq = jnp.tile(q_norm_g.astype(f32), HEADS_PER_GROUP).reshape(1, -1)
    gk = jnp.tile(k_norm_g.astype(f32), HEADS_PER_GROUP).reshape(1, -1)
    head = np.arange(AT_WIDTH) // HEAD_DIM
    hsum = jnp.asarray((head[:, None] == head[None, :]).astype(np.float32)).astype(bf16)
    ga = _attention(xt, w_in_b, b_in2, gq, gk, hsum)

    out = _final(x2, xn2, gh.reshape(B * L, HY_WIDTH), ga.reshape(B * L, AT_WIDTH),
                 w_in_b[:, O_MG:], b_in2[:, O_MG:],
                 w_hy_out.astype(bf16), w_at_out.astype(bf16), w_out.astype(bf16))
    return out.reshape(B, L, D)


def kernel(x, norm_g, w_in, b_in, conv_w, conv_b, hf_w1, hf_b1, hf_w2, hf_b2, hf_w3, hf_b3, hf_w4,
           hf_freq, hy_skip, q_norm_g, k_norm_g, w_hy_out, w_at_out, w_out):
    depth = norm_g.shape[0]
    for i in range(depth):
        x = _layer(x, norm_g[i], w_in[i], b_in[i], conv_w[i], conv_b[i], hf_w1[i], hf_b1[i], hf_w2[i],
                   hf_b2[i], hf_w3[i], hf_b3[i], hf_w4[i], hf_freq[i], hy_skip[i], q_norm_g[i],
                   k_norm_g[i], w_hy_out[i], w_at_out[i], w_out[i])
    return x
```

```python
import functools
import math

import jax
import jax.numpy as jnp
import numpy as np
from jax import lax
from jax.experimental import pallas as pl
from jax.experimental.pallas import tpu as pltpu

D_MODEL = 1024
HY_WIDTH = 768
HY_ORDER = 2
HY_SHORT_CONV = 3
HY_EMB_DIM = 33
HY_FILTER_HIDDEN = 64
HY_FAST_DECAY = 0.3
HY_SLOW_DECAY = 1.5
HY_DECAY_TARGET = 1e-2
HY_MOD_SHIFT = 0.0
HEAD_DIM = 64
HEADS_PER_GROUP = 8
DILATED_GROUPS = ((128, 1), (512, 4), (2048, 16))
N_GROUPS = 3
AT_QKV = N_GROUPS * HEADS_PER_GROUP * HEAD_DIM
AT_WIDTH = HEADS_PER_GROUP * HEAD_DIM
NORM_EPS = 1e-6
NEG_INF = -1e30

O_HY = 0
O_HGATE = 3 * HY_WIDTH
O_QKV = O_HGATE + HY_WIDTH
O_AGATE = O_QKV + 3 * AT_QKV
O_MG = O_AGATE + AT_WIDTH
IN_COLS = O_MG + 2 * D_MODEL

LANES = 128
MXU_DIM = 256
VMEM_LIMIT = 56 * 1024 * 1024

HY_CT = 256
HY_RC = 512
HY_PH = 2
HALO = 8
AT_RC = 512
EMB_PAD = 128
Q_TILE = 128
BAND_HALF = 64
GROUP_ORDER = (1, 2, 0)
PAIR = 2 * HEAD_DIM

f32 = jnp.float32
bf16 = jnp.bfloat16


def _dot(a, b):
    return jnp.dot(a, b, preferred_element_type=f32)


def _const_spec(shape, index_map):
    return pl.BlockSpec(shape, index_map, pipeline_mode=pl.Buffered(1))


@functools.lru_cache(maxsize=None)
def _dft_tables(L):
    n = 2 * L
    f = np.arange(L // 2, dtype=np.int64)[:, None]
    m = np.arange(L // 2, dtype=np.int64)[None, :]
    ang_e = ((f * (2 * m)) % n).astype(np.float64) * (2.0 * np.pi / n)
    ang_o = ((f * (2 * m + 1)) % n).astype(np.float64) * (2.0 * np.pi / n)
    ce, se, co, so = np.cos(ang_e), np.sin(ang_e), np.cos(ang_o), np.sin(ang_o)
    return np.stack([ce, se, co, so, co.T, so.T]).astype(np.float32)


def _phase_major(a, L):
    return np.concatenate([a[p::HY_PH] for p in range(HY_PH)], axis=0)


@functools.lru_cache(maxsize=None)
def _filter_embedding(L):
    t = np.linspace(0.0, 1.0, L)[:, None]
    bands = (HY_EMB_DIM - 1) // 2
    w = 2.0 * np.pi * np.arange(L)[:, None] / L
    f = np.linspace(1e-4, bands - 1, bands)[None, :]
    z = np.concatenate([t, np.cos(f * w), -np.sin(f * w)], axis=-1)
    zp = np.zeros((L, EMB_PAD), np.float64)
    zp[:, :HY_EMB_DIM] = z
    return zp.astype(np.float32)


@functools.lru_cache(maxsize=None)
def _decay_rates():
    max_decay = math.log(HY_DECAY_TARGET) / HY_FAST_DECAY
    min_decay = math.log(HY_DECAY_TARGET) / HY_SLOW_DECAY
    return np.abs(np.linspace(min_decay, max_decay, HY_WIDTH))[None, :].astype(np.float32)


def _alibi_slope(h):
    return 2.0 ** (-8.0 * (h + 1) / HEADS_PER_GROUP)


@functools.lru_cache(maxsize=None)
def _attn_dist(n, dilation, window):
    half = window // (2 * dilation)
    assert half == BAND_HALF
    tq = min(Q_TILE, n)
    w = min(2 * Q_TILE, n)
    masked = NEG_INF / _alibi_slope(HEADS_PER_GROUP - 1)
    offs = sorted({q0 - min(max(q0 - half, 0), n - w) for q0 in range(0, n, tq)})
    assert offs == [BAND_HALF * i for i in range(len(offs))]
    out = np.zeros((len(offs), tq, w), np.float32)
    for ci, off in enumerate(offs):
        rel = np.arange(tq)[:, None] + off - np.arange(w)[None, :]
        out[ci] = np.where(np.abs(rel) <= half, -dilation * np.abs(rel), masked)
    return out


def _prenorm_kernel(x_ref, g_ref, o_ref, ot_ref):
    x = x_ref[...]
    ms = jnp.mean(x * x, axis=-1, keepdims=True)
    xn = x * lax.rsqrt(ms + NORM_EPS) * g_ref[...]
    o_ref[...] = xn.astype(o_ref.dtype)
    for c in range(D_MODEL // LANES):
        ot_ref[0, c] = xn[:, c * LANES:(c + 1) * LANES]


def _prenorm(x2, g, B, L):
    rows = x2.shape[0]
    tm = 1024
    per_b = L // tm
    nt = D_MODEL // LANES
    return pl.pallas_call(
        _prenorm_kernel,
        grid=(rows // tm,),
        in_specs=[pl.BlockSpec((tm, D_MODEL), lambda i: (i, 0)),
                  pl.BlockSpec((1, D_MODEL), lambda i: (0, 0))],
        out_specs=[pl.BlockSpec((tm, D_MODEL), lambda i: (i, 0)),
                   pl.BlockSpec((1, nt, tm, LANES), lambda i: (i // per_b, 0, i % per_b, 0))],
        out_shape=[jax.ShapeDtypeStruct((rows, D_MODEL), bf16),
                   jax.ShapeDtypeStruct((B, nt, L, LANES), f32)],
        compiler_params=pltpu.CompilerParams(dimension_semantics=("arbitrary",),
                                             vmem_limit_bytes=VMEM_LIMIT),
        name="prenorm",
    )(x2, g)


def _filters_kernel(z_ref, w1_ref, b1_ref, w2_ref, b2_ref, w3_ref, b3_ref, fr_ref, w4f_ref, w4b_ref,
                    t_ref, rate_ref, tab_ref, kr_ref, ki_ref, ks_ref, h3_ref):
    L = z_ref.shape[0]
    M = L // HY_PH
    hi = lax.Precision.HIGHEST

    @pl.when((pl.program_id(0) == 0) & (pl.program_id(1) == 0))
    def _():
        fr = fr_ref[...]
        h = jnp.sin(fr * (jnp.dot(z_ref[...], w1_ref[...], precision=hi, preferred_element_type=f32)
                          + b1_ref[...]))
        h = jnp.sin(fr * (jnp.dot(h, w2_ref[...], precision=hi, preferred_element_type=f32) + b2_ref[...]))
        h = jnp.sin(fr * (jnp.dot(h, w3_ref[...], precision=hi, preferred_element_type=f32) + b3_ref[...]))
        h3_ref[...] = h

    h3 = h3_ref[...]
    decay = jnp.exp(-t_ref[...] * rate_ref[...]) + HY_MOD_SHIFT
    hf = jnp.dot(h3, w4f_ref[...], precision=hi, preferred_element_type=f32) * decay
    hb = jnp.dot(h3, w4b_ref[...], precision=hi, preferred_element_type=f32) * decay
    hb0 = hb[0:1, :]
    hs = hf + hb
    hd = hb - hf
    n = 2 * L
    a0 = _dot(tab_ref[0], hs[0:M].astype(bf16))
    a1 = _dot(tab_ref[2], hs[M:L].astype(bf16))
    b0 = _dot(tab_ref[1], hd[0:M].astype(bf16))
    b1 = _dot(tab_ref[3], hd[M:L].astype(bf16))
    row = lax.broadcasted_iota(jnp.int32, a0.shape, 0)
    wgt = jnp.where(row == 0, 1.0 / n, 2.0 / n).astype(f32)
    kr_ref[0, 0] = (a0 + a1 - hb0) * wgt
    kr_ref[0, 1] = (a0 - a1 - hb0) * wgt
    ki_ref[0, 0] = (b0 + b1) * wgt
    ki_ref[0, 1] = (b1 - b0) * wgt
    sgn = jnp.where((row & 1) == 1, -1.0, 1.0).astype(f32)
    ks_ref[0, 0:1, :] = (jnp.sum(hs[0:M] * sgn, axis=0, keepdims=True) - hb0) * (2.0 / n)
    ks_ref[0, 1:2, :] = jnp.sum(hd[M:L] * sgn, axis=0, keepdims=True) * (2.0 / n)


def _filters(L, tab, w1, b1, w2, b2, w3, b3, w4, freq):
    M = L // HY_PH
    z = jnp.asarray(_phase_major(_filter_embedding(L), L))
    t = jnp.asarray(_phase_major(np.linspace(0.0, 1.0, L)[:, None].astype(np.float32), L))
    rate = jnp.asarray(_decay_rates())
    w1p = jnp.zeros((EMB_PAD, HY_FILTER_HIDDEN), f32).at[:HY_EMB_DIM].set(w1.astype(f32))
    nct = HY_WIDTH // HY_CT
    row = lambda a: a.astype(f32).reshape(1, -1)
    full = lambda shape: pl.BlockSpec(shape, lambda o, j: (0,) * len(shape))
    H = HY_FILTER_HIDDEN
    kspec = pl.BlockSpec((1, 2, M, HY_CT), lambda o, j: (o, 0, 0, j))
    kshape = jax.ShapeDtypeStruct((HY_ORDER, 2, M, HY_WIDTH), f32)
    return pl.pallas_call(
        _filters_kernel,
        grid=(HY_ORDER, nct),
        in_specs=[full((L, EMB_PAD)), full((EMB_PAD, H)), full((1, H)), full((H, H)), full((1, H)),
                  full((H, H)), full((1, H)), full((1, H)),
                  pl.BlockSpec((H, HY_CT), lambda o, j: (0, 2 * nct * o + j)),
                  pl.BlockSpec((H, HY_CT), lambda o, j: (0, 2 * nct * o + nct + j)),
                  full((L, 1)),
                  pl.BlockSpec((1, HY_CT), lambda o, j: (0, j)),
                  _const_spec(tab.shape, lambda o, j: (0, 0, 0))],
        out_specs=[kspec, kspec, pl.BlockSpec((1, 2, HY_CT), lambda o, j: (o, 0, j))],
        out_shape=[kshape, kshape, jax.ShapeDtypeStruct((HY_ORDER, 2, HY_WIDTH), f32)],
        scratch_shapes=[pltpu.VMEM((L, H), f32)],
        compiler_params=pltpu.CompilerParams(dimension_semantics=("arbitrary", "arbitrary"),
                                             vmem_limit_bytes=VMEM_LIMIT),
        name="hyena_filters",
    )(z, w1p, row(b1), w2.astype(f32), row(b2), w3.astype(f32), row(b3), row(freq),
      w4.astype(f32), w4.astype(f32), t, rate, tab)


def _hyena_kernel(xn_ref, wv_ref, wx1_ref, wx2_ref, wg_ref, bv_ref, bx1_ref, bx2_ref, bg_ref,
                  cwv_ref, cwx1_ref, cwx2_ref, cbv_ref, cbx1_ref, cbx2_ref,
                  tab_ref, kr_ref, ki_ref, ks_ref, skip_ref, o_ref,
                  z_ref, u_ref, ub_ref, pq_ref, x_ref):
    L = xn_ref.shape[1]
    M = L // HY_PH
    nlt = HY_CT // LANES
    row_chunks = [slice(r, r + HY_RC) for r in range(0, L, HY_RC)]
    chunks = [slice(r, r + HY_RC) for r in range(0, M, HY_RC)]
    row = lax.broadcasted_iota(jnp.int32, (HY_RC, HY_CT), 0)
    sgn = jnp.where((row & 1) == 1, -1.0, 1.0).astype(f32)
    for lt in range(nlt):
        z_ref[lt, 0:HALO] = jnp.zeros((HALO, LANES), f32)
        z_ref[lt, L + HALO:L + 2 * HALO] = jnp.zeros((HALO, LANES), f32)

    def stage_natural(val, c):
        for lt in range(nlt):
            z_ref[lt, HALO + c.start:HALO + c.stop, :] = val[:, lt * LANES:(lt + 1) * LANES]

    def phase_rows(p, c, shift=0):
        src = pl.ds(HALO + p + shift + HY_PH * c.start, HY_RC, stride=HY_PH)
        return jnp.concatenate([z_ref[lt, src, :] for lt in range(nlt)], axis=1)

    def proj_conv(dst_ref, w_ref, b_ref, cw_ref, cb_ref):
        for c in row_chunks:
            stage_natural(_dot(xn_ref[0, c, :], w_ref[...]) + b_ref[...], c)
        for p in range(HY_PH):
            for c in chunks:
                dst_ref[p, c] = (cb_ref[...] + phase_rows(p, c, -1) * cw_ref[0:1, :]
                                 + phase_rows(p, c) * cw_ref[1:2, :] + phase_rows(p, c, 1) * cw_ref[2:3, :])

    def long_conv(o):
        mid = [jnp.zeros((1, HY_CT), f32) for _ in range(HY_PH)]
        for p in range(HY_PH):
            for c in chunks:
                u = u_ref[p, c]
                ub_ref[p, c] = u.astype(bf16)
                mid[p] = mid[p] + jnp.sum(u * sgn, axis=0, keepdims=True)
        for c in chunks:
            a0 = _dot(tab_ref[0, c, :], ub_ref[0])
            b0 = _dot(tab_ref[1, c, :], ub_ref[0])
            a1 = _dot(tab_ref[2, c, :], ub_ref[1])
            b1 = _dot(tab_ref[3, c, :], ub_ref[1])
            al, ah, bl, bh = a0 + a1, a0 - a1, b0 + b1, b1 - b0
            krl, kil, krh, kih = kr_ref[o, 0, c, :], ki_ref[o, 0, c, :], kr_ref[o, 1, c, :], ki_ref[o, 1, c, :]
            p_lo, q_lo = al * krl + bl * kil, bl * krl - al * kil
            p_hi, q_hi = ah * krh + bh * kih, bh * krh - ah * kih
            pq_ref[0, c] = (p_lo + p_hi).astype(bf16)
            pq_ref[1, c] = (q_lo - q_hi).astype(bf16)
            pq_ref[2, c] = (p_lo - p_hi).astype(bf16)
            pq_ref[3, c] = (q_lo + q_hi).astype(bf16)
        kr_mid, ki_mid = ks_ref[o, 0:1, :], ks_ref[o, 1:2, :]
        p_mid = mid[0] * kr_mid + mid[1] * ki_mid
        q_mid = mid[1] * kr_mid - mid[0] * ki_mid
        skip = skip_ref[o:o + 1, :]
        for c in chunks:
            ye = _dot(tab_ref[0, c, :], pq_ref[0]) + _dot(tab_ref[1, c, :], pq_ref[1])
            u_ref[0, c] = x_ref[0, c] * (ye + sgn * p_mid + u_ref[0, c] * skip)
            yo = _dot(tab_ref[4, c, :], pq_ref[2]) + _dot(tab_ref[5, c, :], pq_ref[3])
            u_ref[1, c] = x_ref[1, c] * (yo + sgn * q_mid + u_ref[1, c] * skip)

    proj_conv(u_ref, wv_ref, bv_ref, cwv_ref, cbv_ref)
    proj_conv(x_ref, wx1_ref, bx1_ref, cwx1_ref, cbx1_ref)
    long_conv(0)
    proj_conv(x_ref, wx2_ref, bx2_ref, cwx2_ref, cbx2_ref)
    long_conv(1)
    for p in range(HY_PH):
        for c in chunks:
            dst = pl.ds(HALO + p + HY_PH * c.start, HY_RC, stride=HY_PH)
            y = u_ref[p, c]
            for lt in range(nlt):
                z_ref[lt, dst, :] = y[:, lt * LANES:(lt + 1) * LANES]
    for c in row_chunks:
        y = jnp.concatenate([z_ref[lt, HALO + c.start:HALO + c.stop, :] for lt in range(nlt)], axis=1)
        g = _dot(xn_ref[0, c, :], wg_ref[...]) + bg_ref[...]
        o_ref[0, c, :] = (y * (g * jax.nn.sigmoid(g))).astype(o_ref.dtype)


def _hyena(xn3, w_in_b, b_in, conv_w, conv_b, tab, kr, ki, ks, skip):
    B, L, _ = xn3.shape
    M = L // HY_PH
    nct = HY_WIDTH // HY_CT
    hg = O_HGATE // HY_CT

    def col(k):
        return lambda j, b: (0, k * nct + j)

    wspec = lambda k: _const_spec((D_MODEL, HY_CT), col(k))
    bspec = lambda k: pl.BlockSpec((1, HY_CT), col(k))
    cwspec = lambda k: pl.BlockSpec((HY_SHORT_CONV, HY_CT), col(k))
    kspec = _const_spec((HY_ORDER, 2, M, HY_CT), lambda j, b: (0, 0, 0, j))
    return pl.pallas_call(
        _hyena_kernel,
        grid=(nct, B),
        in_specs=[pl.BlockSpec((1, L, D_MODEL), lambda j, b: (b, 0, 0)),
                  wspec(0), wspec(1), wspec(2), _const_spec((D_MODEL, HY_CT), lambda j, b: (0, hg + j)),
                  bspec(0), bspec(1), bspec(2), pl.BlockSpec((1, HY_CT), lambda j, b: (0, hg + j)),
                  cwspec(0), cwspec(1), cwspec(2), bspec(0), bspec(1), bspec(2),
                  _const_spec(tab.shape, lambda j, b: (0, 0, 0)),
                  kspec, kspec,
                  pl.BlockSpec((HY_ORDER, 2, HY_CT), lambda j, b: (0, 0, j)),
                  pl.BlockSpec((HY_ORDER, HY_CT), lambda j, b: (0, j))],
        out_specs=pl.BlockSpec((1, L, HY_CT), lambda j, b: (b, 0, j)),
        out_shape=jax.ShapeDtypeStruct((B, L, HY_WIDTH), bf16),
        scratch_shapes=[pltpu.VMEM((HY_CT // LANES, L + 2 * HALO, LANES), f32), pltpu.VMEM((HY_PH, M, HY_CT), f32),
                        pltpu.VMEM((HY_PH, M, HY_CT), bf16), pltpu.VMEM((2 * HY_PH, M, HY_CT), bf16),
                        pltpu.VMEM((HY_PH, M, HY_CT), f32)],
        compiler_params=pltpu.CompilerParams(dimension_semantics=("arbitrary", "arbitrary"),
                                             vmem_limit_bytes=VMEM_LIMIT),
        name="hyena_mixer",
    )(xn3, w_in_b, w_in_b, w_in_b, w_in_b, b_in, b_in, b_in, b_in,
      conv_w, conv_w, conv_w, conv_b, conv_b, conv_b, tab, kr, ki, ks, skip)


def _attn_kernel(xt_ref, wq_ref, wk_ref, wv_ref, bq_ref, bk_ref, bv_ref, wag_ref, bag_ref,
                 gq_ref, gk_ref, hsum_ref, d0_ref, d1_ref, d2_ref, o_ref,
                 xp_ref, qs_ref, ks_ref, vs_ref, acc_ref, mx_ref, den_ref):
    L = xt_ref.shape[2]
    gw = HEADS_PER_GROUP * HEAD_DIM
    npair = HEADS_PER_GROUP // 2
    dist_refs = (d0_ref, d1_ref, d2_ref)
    tq = Q_TILE
    first = lax.broadcasted_iota(jnp.int32, (tq, PAIR), 1) < HEAD_DIM
    nt_dims = (((1,), (1,)), ((), ()))

    def normed(x, w_ref, b_ref, g_ref, cols):
        z = _dot(x, w_ref[:, cols]) + b_ref[:, cols]
        ssq = _dot((z * z).astype(bf16), hsum_ref[...])
        return z * lax.rsqrt(ssq * (1.0 / HEAD_DIM) + NORM_EPS) * g_ref[...]

    for gi, g in enumerate(GROUP_ORDER):
        _, d = DILATED_GROUPS[g]
        n = L // d
        w = min(2 * tq, n)
        per_class = n // tq
        dist_ref = dist_refs[g]
        cols = slice(g * gw, (g + 1) * gw)

        def gather(r, carry, d=d, n=n):
            dst = pl.ds(pl.multiple_of(r * n, n), n)
            src = pl.ds(r, n, stride=d) if d > 1 else pl.ds(0, n)
            for c in range(D_MODEL // LANES):
                xp_ref[dst, c * LANES:(c + 1) * LANES] = xt_ref[0, c, src, :].astype(bf16)
            return carry

        if d > 1:
            lax.fori_loop(0, d, gather, 0)
        else:
            gather(0, 0)

        for r0 in range(0, L, AT_RC):
            rows = slice(r0, r0 + AT_RC)
            x = xp_ref[rows, :]
            qs_ref[rows] = (normed(x, wq_ref, bq_ref, gq_ref, cols) * (HEAD_DIM ** -0.5)).astype(bf16)
            ks_ref[rows] = normed(x, wk_ref, bk_ref, gk_ref, cols).astype(bf16)
            vs_ref[rows] = (_dot(x, wv_ref[:, cols]) + bv_ref[:, cols]).astype(bf16)

        def tile(idx, carry, gi=gi, d=d, n=n, w=w, per_class=per_class, dist_ref=dist_ref):
            r = idx // per_class
            t = idx % per_class
            q0 = pl.multiple_of(idx * tq, tq)
            koff = jnp.clip(t * tq - BAND_HALF, 0, n - w)
            dist = dist_ref[(t * tq - koff) // BAND_HALF]
            k0 = pl.multiple_of(r * n + koff, BAND_HALF)
            nat = pl.ds(t * tq * d + r, tq, stride=d) if d > 1 else pl.ds(q0, tq)
            for p in range(npair):
                pc = slice(p * PAIR, (p + 1) * PAIR)
                q = qs_ref[pl.ds(q0, tq), pc]
                zero = jnp.zeros_like(q)
                qq = jnp.concatenate([jnp.where(first, q, zero), jnp.where(first, zero, q)], axis=0)
                s = lax.dot_general(qq, ks_ref[pl.ds(k0, w), pc], nt_dims, preferred_element_type=f32)
                s = s + jnp.concatenate([dist * _alibi_slope(2 * p), dist * _alibi_slope(2 * p + 1)], axis=0)
                m = jnp.max(s, axis=-1, keepdims=True)
                pr = jnp.exp(s - m).astype(bf16)
                rhs = jnp.concatenate([vs_ref[pl.ds(k0, w), pc], jnp.ones((w, PAIR), bf16)], axis=1)
                ov = _dot(pr, rhs)
                num = jnp.where(first, ov[0:tq, 0:PAIR], ov[tq:2 * tq, 0:PAIR])
                den = jnp.where(first, ov[0:tq, PAIR:2 * PAIR], ov[tq:2 * tq, PAIR:2 * PAIR])
                mb = jnp.where(first, m[0:tq], m[tq:2 * tq])
                if gi == 0:
                    acc_ref[p, nat, :] = num
                    den_ref[p, nat, :] = den
                    mx_ref[p, nat, :] = mb
                else:
                    m_old = mx_ref[p, nat, :]
                    m_new = jnp.maximum(m_old, mb)
                    a = jnp.exp(m_old - m_new)
                    b = jnp.exp(mb - m_new)
                    acc_ref[p, nat, :] = acc_ref[p, nat, :] * a + num * b
                    den_ref[p, nat, :] = den_ref[p, nat, :] * a + den * b
                    mx_ref[p, nat, :] = m_new
            return carry

        lax.fori_loop(0, L // tq, tile, 0)

    for r0 in range(0, L, AT_RC):
        rows = slice(r0, r0 + AT_RC)
        ag = _dot(xp_ref[rows, :], wag_ref[...]) + bag_ref[...]
        o = jnp.concatenate([acc_ref[p, rows, :] / den_ref[p, rows, :] for p in range(npair)], axis=1)
        o_ref[0, rows, :] = (o * (ag * jax.nn.sigmoid(ag))).astype(o_ref.dtype)


def _attention(xt, w_in_b, b_in, gq, gk, hsum):
    B, nt, L, _ = xt.shape
    assert DILATED_GROUPS[GROUP_ORDER[-1]][1] == 1 and L % AT_RC == 0
    assert all(L % (Q_TILE * d) == 0 for _, d in DILATED_GROUPS)
    gw = HEADS_PER_GROUP * HEAD_DIM
    npair = HEADS_PER_GROUP // 2
    dists = [jnp.asarray(_attn_dist(L // d, d, window)) for window, d in DILATED_GROUPS]
    qb, agb = O_QKV // AT_QKV, O_AGATE // gw
    wspec = lambda k: _const_spec((D_MODEL, AT_QKV), lambda b: (0, qb + k))
    bspec = lambda k: pl.BlockSpec((1, AT_QKV), lambda b: (0, qb + k))
    vec = pl.BlockSpec((1, gw), lambda b: (0, 0))
    acc = pltpu.VMEM((npair, L, PAIR), f32)
    return pl.pallas_call(
        _attn_kernel,
        grid=(B,),
        in_specs=[_const_spec((1, nt, L, LANES), lambda b: (b, 0, 0, 0)),
                  wspec(0), wspec(1), wspec(2), bspec(0), bspec(1), bspec(2),
                  _const_spec((D_MODEL, gw), lambda b: (0, agb)), pl.BlockSpec((1, gw), lambda b: (0, agb)),
                  vec, vec, _const_spec((gw, gw), lambda b: (0, 0))]
                 + [_const_spec(t.shape, lambda b: (0, 0, 0)) for t in dists],
        out_specs=pl.BlockSpec((1, L, gw), lambda b: (b, 0, 0)),
        out_shape=jax.ShapeDtypeStruct((B, L, gw), bf16),
        scratch_shapes=[pltpu.VMEM((L, D_MODEL), bf16), pltpu.VMEM((L, gw), bf16), pltpu.VMEM((L, gw), bf16),
                        pltpu.VMEM((L, gw), bf16), acc, acc, acc],
        compiler_params=pltpu.CompilerParams(dimension_semantics=("arbitrary",),
                                             vmem_limit_bytes=VMEM_LIMIT),
        name="dilated_attention",
    )(xt, w_in_b, w_in_b, w_in_b, b_in, b_in, b_in, w_in_b, b_in, gq, gk, hsum, *dists)


def _final_kernel(x_ref, xn_ref, gh_ref, ga_ref, wg_ref, bg_ref, why_ref, wat_ref, wout_ref, out_ref):
    gates = _dot(xn_ref[...], wg_ref[...]) + bg_ref[...]
    u_h = _dot(gh_ref[...], why_ref[...])
    u_a = _dot(ga_ref[...], wat_ref[...])
    merged = jax.nn.sigmoid(gates[:, 0:D_MODEL]) * u_h + jax.nn.sigmoid(gates[:, D_MODEL:]) * u_a
    out_ref[...] = x_ref[...] + _dot(merged.astype(bf16), wout_ref[...])


def _final(x2, xn2, gh2, ga2, wg, bg, why, wat, wout):
    rows = x2.shape[0]
    tm = 512
    rspec = lambda c: pl.BlockSpec((tm, c), lambda i: (i, 0))
    cspec = lambda a: _const_spec(a.shape, lambda i: (0, 0))
    return pl.pallas_call(
        _final_kernel,
        grid=(rows // tm,),
        in_specs=[rspec(D_MODEL), rspec(D_MODEL), rspec(HY_WIDTH), rspec(AT_WIDTH),
                  cspec(wg), cspec(bg), cspec(why), cspec(wat), cspec(wout)],
        out_specs=rspec(D_MODEL),
        out_shape=jax.ShapeDtypeStruct((rows, D_MODEL), f32),
        compiler_params=pltpu.CompilerParams(dimension_semantics=("arbitrary",),
                                             vmem_limit_bytes=VMEM_LIMIT),
        name="merge_output",
    )(x2, xn2, gh2, ga2, wg, bg, why, wat, wout)


def _layer(x, norm_g, w_in, b_in, conv_w, conv_b, hf_w1, hf_b1, hf_w2, hf_b2, hf_w3, hf_b3, hf_w4,
           hf_freq, hy_skip, q_norm_g, k_norm_g, w_hy_out, w_at_out, w_out):
    B, L, D = x.shape
    x2 = x.reshape(B * L, D)
    tab = jnp.asarray(_dft_tables(L)).astype(bf16)
    w_in_b = w_in.astype(bf16)
    b_in2 = b_in.astype(f32).reshape(1, IN_COLS)

    xn2, xt = _prenorm(x2, norm_g.astype(f32).reshape(1, D), B, L)
    xn3 = xn2.reshape(B, L, D)

    kr, ki, ks = _filters(L, tab, hf_w1, hf_b1, hf_w2, hf_b2, hf_w3, hf_b3, hf_w4, hf_freq)
    gh = _hyena(xn3, w_in_b, b_in2, conv_w.astype(f32), conv_b.astype(f32).reshape(1, -1),
                tab, kr, ki, ks, hy_skip.astype(f32))

    gq = jnp.tile(q_norm_g.astype(f32), HEADS_PER_GROUP).reshape(1, -1)
    gk = jnp.tile(k_norm_g.astype(f32), HEADS_PER_GROUP).reshape(1, -1)
    head = np.arange(AT_WIDTH) // HEAD_DIM
    hsum = jnp.asarray((head[:, None] == head[None, :]).astype(np.float32)).astype(bf16)
    ga = _attention(xt, w_in_b, b_in2, gq, gk, hsum)

    out = _final(x2, xn2, gh.reshape(B * L, HY_WIDTH), ga.reshape(B * L, AT_WIDTH),
                 w_in_b[:, O_MG:], b_in2[:, O_MG:],
                 w_hy_out.astype(bf16), w_at_out.astype(bf16), w_out.astype(bf16))
    return out.reshape(B, L, D)


def kernel(x, norm_g, w_in, b_in, conv_w, conv_b, hf_w1, hf_b1, hf_w2, hf_b2, hf_w3, hf_b3, hf_w4,
           hf_freq, hy_skip, q_norm_g, k_norm_g, w_hy_out, w_at_out, w_out):
    depth = norm_g.shape[0]
    for i in range(depth):
        x = _layer(x, norm_g[i], w_in[i], b_in[i], conv_w[i], conv_b[i], hf_w1[i], hf_b1[i], hf_w2[i],
                   hf_b2[i], hf_w3[i], hf_b3[i], hf_w4[i], hf_freq[i], hy_skip[i], q_norm_g[i],
                   k_norm_g[i], w_hy_out[i], w_at_out[i], w_out[i])
    return x
```

```python
import functools
import math

import jax
import jax.numpy as jnp
import numpy as np
from jax import lax
from jax.experimental import pallas as pl
from jax.experimental.pallas import tpu as pltpu

D_MODEL = 1024
HY_WIDTH = 768
HY_ORDER = 2
HY_SHORT_CONV = 3
HY_EMB_DIM = 33
HY_FILTER_HIDDEN = 64
HY_FAST_DECAY = 0.3
HY_SLOW_DECAY = 1.5
HY_DECAY_TARGET = 1e-2
HY_MOD_SHIFT = 0.0
HEAD_DIM = 64
HEADS_PER_GROUP = 8
DILATED_GROUPS = ((128, 1), (512, 4), (2048, 16))
N_GROUPS = 3
AT_QKV = N_GROUPS * HEADS_PER_GROUP * HEAD_DIM
AT_WIDTH = HEADS_PER_GROUP * HEAD_DIM
NORM_EPS = 1e-6
NEG_INF = -1e30

O_HY = 0
O_HGATE = 3 * HY_WIDTH
O_QKV = O_HGATE + HY_WIDTH
O_AGATE = O_QKV + 3 * AT_QKV
O_MG = O_AGATE + AT_WIDTH
IN_COLS = O_MG + 2 * D_MODEL

LANES = 128
MXU_DIM = 256
VMEM_LIMIT = 56 * 1024 * 1024

HY_CT = 256
HY_RC = 512
HY_PH = 2
HALO = 8
AT_RC = 512
EMB_PAD = 128
Q_TILE = 128
BAND_HALF = 64
GROUP_ORDER = (2, 1, 0)
PAIR = 2 * HEAD_DIM

f32 = jnp.float32
bf16 = jnp.bfloat16


def _dot(a, b):
    return jnp.dot(a, b, preferred_element_type=f32)


def _const_spec(shape, index_map):
    return pl.BlockSpec(shape, index_map, pipeline_mode=pl.Buffered(1))


@functools.lru_cache(maxsize=None)
def _dft_tables(L):
    n = 2 * L
    f = np.arange(L // 2, dtype=np.int64)[:, None]
    m = np.arange(L // 2, dtype=np.int64)[None, :]
    ang_e = ((f * (2 * m)) % n).astype(np.float64) * (2.0 * np.pi / n)
    ang_o = ((f * (2 * m + 1)) % n).astype(np.float64) * (2.0 * np.pi / n)
    ce, se, co, so = np.cos(ang_e), np.sin(ang_e), np.cos(ang_o), np.sin(ang_o)
    return np.stack([ce, se, co, so, co.T, so.T]).astype(np.float32)


def _phase_major(a, L):
    return np.concatenate([a[p::HY_PH] for p in range(HY_PH)], axis=0)


@functools.lru_cache(maxsize=None)
def _filter_embedding(L):
    t = np.linspace(0.0, 1.0, L)[:, None]
    bands = (HY_EMB_DIM - 1) // 2
    w = 2.0 * np.pi * np.arange(L)[:, None] / L
    f = np.linspace(1e-4, bands - 1, bands)[None, :]
    z = np.concatenate([t, np.cos(f * w), -np.sin(f * w)], axis=-1)
    zp = np.zeros((L, EMB_PAD), np.float64)
    zp[:, :HY_EMB_DIM] = z
    return zp.astype(np.float32)


@functools.lru_cache(maxsize=None)
def _decay_rates():
    max_decay = math.log(HY_DECAY_TARGET) / HY_FAST_DECAY
    min_decay = math.log(HY_DECAY_TARGET) / HY_SLOW_DECAY
    return np.abs(np.linspace(min_decay, max_decay, HY_WIDTH))[None, :].astype(np.float32)


def _alibi_slope(h):
    return 2.0 ** (-8.0 * (h + 1) / HEADS_PER_GROUP)


@functools.lru_cache(maxsize=None)
def _attn_dist(n, dilation, window):
    half = window // (2 * dilation)
    assert half == BAND_HALF
    tq = min(Q_TILE, n)
    w = min(2 * Q_TILE, n)
    masked = NEG_INF / _alibi_slope(HEADS_PER_GROUP - 1)
    offs = sorted({q0 - min(max(q0 - half, 0), n - w) for q0 in range(0, n, tq)})
    assert offs == [BAND_HALF * i for i in range(len(offs))]
    out = np.zeros((len(offs), w, tq), np.float32)
    for ci, off in enumerate(offs):
        rel = np.arange(tq)[None, :] + off - np.arange(w)[:, None]
        out[ci] = np.where(np.abs(rel) <= half, -dilation * np.abs(rel), masked)
    return out


@functools.lru_cache(maxsize=None)
def _slope_eye():
    eye = np.eye(Q_TILE, dtype=np.float32)
    return np.stack([np.concatenate([_alibi_slope(2 * p) * eye, _alibi_slope(2 * p + 1) * eye], axis=0)
                     for p in range(HEADS_PER_GROUP // 2)])


def _prenorm_kernel(x_ref, g_ref, o_ref, ot_ref):
    x = x_ref[...]
    ms = jnp.mean(x * x, axis=-1, keepdims=True)
    xn = x * lax.rsqrt(ms + NORM_EPS) * g_ref[...]
    o_ref[...] = xn.astype(o_ref.dtype)
    for c in range(D_MODEL // LANES):
        ot_ref[0, c] = xn[:, c * LANES:(c + 1) * LANES]


def _prenorm(x2, g, B, L):
    rows = x2.shape[0]
    tm = 1024
    per_b = L // tm
    nt = D_MODEL // LANES
    return pl.pallas_call(
        _prenorm_kernel,
        grid=(rows // tm,),
        in_specs=[pl.BlockSpec((tm, D_MODEL), lambda i: (i, 0)),
                  pl.BlockSpec((1, D_MODEL), lambda i: (0, 0))],
        out_specs=[pl.BlockSpec((tm, D_MODEL), lambda i: (i, 0)),
                   pl.BlockSpec((1, nt, tm, LANES), lambda i: (i // per_b, 0, i % per_b, 0))],
        out_shape=[jax.ShapeDtypeStruct((rows, D_MODEL), bf16),
                   jax.ShapeDtypeStruct((B, nt, L, LANES), f32)],
        compiler_params=pltpu.CompilerParams(dimension_semantics=("arbitrary",),
                                             vmem_limit_bytes=VMEM_LIMIT),
        name="prenorm",
    )(x2, g)


def _filters_kernel(z_ref, w1_ref, b1_ref, w2_ref, b2_ref, w3_ref, b3_ref, fr_ref, w4f_ref, w4b_ref,
                    t_ref, rate_ref, tab_ref, kr_ref, ki_ref, ks_ref, h3_ref):
    L = z_ref.shape[0]
    M = L // HY_PH
    hi = lax.Precision.HIGHEST

    @pl.when((pl.program_id(0) == 0) & (pl.program_id(1) == 0))
    def _():
        fr = fr_ref[...]
        h = jnp.sin(fr * (jnp.dot(z_ref[...], w1_ref[...], precision=hi, preferred_element_type=f32)
                          + b1_ref[...]))
        h = jnp.sin(fr * (jnp.dot(h, w2_ref[...], precision=hi, preferred_element_type=f32) + b2_ref[...]))
        h = jnp.sin(fr * (jnp.dot(h, w3_ref[...], precision=hi, preferred_element_type=f32) + b3_ref[...]))
        h3_ref[...] = h

    h3 = h3_ref[...]
    decay = jnp.exp(-t_ref[...] * rate_ref[...]) + HY_MOD_SHIFT
    hf = jnp.dot(h3, w4f_ref[...], precision=hi, preferred_element_type=f32) * decay
    hb = jnp.dot(h3, w4b_ref[...], precision=hi, preferred_element_type=f32) * decay
    hb0 = hb[0:1, :]
    hs = hf + hb
    hd = hb - hf
    n = 2 * L
    a0 = _dot(tab_ref[0], hs[0:M].astype(bf16))
    a1 = _dot(tab_ref[2], hs[M:L].astype(bf16))
    b0 = _dot(tab_ref[1], hd[0:M].astype(bf16))
    b1 = _dot(tab_ref[3], hd[M:L].astype(bf16))
    row = lax.broadcasted_iota(jnp.int32, a0.shape, 0)
    wgt = jnp.where(row == 0, 1.0 / n, 2.0 / n).astype(f32)
    kr_ref[0, 0] = (a0 + a1 - hb0) * wgt
    kr_ref[0, 1] = (a0 - a1 - hb0) * wgt
    ki_ref[0, 0] = (b0 + b1) * wgt
    ki_ref[0, 1] = (b1 - b0) * wgt
    sgn = jnp.where((row & 1) == 1, -1.0, 1.0).astype(f32)
    ks_ref[0, 0:1, :] = (jnp.sum(hs[0:M] * sgn, axis=0, keepdims=True) - hb0) * (2.0 / n)
    ks_ref[0, 1:2, :] = jnp.sum(hd[M:L] * sgn, axis=0, keepdims=True) * (2.0 / n)


def _filters(L, tab, w1, b1, w2, b2, w3, b3, w4, freq):
    M = L // HY_PH
    z = jnp.asarray(_phase_major(_filter_embedding(L), L))
    t = jnp.asarray(_phase_major(np.linspace(0.0, 1.0, L)[:, None].astype(np.float32), L))
    rate = jnp.asarray(_decay_rates())
    w1p = jnp.zeros((EMB_PAD, HY_FILTER_HIDDEN), f32).at[:HY_EMB_DIM].set(w1.astype(f32))
    nct = HY_WIDTH // HY_CT
    row = lambda a: a.astype(f32).reshape(1, -1)
    full = lambda shape: pl.BlockSpec(shape, lambda o, j: (0,) * len(shape))
    H = HY_FILTER_HIDDEN
    kspec = pl.BlockSpec((1, 2, M, HY_CT), lambda o, j: (o, 0, 0, j))
    kshape = jax.ShapeDtypeStruct((HY_ORDER, 2, M, HY_WIDTH), f32)
    return pl.pallas_call(
        _filters_kernel,
        grid=(HY_ORDER, nct),
        in_specs=[full((L, EMB_PAD)), full((EMB_PAD, H)), full((1, H)), full((H, H)), full((1, H)),
                  full((H, H)), full((1, H)), full((1, H)),
                  pl.BlockSpec((H, HY_CT), lambda o, j: (0, 2 * nct * o + j)),
                  pl.BlockSpec((H, HY_CT), lambda o, j: (0, 2 * nct * o + nct + j)),
                  full((L, 1)),
                  pl.BlockSpec((1, HY_CT), lambda o, j: (0, j)),
                  _const_spec(tab.shape, lambda o, j: (0, 0, 0))],
        out_specs=[kspec, kspec, pl.BlockSpec((1, 2, HY_CT), lambda o, j: (o, 0, j))],
        out_shape=[kshape, kshape, jax.ShapeDtypeStruct((HY_ORDER, 2, HY_WIDTH), f32)],
        scratch_shapes=[pltpu.VMEM((L, H), f32)],
        compiler_params=pltpu.CompilerParams(dimension_semantics=("arbitrary", "arbitrary"),
                                             vmem_limit_bytes=VMEM_LIMIT),
        name="hyena_filters",
    )(z, w1p, row(b1), w2.astype(f32), row(b2), w3.astype(f32), row(b3), row(freq),
      w4.astype(f32), w4.astype(f32), t, rate, tab)


def _hyena_kernel(xn_ref, wv_ref, wx1_ref, wx2_ref, wg_ref, bv_ref, bx1_ref, bx2_ref, bg_ref,
                  cwv_ref, cwx1_ref, cwx2_ref, cbv_ref, cbx1_ref, cbx2_ref,
                  tab_ref, kr_ref, ki_ref, ks_ref, skip_ref, o_ref,
                  z_ref, u_ref, ub_ref, pq_ref, x_ref):
    L = xn_ref.shape[1]
    M = L // HY_PH
    nlt = HY_CT // LANES
    row_chunks = [slice(r, r + HY_RC) for r in range(0, L, HY_RC)]
    chunks = [slice(r, r + HY_RC) for r in range(0, M, HY_RC)]
    row = lax.broadcasted_iota(jnp.int32, (HY_RC, HY_CT), 0)
    sgn = jnp.where((row & 1) == 1, -1.0, 1.0).astype(f32)
    for lt in range(nlt):
        z_ref[lt, 0:HALO] = jnp.zeros((HALO, LANES), f32)
        z_ref[lt, L + HALO:L + 2 * HALO] = jnp.zeros((HALO, LANES), f32)

    def stage_natural(val, c):
        for lt in range(nlt):
            z_ref[lt, HALO + c.start:HALO + c.stop, :] = val[:, lt * LANES:(lt + 1) * LANES]

    def phase_rows(p, c, shift=0):
        src = pl.ds(HALO + p + shift + HY_PH * c.start, HY_RC, stride=HY_PH)
        return jnp.concatenate([z_ref[lt, src, :] for lt in range(nlt)], axis=1)

    def proj_conv(dst_ref, w_ref, b_ref, cw_ref, cb_ref):
        for c in row_chunks:
            stage_natural(_dot(xn_ref[0, c, :], w_ref[...]) + b_ref[...], c)
        for p in range(HY_PH):
            for c in chunks:
                dst_ref[p, c] = (cb_ref[...] + phase_rows(p, c, -1) * cw_ref[0:1, :]
                                 + phase_rows(p, c) * cw_ref[1:2, :] + phase_rows(p, c, 1) * cw_ref[2:3, :])

    def long_conv(o):
        mid = [jnp.zeros((1, HY_CT), f32) for _ in range(HY_PH)]
        for p in range(HY_PH):
            for c in chunks:
                u = u_ref[p, c]
                ub_ref[p, c] = u.astype(bf16)
                mid[p] = mid[p] + jnp.sum(u * sgn, axis=0, keepdims=True)
        for c in chunks:
            a0 = _dot(tab_ref[0, c, :], ub_ref[0])
            b0 = _dot(tab_ref[1, c, :], ub_ref[0])
            a1 = _dot(tab_ref[2, c, :], ub_ref[1])
            b1 = _dot(tab_ref[3, c, :], ub_ref[1])
            al, ah, bl, bh = a0 + a1, a0 - a1, b0 + b1, b1 - b0
            krl, kil, krh, kih = kr_ref[o, 0, c, :], ki_ref[o, 0, c, :], kr_ref[o, 1, c, :], ki_ref[o, 1, c, :]
            p_lo, q_lo = al * krl + bl * kil, bl * krl - al * kil
            p_hi, q_hi = ah * krh + bh * kih, bh * krh - ah * kih
            pq_ref[0, c] = (p_lo + p_hi).astype(bf16)
            pq_ref[1, c] = (q_lo - q_hi).astype(bf16)
            pq_ref[2, c] = (p_lo - p_hi).astype(bf16)
            pq_ref[3, c] = (q_lo + q_hi).astype(bf16)
        kr_mid, ki_mid = ks_ref[o, 0:1, :], ks_ref[o, 1:2, :]
        p_mid = mid[0] * kr_mid + mid[1] * ki_mid
        q_mid = mid[1] * kr_mid - mid[0] * ki_mid
        skip = skip_ref[o:o + 1, :]
        for c in chunks:
            ye = _dot(tab_ref[0, c, :], pq_ref[0]) + _dot(tab_ref[1, c, :], pq_ref[1])
            u_ref[0, c] = x_ref[0, c] * (ye + sgn * p_mid + u_ref[0, c] * skip)
            yo = _dot(tab_ref[4, c, :], pq_ref[2]) + _dot(tab_ref[5, c, :], pq_ref[3])
            u_ref[1, c] = x_ref[1, c] * (yo + sgn * q_mid + u_ref[1, c] * skip)

    proj_conv(u_ref, wv_ref, bv_ref, cwv_ref, cbv_ref)
    proj_conv(x_ref, wx1_ref, bx1_ref, cwx1_ref, cbx1_ref)
    long_conv(0)
    proj_conv(x_ref, wx2_ref, bx2_ref, cwx2_ref, cbx2_ref)
    long_conv(1)
    for p in range(HY_PH):
        for c in chunks:
            dst = pl.ds(HALO + p + HY_PH * c.start, HY_RC, stride=HY_PH)
            y = u_ref[p, c]
            for lt in range(nlt):
                z_ref[lt, dst, :] = y[:, lt * LANES:(lt + 1) * LANES]
    for c in row_chunks:
        y = jnp.concatenate([z_ref[lt, HALO + c.start:HALO + c.stop, :] for lt in range(nlt)], axis=1)
        g = _dot(xn_ref[0, c, :], wg_ref[...]) + bg_ref[...]
        o_ref[0, c, :] = (y * (g * jax.nn.sigmoid(g))).astype(o_ref.dtype)


def _hyena(xn3, w_in_b, b_in, conv_w, conv_b, tab, kr, ki, ks, skip):
    B, L, _ = xn3.shape
    M = L // HY_PH
    nct = HY_WIDTH // HY_CT
    hg = O_HGATE // HY_CT

    def col(k):
        return lambda j, b: (0, k * nct + j)

    wspec = lambda k: _const_spec((D_MODEL, HY_CT), col(k))
    bspec = lambda k: pl.BlockSpec((1, HY_CT), col(k))
    cwspec = lambda k: pl.BlockSpec((HY_SHORT_CONV, HY_CT), col(k))
    kspec = _const_spec((HY_ORDER, 2, M, HY_CT), lambda j, b: (0, 0, 0, j))
    return pl.pallas_call(
        _hyena_kernel,
        grid=(nct, B),
        in_specs=[pl.BlockSpec((1, L, D_MODEL), lambda j, b: (b, 0, 0)),
                  wspec(0), wspec(1), wspec(2), _const_spec((D_MODEL, HY_CT), lambda j, b: (0, hg + j)),
                  bspec(0), bspec(1), bspec(2), pl.BlockSpec((1, HY_CT), lambda j, b: (0, hg + j)),
                  cwspec(0), cwspec(1), cwspec(2), bspec(0), bspec(1), bspec(2),
                  _const_spec(tab.shape, lambda j, b: (0, 0, 0)),
                  kspec, kspec,
                  pl.BlockSpec((HY_ORDER, 2, HY_CT), lambda j, b: (0, 0, j)),
                  pl.BlockSpec((HY_ORDER, HY_CT), lambda j, b: (0, j))],
        out_specs=pl.BlockSpec((1, L, HY_CT), lambda j, b: (b, 0, j)),
        out_shape=jax.ShapeDtypeStruct((B, L, HY_WIDTH), bf16),
        scratch_shapes=[pltpu.VMEM((HY_CT // LANES, L + 2 * HALO, LANES), f32), pltpu.VMEM((HY_PH, M, HY_CT), f32),
                        pltpu.VMEM((HY_PH, M, HY_CT), bf16), pltpu.VMEM((2 * HY_PH, M, HY_CT), bf16),
                        pltpu.VMEM((HY_PH, M, HY_CT), f32)],
        compiler_params=pltpu.CompilerParams(dimension_semantics=("arbitrary", "arbitrary"),
                                             vmem_limit_bytes=VMEM_LIMIT),
        name="hyena_mixer",
    )(xn3, w_in_b, w_in_b, w_in_b, w_in_b, b_in, b_in, b_in, b_in,
      conv_w, conv_w, conv_w, conv_b, conv_b, conv_b, tab, kr, ki, ks, skip)


def _attn_kernel(xt_ref, wq_ref, wk_ref, wv_ref, bq_ref, bk_ref, bv_ref, wag_ref, bag_ref,
                 gq_ref, gk_ref, hsum_ref, sl_ref, d0_ref, d1_ref, d2_ref, o_ref,
                 xp_ref, qs_ref, ks_ref, vs_ref, acc_ref, mx_ref, den_ref):
    L = xt_ref.shape[2]
    gw = HEADS_PER_GROUP * HEAD_DIM
    npair = HEADS_PER_GROUP // 2
    dist_refs = (d0_ref, d1_ref, d2_ref)
    tq = Q_TILE
    first = lax.broadcasted_iota(jnp.int32, (tq, PAIR), 1) < HEAD_DIM
    nt_dims = (((1,), (1,)), ((), ()))

    def normed(x, w_ref, b_ref, g_ref, cols):
        z = _dot(x, w_ref[:, cols]) + b_ref[:, cols]
        ssq = _dot((z * z).astype(bf16), hsum_ref[...])
        return z * lax.rsqrt(ssq * (1.0 / HEAD_DIM) + NORM_EPS) * g_ref[...]

    for gi, g in enumerate(GROUP_ORDER):
        _, d = DILATED_GROUPS[g]
        n = L // d
        w = min(2 * tq, n)
        per_class = n // tq
        dist_ref = dist_refs[g]
        cols = slice(g * gw, (g + 1) * gw)

        def gather(r, carry, d=d, n=n):
            dst = pl.ds(pl.multiple_of(r * n, n), n)
            src = pl.ds(r, n, stride=d) if d > 1 else pl.ds(0, n)
            for c in range(D_MODEL // LANES):
                xp_ref[dst, c * LANES:(c + 1) * LANES] = xt_ref[0, c, src, :].astype(bf16)
            return carry

        if d > 1:
            lax.fori_loop(0, d, gather, 0)
        else:
            gather(0, 0)

        for r0 in range(0, L, AT_RC):
            rows = slice(r0, r0 + AT_RC)
            x = xp_ref[rows, :]
            qs_ref[rows] = (normed(x, wq_ref, bq_ref, gq_ref, cols) * (HEAD_DIM ** -0.5)).astype(bf16)
            ks_ref[rows] = normed(x, wk_ref, bk_ref, gk_ref, cols).astype(bf16)
            vs_ref[rows] = (_dot(x, wv_ref[:, cols]) + bv_ref[:, cols]).astype(bf16)

        def tile(idx, carry, gi=gi, d=d, n=n, w=w, per_class=per_class, dist_ref=dist_ref):
            r = idx // per_class
            t = idx % per_class
            q0 = pl.multiple_of(idx * tq, tq)
            koff = jnp.clip(t * tq - BAND_HALF, 0, n - w)
            dist_t = dist_ref[(t * tq - koff) // BAND_HALF]
            k0 = pl.multiple_of(r * n + koff, BAND_HALF)
            nat = pl.ds(t * tq * d + r, tq, stride=d) if d > 1 else pl.ds(q0, tq)
            for p in range(npair):
                pc = slice(p * PAIR, (p + 1) * PAIR)
                q = qs_ref[pl.ds(q0, tq), pc]
                zero = jnp.zeros_like(q)
                qq = jnp.concatenate([jnp.where(first, q, zero), jnp.where(first, zero, q)], axis=0)
                lhs = jnp.concatenate([qq, sl_ref[p]], axis=1)
                rhs_t = jnp.concatenate([ks_ref[pl.ds(k0, w), pc], dist_t], axis=1)
                s = lax.dot_general(lhs, rhs_t, nt_dims, preferred_element_type=f32)
                m = jnp.max(s, axis=-1, keepdims=True)
                pr = jnp.exp(s - m).astype(bf16)
                rhs = jnp.concatenate([vs_ref[pl.ds(k0, w), pc], jnp.ones((w, PAIR), bf16)], axis=1)
                ov = _dot(pr, rhs)
                num = jnp.where(first, ov[0:tq, 0:PAIR], ov[tq:2 * tq, 0:PAIR])
                den = jnp.where(first, ov[0:tq, PAIR:2 * PAIR], ov[tq:2 * tq, PAIR:2 * PAIR])
                mb = jnp.where(first, m[0:tq], m[tq:2 * tq])
                if gi == 0:
                    acc_ref[p, nat, :] = num
                    den_ref[p, nat, :] = den
                    mx_ref[p, nat, :] = mb
                else:
                    m_old = mx_ref[p, nat, :]
                    m_new = jnp.maximum(m_old, mb)
                    a = jnp.exp(m_old - m_new)
                    b = jnp.exp(mb - m_new)
                    acc_ref[p, nat, :] = acc_ref[p, nat, :] * a + num * b
                    den_ref[p, nat, :] = den_ref[p, nat, :] * a + den * b
                    mx_ref[p, nat, :] = m_new
            return carry

        lax.fori_loop(0, L // tq, tile, 0, unroll=8)

    for r0 in range(0, L, AT_RC):
        rows = slice(r0, r0 + AT_RC)
        ag = _dot(xp_ref[rows, :], wag_ref[...]) + bag_ref[...]
        o = jnp.concatenate([acc_ref[p, rows, :] / den_ref[p, rows, :] for p in range(npair)], axis=1)
        o_ref[0, rows, :] = (o * (ag * jax.nn.sigmoid(ag))).astype(o_ref.dtype)


def _attention(xt, w_in_b, b_in, gq, gk, hsum):
    B, nt, L, _ = xt.shape
    assert DILATED_GROUPS[GROUP_ORDER[-1]][1] == 1 and L % AT_RC == 0
    assert all(L % (Q_TILE * d) == 0 for _, d in DILATED_GROUPS)
    gw = HEADS_PER_GROUP * HEAD_DIM
    npair = HEADS_PER_GROUP // 2
    dists = [jnp.asarray(_attn_dist(L // d, d, window)).astype(bf16) for window, d in DILATED_GROUPS]
    sl = jnp.asarray(_slope_eye()).astype(bf16)
    qb, agb = O_QKV // AT_QKV, O_AGATE // gw
    wspec = lambda k: _const_spec((D_MODEL, AT_QKV), lambda b: (0, qb + k))
    bspec = lambda k: pl.BlockSpec((1, AT_QKV), lambda b: (0, qb + k))
    vec = pl.BlockSpec((1, gw), lambda b: (0, 0))
    acc = pltpu.VMEM((npair, L, PAIR), f32)
    return pl.pallas_call(
        _attn_kernel,
        grid=(B,),
        in_specs=[_const_spec((1, nt, L, LANES), lambda b: (b, 0, 0, 0)),
                  wspec(0), wspec(1), wspec(2), bspec(0), bspec(1), bspec(2),
                  _const_spec((D_MODEL, gw), lambda b: (0, agb)), pl.BlockSpec((1, gw), lambda b: (0, agb)),
                  vec, vec, _const_spec((gw, gw), lambda b: (0, 0)), _const_spec(sl.shape, lambda b: (0, 0, 0))]
                 + [_const_spec(t.shape, lambda b: (0, 0, 0)) for t in dists],
        out_specs=pl.BlockSpec((1, L, gw), lambda b: (b, 0, 0)),
        out_shape=jax.ShapeDtypeStruct((B, L, gw), bf16),
        scratch_shapes=[pltpu.VMEM((L, D_MODEL), bf16), pltpu.VMEM((L, gw), bf16), pltpu.VMEM((L, gw), bf16),
                        pltpu.VMEM((L, gw), bf16), acc, acc, acc],
        compiler_params=pltpu.CompilerParams(dimension_semantics=("arbitrary",),
                                             vmem_limit_bytes=VMEM_LIMIT),
        name="dilated_attention",
    )(xt, w_in_b, w_in_b, w_in_b, b_in, b_in, b_in, w_in_b, b_in, gq, gk, hsum, sl, *dists)


def _final_kernel(x_ref, xn_ref, gh_ref, ga_ref, wg_ref, bg_ref, why_ref, wat_ref, wout_ref, out_ref):
    gates = _dot(xn_ref[...], wg_ref[...]) + bg_ref[...]
    u_h = _dot(gh_ref[...], why_ref[...])
    u_a = _dot(ga_ref[...], wat_ref[...])
    merged = jax.nn.sigmoid(gates[:, 0:D_MODEL]) * u_h + jax.nn.sigmoid(gates[:, D_MODEL:]) * u_a
    out_ref[...] = x_ref[...] + _dot(merged.astype(bf16), wout_ref[...])


def _final(x2, xn2, gh2, ga2, wg, bg, why, wat, wout):
    rows = x2.shape[0]
    tm = 512
    rspec = lambda c: pl.BlockSpec((tm, c), lambda i: (i, 0))
    cspec = lambda a: _const_spec(a.shape, lambda i: (0, 0))
    return pl.pallas_call(
        _final_kernel,
        grid=(rows // tm,),
        in_specs=[rspec(D_MODEL), rspec(D_MODEL), rspec(HY_WIDTH), rspec(AT_WIDTH),
                  cspec(wg), cspec(bg), cspec(why), cspec(wat), cspec(wout)],
        out_specs=rspec(D_MODEL),
        out_shape=jax.ShapeDtypeStruct((rows, D_MODEL), f32),
        compiler_params=pltpu.CompilerParams(dimension_semantics=("arbitrary",),
                                             vmem_limit_bytes=VMEM_LIMIT),
        name="merge_output",
    )(x2, xn2, gh2, ga2, wg, bg, why, wat, wout)


def _layer(x, norm_g, w_in, b_in, conv_w, conv_b, hf_w1, hf_b1, hf_w2, hf_b2, hf_w3, hf_b3, hf_w4,
           hf_freq, hy_skip, q_norm_g, k_norm_g, w_hy_out, w_at_out, w_out):
    B, L, D = x.shape
    x2 = x.reshape(B * L, D)
    tab = jnp.asarray(_dft_tables(L)).astype(bf16)
    w_in_b = w_in.astype(bf16)
    b_in2 = b_in.astype(f32).reshape(1, IN_COLS)

    xn2, xt = _prenorm(x2, norm_g.astype(f32).reshape(1, D), B, L)
    xn3 = xn2.reshape(B, L, D)

    kr, ki, ks = _filters(L, tab, hf_w1, hf_b1, hf_w2, hf_b2, hf_w3, hf_b3, hf_w4, hf_freq)
    gh = _hyena(xn3, w_in_b, b_in2, conv_w.astype(f32), conv_b.astype(f32).reshape(1, -1),
                tab, kr, ki, ks, hy_skip.astype(f32))

    gq = jnp.tile(q_norm_g.astype(f32), HEADS_PER_GROUP).reshape(1, -1)
    gk = jnp.tile(k_norm_g.astype(f32), HEADS_PER_GROUP).reshape(1, -1)
    head = np.arange(AT_WIDTH) // HEAD_DIM
    hsum = jnp.asarray((head[:, None] == head[None, :]).astype(np.float32)).astype(bf16)
    ga = _attention(xt, w_in_b, b_in2, gq, gk, hsum)

    out = _final(x2, xn2, gh.reshape(B * L, HY_WIDTH), ga.reshape(B * L, AT_WIDTH),
                 w_in_b[:, O_MG:], b_in2[:, O_MG:],
                 w_hy_out.astype(bf16), w_at_out.astype(bf16), w_out.astype(bf16))
    return out.reshape(B, L, D)


def kernel(x, norm_g, w_in, b_in, conv_w, conv_b, hf_w1, hf_b1, hf_w2, hf_b2, hf_w3, hf_b3, hf_w4,
           hf_freq, hy_skip, q_norm_g, k_norm_g, w_hy_out, w_at_out, w_out):
    depth = norm_g.shape[0]
    for i in range(depth):
        x = _layer(x, norm_g[i], w_in[i], b_in[i], conv_w[i], conv_b[i], hf_w1[i], hf_b1[i], hf_w2[i],
                   hf_b2[i], hf_w3[i], hf_b3[i], hf_w4[i], hf_freq[i], hy_skip[i], q_norm_g[i],
                   k_norm_g[i], w_hy_out[i], w_at_out[i], w_out[i])
    return x
```

```python
import functools
import math

import jax
import jax.numpy as jnp
import numpy as np
from jax import lax
from jax.experimental import pallas as pl
from jax.experimental.pallas import tpu as pltpu

D_MODEL = 1024
HY_WIDTH = 768
HY_ORDER = 2
HY_SHORT_CONV = 3
HY_EMB_DIM = 33
HY_FILTER_HIDDEN = 64
HY_FAST_DECAY = 0.3
HY_SLOW_DECAY = 1.5
HY_DECAY_TARGET = 1e-2
HY_MOD_SHIFT = 0.0
HEAD_DIM = 64
HEADS_PER_GROUP = 8
DILATED_GROUPS = ((128, 1), (512, 4), (2048, 16))
N_GROUPS = 3
AT_QKV = N_GROUPS * HEADS_PER_GROUP * HEAD_DIM
AT_WIDTH = HEADS_PER_GROUP * HEAD_DIM
NORM_EPS = 1e-6
NEG_INF = -1e30

O_HY = 0
O_HGATE = 3 * HY_WIDTH
O_QKV = O_HGATE + HY_WIDTH
O_AGATE = O_QKV + 3 * AT_QKV
O_MG = O_AGATE + AT_WIDTH
IN_COLS = O_MG + 2 * D_MODEL

LANES = 128
MXU_DIM = 256
VMEM_LIMIT = 56 * 1024 * 1024

HY_CT = 256
HY_RC = 512
HY_PH = 4
RSQRT2 = math.sqrt(0.5)
HALO = 8
AT_RC = 512
EMB_PAD = 128
Q_TILE = 128
BAND_HALF = 64
GROUP_ORDER = (2, 1, 0)
PAIR = 2 * HEAD_DIM

f32 = jnp.float32
bf16 = jnp.bfloat16


def _dot(a, b):
    return jnp.dot(a, b, preferred_element_type=f32)


def _const_spec(shape, index_map):
    return pl.BlockSpec(shape, index_map, pipeline_mode=pl.Buffered(1))


@functools.lru_cache(maxsize=None)
def _dft_tables(L):
    n = 2 * L
    f = np.arange(L // HY_PH, dtype=np.int64)[:, None]
    m = np.arange(L // HY_PH, dtype=np.int64)[None, :]
    fwd = []
    for p in range(HY_PH):
        ang = ((f * (HY_PH * m + p)) % n).astype(np.float64) * (2.0 * np.pi / n)
        fwd += [np.cos(ang), np.sin(ang)]
    return np.stack(fwd + [t.T for t in fwd]).astype(np.float32)


def _butterfly(A, B):
    e = (A[0] + A[2], A[0] - A[2], A[1] + A[3], A[1] - A[3])
    f = (B[0] + B[2], B[0] - B[2], B[1] + B[3], B[1] - B[3])
    return e, f


def _spectrum_cos(e, f):
    return (e[0] + e[2], e[1] - f[3], e[0] - e[2], e[1] + f[3])


def _spectrum_sin(e, f):
    return (f[0] + f[2], f[1] + e[3], f[2] - f[0], e[3] - f[1])


def _odd_bins(r):
    d, s = (r[1] - r[3]) * RSQRT2, (r[1] + r[3]) * RSQRT2
    return (r[0] + d, s + r[2]), (r[0] - d, s - r[2])


def _phase_major(a, L):
    return np.concatenate([a[p::HY_PH] for p in range(HY_PH)], axis=0)


@functools.lru_cache(maxsize=None)
def _filter_embedding(L):
    t = np.linspace(0.0, 1.0, L)[:, None]
    bands = (HY_EMB_DIM - 1) // 2
    w = 2.0 * np.pi * np.arange(L)[:, None] / L
    f = np.linspace(1e-4, bands - 1, bands)[None, :]
    z = np.concatenate([t, np.cos(f * w), -np.sin(f * w)], axis=-1)
    zp = np.zeros((L, EMB_PAD), np.float64)
    zp[:, :HY_EMB_DIM] = z
    return zp.astype(np.float32)


@functools.lru_cache(maxsize=None)
def _decay_rates():
    max_decay = math.log(HY_DECAY_TARGET) / HY_FAST_DECAY
    min_decay = math.log(HY_DECAY_TARGET) / HY_SLOW_DECAY
    return np.abs(np.linspace(min_decay, max_decay, HY_WIDTH))[None, :].astype(np.float32)


def _alibi_slope(h):
    return 2.0 ** (-8.0 * (h + 1) / HEADS_PER_GROUP)


@functools.lru_cache(maxsize=None)
def _attn_dist(n, dilation, window):
    half = window // (2 * dilation)
    assert half == BAND_HALF
    tq = min(Q_TILE, n)
    w = min(2 * Q_TILE, n)
    masked = NEG_INF / _alibi_slope(HEADS_PER_GROUP - 1)
    offs = sorted({q0 - min(max(q0 - half, 0), n - w) for q0 in range(0, n, tq)})
    assert offs == [BAND_HALF * i for i in range(len(offs))]
    out = np.zeros((len(offs), w, tq), np.float32)
    for ci, off in enumerate(offs):
        rel = np.arange(tq)[None, :] + off - np.arange(w)[:, None]
        out[ci] = np.where(np.abs(rel) <= half, -dilation * np.abs(rel), masked)
    return out


@functools.lru_cache(maxsize=None)
def _slope_eye():
    eye = np.eye(Q_TILE, dtype=np.float32)
    return np.stack([np.concatenate([_alibi_slope(2 * p) * eye, _alibi_slope(2 * p + 1) * eye], axis=0)
                     for p in range(HEADS_PER_GROUP // 2)])


def _prenorm_kernel(x_ref, g_ref, o_ref, ot_ref):
    x = x_ref[...]
    ms = jnp.mean(x * x, axis=-1, keepdims=True)
    xn = x * lax.rsqrt(ms + NORM_EPS) * g_ref[...]
    o_ref[...] = xn.astype(o_ref.dtype)
    for c in range(D_MODEL // LANES):
        ot_ref[0, c] = xn[:, c * LANES:(c + 1) * LANES]


def _prenorm(x2, g, B, L):
    rows = x2.shape[0]
    tm = 1024
    per_b = L // tm
    nt = D_MODEL // LANES
    return pl.pallas_call(
        _prenorm_kernel,
        grid=(rows // tm,),
        in_specs=[pl.BlockSpec((tm, D_MODEL), lambda i: (i, 0)),
                  pl.BlockSpec((1, D_MODEL), lambda i: (0, 0))],
        out_specs=[pl.BlockSpec((tm, D_MODEL), lambda i: (i, 0)),
                   pl.BlockSpec((1, nt, tm, LANES), lambda i: (i // per_b, 0, i % per_b, 0))],
        out_shape=[jax.ShapeDtypeStruct((rows, D_MODEL), bf16),
                   jax.ShapeDtypeStruct((B, nt, L, LANES), f32)],
        compiler_params=pltpu.CompilerParams(dimension_semantics=("arbitrary",),
                                             vmem_limit_bytes=VMEM_LIMIT),
        name="prenorm",
    )(x2, g)


def _filters_kernel(z_ref, w1_ref, b1_ref, w2_ref, b2_ref, w3_ref, b3_ref, fr_ref, w4f_ref, w4b_ref,
                    t_ref, rate_ref, tab_ref, kr_ref, ki_ref, ks_ref, h3_ref):
    L = z_ref.shape[0]
    M = L // HY_PH
    hi = lax.Precision.HIGHEST

    @pl.when((pl.program_id(0) == 0) & (pl.program_id(1) == 0))
    def _():
        fr = fr_ref[...]
        h = jnp.sin(fr * (jnp.dot(z_ref[...], w1_ref[...], precision=hi, preferred_element_type=f32)
                          + b1_ref[...]))
        h = jnp.sin(fr * (jnp.dot(h, w2_ref[...], precision=hi, preferred_element_type=f32) + b2_ref[...]))
        h = jnp.sin(fr * (jnp.dot(h, w3_ref[...], precision=hi, preferred_element_type=f32) + b3_ref[...]))
        h3_ref[...] = h

    h3 = h3_ref[...]
    decay = jnp.exp(-t_ref[...] * rate_ref[...]) + HY_MOD_SHIFT
    hf = jnp.dot(h3, w4f_ref[...], precision=hi, preferred_element_type=f32) * decay
    hb = jnp.dot(h3, w4b_ref[...], precision=hi, preferred_element_type=f32) * decay
    hb0 = hb[0:1, :]
    hs = hf + hb
    hd = hb - hf
    n = 2 * L
    row = lax.broadcasted_iota(jnp.int32, (M, HY_CT), 0)
    sgn = jnp.where((row & 1) == 1, -1.0, 1.0).astype(f32)

    def transform(x):
        xp = [x[p * M:(p + 1) * M] for p in range(HY_PH)]
        A = [_dot(tab_ref[2 * p], xp[p].astype(bf16)) for p in range(HY_PH)]
        B = [_dot(tab_ref[2 * p + 1], xp[p].astype(bf16)) for p in range(HY_PH)]
        r = [jnp.sum(xp[p] * sgn, axis=0, keepdims=True) for p in range(HY_PH)]
        return _butterfly(A, B), _odd_bins(r)

    (es, fs), odd_s = transform(hs)
    (ed, fd), odd_d = transform(hd)
    kr = _spectrum_cos(es, fs)
    ki = _spectrum_sin(ed, fd)
    two = 2.0 / n
    edge = jnp.where(row == 0, 1.0 / n, two).astype(f32)
    once = jnp.where(row == 0, 0.0, two).astype(f32)
    for cls, wgt in enumerate((edge, two, edge, once)):
        kr_ref[0, cls] = (kr[cls] - hb0) * wgt
        ki_ref[0, cls] = ki[cls] * wgt
    for j in range(2):
        ks_ref[0, 2 * j:2 * j + 1, :] = (odd_s[j][0] - hb0) * two
        ks_ref[0, 2 * j + 1:2 * j + 2, :] = odd_d[j][1] * two


def _filters(L, tab, w1, b1, w2, b2, w3, b3, w4, freq):
    M = L // HY_PH
    z = jnp.asarray(_phase_major(_filter_embedding(L), L))
    t = jnp.asarray(_phase_major(np.linspace(0.0, 1.0, L)[:, None].astype(np.float32), L))
    rate = jnp.asarray(_decay_rates())
    w1p = jnp.zeros((EMB_PAD, HY_FILTER_HIDDEN), f32).at[:HY_EMB_DIM].set(w1.astype(f32))
    nct = HY_WIDTH // HY_CT
    row = lambda a: a.astype(f32).reshape(1, -1)
    full = lambda shape: pl.BlockSpec(shape, lambda o, j: (0,) * len(shape))
    H = HY_FILTER_HIDDEN
    kspec = pl.BlockSpec((1, HY_PH, M, HY_CT), lambda o, j: (o, 0, 0, j))
    kshape = jax.ShapeDtypeStruct((HY_ORDER, HY_PH, M, HY_WIDTH), f32)
    return pl.pallas_call(
        _filters_kernel,
        grid=(HY_ORDER, nct),
        in_specs=[full((L, EMB_PAD)), full((EMB_PAD, H)), full((1, H)), full((H, H)), full((1, H)),
                  full((H, H)), full((1, H)), full((1, H)),
                  pl.BlockSpec((H, HY_CT), lambda o, j: (0, 2 * nct * o + j)),
                  pl.BlockSpec((H, HY_CT), lambda o, j: (0, 2 * nct * o + nct + j)),
                  full((L, 1)),
                  pl.BlockSpec((1, HY_CT), lambda o, j: (0, j)),
                  _const_spec(tab.shape, lambda o, j: (0, 0, 0))],
        out_specs=[kspec, kspec, pl.BlockSpec((1, 4, HY_CT), lambda o, j: (o, 0, j))],
        out_shape=[kshape, kshape, jax.ShapeDtypeStruct((HY_ORDER, 4, HY_WIDTH), f32)],
        scratch_shapes=[pltpu.VMEM((L, H), f32)],
        compiler_params=pltpu.CompilerParams(dimension_semantics=("arbitrary", "arbitrary"),
                                             vmem_limit_bytes=VMEM_LIMIT),
        name="hyena_filters",
    )(z, w1p, row(b1), w2.astype(f32), row(b2), w3.astype(f32), row(b3), row(freq),
      w4.astype(f32), w4.astype(f32), t, rate, tab)


def _hyena_kernel(xn_ref, wv_ref, wx1_ref, wx2_ref, wg_ref, bv_ref, bx1_ref, bx2_ref, bg_ref,
                  cwv_ref, cwx1_ref, cwx2_ref, cbv_ref, cbx1_ref, cbx2_ref,
                  tab_ref, kr_ref, ki_ref, ks_ref, skip_ref, o_ref,
                  z_ref, u_ref, ub_ref, pq_ref, x_ref):
    L = xn_ref.shape[1]
    M = L // HY_PH
    nlt = HY_CT // LANES
    row_chunks = [slice(r, r + HY_RC) for r in range(0, L, HY_RC)]
    chunks = [slice(r, r + HY_RC) for r in range(0, M, HY_RC)]
    row = lax.broadcasted_iota(jnp.int32, (HY_RC, HY_CT), 0)
    sgn = jnp.where((row & 1) == 1, -1.0, 1.0).astype(f32)
    for lt in range(nlt):
        z_ref[lt, 0:HALO] = jnp.zeros((HALO, LANES), f32)
        z_ref[lt, L + HALO:L + 2 * HALO] = jnp.zeros((HALO, LANES), f32)

    def stage_natural(val, c):
        for lt in range(nlt):
            z_ref[lt, HALO + c.start:HALO + c.stop, :] = val[:, lt * LANES:(lt + 1) * LANES]

    def phase_rows(p, c, shift=0):
        src = pl.ds(HALO + p + shift + HY_PH * c.start, HY_RC, stride=HY_PH)
        return jnp.concatenate([z_ref[lt, src, :] for lt in range(nlt)], axis=1)

    def proj_conv(dst_ref, w_ref, b_ref, cw_ref, cb_ref):
        for c in row_chunks:
            stage_natural(_dot(xn_ref[0, c, :], w_ref[...]) + b_ref[...], c)
        for p in range(HY_PH):
            for c in chunks:
                dst_ref[p, c] = (cb_ref[...] + phase_rows(p, c, -1) * cw_ref[0:1, :]
                                 + phase_rows(p, c) * cw_ref[1:2, :] + phase_rows(p, c, 1) * cw_ref[2:3, :])

    def long_conv(o):
        r = [jnp.zeros((1, HY_CT), f32) for _ in range(HY_PH)]
        for p in range(HY_PH):
            for c in chunks:
                u = u_ref[p, c]
                ub_ref[p, c] = u.astype(bf16)
                r[p] = r[p] + jnp.sum(u * sgn, axis=0, keepdims=True)
        for c in chunks:
            A = [_dot(tab_ref[2 * p, c, :], ub_ref[p]) for p in range(HY_PH)]
            B = [_dot(tab_ref[2 * p + 1, c, :], ub_ref[p]) for p in range(HY_PH)]
            e, f = _butterfly(A, B)
            a, b = _spectrum_cos(e, f), _spectrum_sin(e, f)
            P, Q = [], []
            for cls in range(HY_PH):
                kr, ki = kr_ref[o, cls, c, :], ki_ref[o, cls, c, :]
                P.append(a[cls] * kr + b[cls] * ki)
                Q.append(b[cls] * kr - a[cls] * ki)
            g = (P[0] + P[2], P[0] - P[2], P[1] + P[3], P[3] - P[1])
            h = (Q[0] - Q[2], Q[0] + Q[2], Q[1] - Q[3], Q[1] + Q[3])
            X = (g[0] + g[2], g[1] + h[3], g[0] - g[2], g[1] - h[3])
            Y = (h[0] + h[2], h[1] + g[3], h[0] - h[2], h[1] - g[3])
            for p in range(HY_PH):
                pq_ref[2 * p, c] = X[p].astype(bf16)
                pq_ref[2 * p + 1, c] = Y[p].astype(bf16)
        pq_odd = []
        for j, (a_o, b_o) in enumerate(_odd_bins(r)):
            kr, ki = ks_ref[o, 2 * j:2 * j + 1, :], ks_ref[o, 2 * j + 1:2 * j + 2, :]
            pq_odd.append((a_o * kr + b_o * ki, b_o * kr - a_o * ki))
        (p1, q1), (p3, q3) = pq_odd
        odd = (p1 + p3, (p1 + q1 - p3 + q3) * RSQRT2, q1 - q3, (q1 - p1 + p3 + q3) * RSQRT2)
        skip = skip_ref[o:o + 1, :]
        nt = 2 * HY_PH
        for p in range(HY_PH):
            for c in chunks:
                y = _dot(tab_ref[nt + 2 * p, c, :], pq_ref[2 * p]) + _dot(tab_ref[nt + 2 * p + 1, c, :], pq_ref[2 * p + 1])
                u_ref[p, c] = x_ref[p, c] * (y + sgn * odd[p] + u_ref[p, c] * skip)

    proj_conv(u_ref, wv_ref, bv_ref, cwv_ref, cbv_ref)
    proj_conv(x_ref, wx1_ref, bx1_ref, cwx1_ref, cbx1_ref)
    long_conv(0)
    proj_conv(x_ref, wx2_ref, bx2_ref, cwx2_ref, cbx2_ref)
    long_conv(1)
    for p in range(HY_PH):
        for c in chunks:
            dst = pl.ds(HALO + p + HY_PH * c.start, HY_RC, stride=HY_PH)
            y = u_ref[p, c]
            for lt in range(nlt):
                z_ref[lt, dst, :] = y[:, lt * LANES:(lt + 1) * LANES]
    for c in row_chunks:
        y = jnp.concatenate([z_ref[lt, HALO + c.start:HALO + c.stop, :] for lt in range(nlt)], axis=1)
        g = _dot(xn_ref[0, c, :], wg_ref[...]) + bg_ref[...]
        o_ref[0, c, :] = (y * (g * jax.nn.sigmoid(g))).astype(o_ref.dtype)


def _hyena(xn3, w_in_b, b_in, conv_w, conv_b, tab, kr, ki, ks, skip):
    B, L, _ = xn3.shape
    M = L // HY_PH
    nct = HY_WIDTH // HY_CT
    hg = O_HGATE // HY_CT

    def col(k):
        return lambda j, b: (0, k * nct + j)

    wspec = lambda k: _const_spec((D_MODEL, HY_CT), col(k))
    bspec = lambda k: pl.BlockSpec((1, HY_CT), col(k))
    cwspec = lambda k: pl.BlockSpec((HY_SHORT_CONV, HY_CT), col(k))
    kspec = _const_spec((HY_ORDER, HY_PH, M, HY_CT), lambda j, b: (0, 0, 0, j))
    return pl.pallas_call(
        _hyena_kernel,
        grid=(nct, B),
        in_specs=[pl.BlockSpec((1, L, D_MODEL), lambda j, b: (b, 0, 0)),
                  wspec(0), wspec(1), wspec(2), _const_spec((D_MODEL, HY_CT), lambda j, b: (0, hg + j)),
                  bspec(0), bspec(1), bspec(2), pl.BlockSpec((1, HY_CT), lambda j, b: (0, hg + j)),
                  cwspec(0), cwspec(1), cwspec(2), bspec(0), bspec(1), bspec(2),
                  _const_spec(tab.shape, lambda j, b: (0, 0, 0)),
                  kspec, kspec,
                  pl.BlockSpec((HY_ORDER, 4, HY_CT), lambda j, b: (0, 0, j)),
                  pl.BlockSpec((HY_ORDER, HY_CT), lambda j, b: (0, j))],
        out_specs=pl.BlockSpec((1, L, HY_CT), lambda j, b: (b, 0, j)),
        out_shape=jax.ShapeDtypeStruct((B, L, HY_WIDTH), bf16),
        scratch_shapes=[pltpu.VMEM((HY_CT // LANES, L + 2 * HALO, LANES), f32), pltpu.VMEM((HY_PH, M, HY_CT), f32),
                        pltpu.VMEM((HY_PH, M, HY_CT), bf16), pltpu.VMEM((2 * HY_PH, M, HY_CT), bf16),
                        pltpu.VMEM((HY_PH, M, HY_CT), f32)],
        compiler_params=pltpu.CompilerParams(dimension_semantics=("arbitrary", "arbitrary"),
                                             vmem_limit_bytes=VMEM_LIMIT),
        name="hyena_mixer",
    )(xn3, w_in_b, w_in_b, w_in_b, w_in_b, b_in, b_in, b_in, b_in,
      conv_w, conv_w, conv_w, conv_b, conv_b, conv_b, tab, kr, ki, ks, skip)


def _attn_kernel(xt_ref, wq_ref, wk_ref, wv_ref, bq_ref, bk_ref, bv_ref, wag_ref, bag_ref,
                 gq_ref, gk_ref, hsum_ref, sl_ref, d0_ref, d1_ref, d2_ref, o_ref,
                 xp_ref, qs_ref, ks_ref, vs_ref, acc_ref, mx_ref, den_ref):
    L = xt_ref.shape[2]
    gw = HEADS_PER_GROUP * HEAD_DIM
    npair = HEADS_PER_GROUP // 2
    dist_refs = (d0_ref, d1_ref, d2_ref)
    tq = Q_TILE
    first = lax.broadcasted_iota(jnp.int32, (tq, PAIR), 1) < HEAD_DIM
    nt_dims = (((1,), (1,)), ((), ()))

    def normed(x, w_ref, b_ref, g_ref, cols):
        z = _dot(x, w_ref[:, cols]) + b_ref[:, cols]
        ssq = _dot((z * z).astype(bf16), hsum_ref[...])
        return z * lax.rsqrt(ssq * (1.0 / HEAD_DIM) + NORM_EPS) * g_ref[...]

    for gi, g in enumerate(GROUP_ORDER):
        _, d = DILATED_GROUPS[g]
        n = L // d
        w = min(2 * tq, n)
        per_class = n // tq
        dist_ref = dist_refs[g]
        cols = slice(g * gw, (g + 1) * gw)

        def gather(r, carry, d=d, n=n):
            dst = pl.ds(pl.multiple_of(r * n, n), n)
            src = pl.ds(r, n, stride=d) if d > 1 else pl.ds(0, n)
            for c in range(D_MODEL // LANES):
                xp_ref[dst, c * LANES:(c + 1) * LANES] = xt_ref[0, c, src, :].astype(bf16)
            return carry

        if d > 1:
            lax.fori_loop(0, d, gather, 0)
        else:
            gather(0, 0)

        for r0 in range(0, L, AT_RC):
            rows = slice(r0, r0 + AT_RC)
            x = xp_ref[rows, :]
            qs_ref[rows] = (normed(x, wq_ref, bq_ref, gq_ref, cols) * (HEAD_DIM ** -0.5)).astype(bf16)
            ks_ref[rows] = normed(x, wk_ref, bk_ref, gk_ref, cols).astype(bf16)
            vs_ref[rows] = (_dot(x, wv_ref[:, cols]) + bv_ref[:, cols]).astype(bf16)

        def tile(idx, carry, gi=gi, d=d, n=n, w=w, per_class=per_class, dist_ref=dist_ref):
            r = idx // per_class
            t = idx % per_class
            q0 = pl.multiple_of(idx * tq, tq)
            koff = jnp.clip(t * tq - BAND_HALF, 0, n - w)
            dist_t = dist_ref[(t * tq - koff) // BAND_HALF]
            k0 = pl.multiple_of(r * n + koff, BAND_HALF)
            nat = pl.ds(t * tq * d + r, tq, stride=d) if d > 1 else pl.ds(q0, tq)
            for p in range(npair):
                pc = slice(p * PAIR, (p + 1) * PAIR)
                q = qs_ref[pl.ds(q0, tq), pc]
                zero = jnp.zeros_like(q)
                qq = jnp.concatenate([jnp.where(first, q, zero), jnp.where(first, zero, q)], axis=0)
                lhs = jnp.concatenate([qq, sl_ref[p]], axis=1)
                rhs_t = jnp.concatenate([ks_ref[pl.ds(k0, w), pc], dist_t], axis=1)
                s = lax.dot_general(lhs, rhs_t, nt_dims, preferred_element_type=f32)
                m = jnp.max(s, axis=-1, keepdims=True)
                pr = jnp.exp(s - m).astype(bf16)
                rhs = jnp.concatenate([vs_ref[pl.ds(k0, w), pc], jnp.ones((w, PAIR), bf16)], axis=1)
                ov = _dot(pr, rhs)
                num = jnp.where(first, ov[0:tq, 0:PAIR], ov[tq:2 * tq, 0:PAIR])
                den = jnp.where(first, ov[0:tq, PAIR:2 * PAIR], ov[tq:2 * tq, PAIR:2 * PAIR])
                mb = jnp.where(first, m[0:tq], m[tq:2 * tq])
                if gi == 0:
                    acc_ref[p, nat, :] = num
                    den_ref[p, nat, :] = den
                    mx_ref[p, nat, :] = mb
                else:
                    m_old = mx_ref[p, nat, :]
                    m_new = jnp.maximum(m_old, mb)
                    a = jnp.exp(m_old - m_new)
                    b = jnp.exp(mb - m_new)
                    acc_ref[p, nat, :] = acc_ref[p, nat, :] * a + num * b
                    den_ref[p, nat, :] = den_ref[p, nat, :] * a + den * b
                    mx_ref[p, nat, :] = m_new
            return carry

        lax.fori_loop(0, L // tq, tile, 0, unroll=8)

    for r0 in range(0, L, AT_RC):
        rows = slice(r0, r0 + AT_RC)
        ag = _dot(xp_ref[rows, :], wag_ref[...]) + bag_ref[...]
        o = jnp.concatenate([acc_ref[p, rows, :] / den_ref[p, rows, :] for p in range(npair)], axis=1)
        o_ref[0, rows, :] = (o * (ag * jax.nn.sigmoid(ag))).astype(o_ref.dtype)


def _attention(xt, w_in_b, b_in, gq, gk, hsum):
    B, nt, L, _ = xt.shape
    assert DILATED_GROUPS[GROUP_ORDER[-1]][1] == 1 and L % AT_RC == 0
    assert all(L % (Q_TILE * d) == 0 for _, d in DILATED_GROUPS)
    gw = HEADS_PER_GROUP * HEAD_DIM
    npair = HEADS_PER_GROUP // 2
    dists = [jnp.asarray(_attn_dist(L // d, d, window)).astype(bf16) for window, d in DILATED_GROUPS]
    sl = jnp.asarray(_slope_eye()).astype(bf16)
    qb, agb = O_QKV // AT_QKV, O_AGATE // gw
    wspec = lambda k: _const_spec((D_MODEL, AT_QKV), lambda b: (0, qb + k))
    bspec = lambda k: pl.BlockSpec((1, AT_QKV), lambda b: (0, qb + k))
    vec = pl.BlockSpec((1, gw), lambda b: (0, 0))
    acc = pltpu.VMEM((npair, L, PAIR), f32)
    return pl.pallas_call(
        _attn_kernel,
        grid=(B,),
        in_specs=[_const_spec((1, nt, L, LANES), lambda b: (b, 0, 0, 0)),
                  wspec(0), wspec(1), wspec(2), bspec(0), bspec(1), bspec(2),
                  _const_spec((D_MODEL, gw), lambda b: (0, agb)), pl.BlockSpec((1, gw), lambda b: (0, agb)),
                  vec, vec, _const_spec((gw, gw), lambda b: (0, 0)), _const_spec(sl.shape, lambda b: (0, 0, 0))]
                 + [_const_spec(t.shape, lambda b: (0, 0, 0)) for t in dists],
        out_specs=pl.BlockSpec((1, L, gw), lambda b: (b, 0, 0)),
        out_shape=jax.ShapeDtypeStruct((B, L, gw), bf16),
        scratch_shapes=[pltpu.VMEM((L, D_MODEL), bf16), pltpu.VMEM((L, gw), bf16), pltpu.VMEM((L, gw), bf16),
                        pltpu.VMEM((L, gw), bf16), acc, acc, acc],
        compiler_params=pltpu.CompilerParams(dimension_semantics=("arbitrary",),
                                             vmem_limit_bytes=VMEM_LIMIT),
        name="dilated_attention",
    )(xt, w_in_b, w_in_b, w_in_b, b_in, b_in, b_in, w_in_b, b_in, gq, gk, hsum, sl, *dists)


def _final_kernel(x_ref, xn_ref, gh_ref, ga_ref, wg_ref, bg_ref, why_ref, wat_ref, wout_ref, out_ref):
    gates = _dot(xn_ref[...], wg_ref[...]) + bg_ref[...]
    u_h = _dot(gh_ref[...], why_ref[...])
    u_a = _dot(ga_ref[...], wat_ref[...])
    merged = jax.nn.sigmoid(gates[:, 0:D_MODEL]) * u_h + jax.nn.sigmoid(gates[:, D_MODEL:]) * u_a
    out_ref[...] = x_ref[...] + _dot(merged.astype(bf16), wout_ref[...])


def _final(x2, xn2, gh2, ga2, wg, bg, why, wat, wout):
    rows = x2.shape[0]
    tm = 512
    rspec = lambda c: pl.BlockSpec((tm, c), lambda i: (i, 0))
    cspec = lambda a: _const_spec(a.shape, lambda i: (0, 0))
    return pl.pallas_call(
        _final_kernel,
        grid=(rows // tm,),
        in_specs=[rspec(D_MODEL), rspec(D_MODEL), rspec(HY_WIDTH), rspec(AT_WIDTH),
                  cspec(wg), cspec(bg), cspec(why), cspec(wat), cspec(wout)],
        out_specs=rspec(D_MODEL),
        out_shape=jax.ShapeDtypeStruct((rows, D_MODEL), f32),
        compiler_params=pltpu.CompilerParams(dimension_semantics=("arbitrary",),
                                             vmem_limit_bytes=VMEM_LIMIT),
        name="merge_output",
    )(x2, xn2, gh2, ga2, wg, bg, why, wat, wout)


def _layer(x, norm_g, w_in, b_in, conv_w, conv_b, hf_w1, hf_b1, hf_w2, hf_b2, hf_w3, hf_b3, hf_w4,
           hf_freq, hy_skip, q_norm_g, k_norm_g, w_hy_out, w_at_out, w_out):
    B, L, D = x.shape
    x2 = x.reshape(B * L, D)
    tab = jnp.asarray(_dft_tables(L)).astype(bf16)
    w_in_b = w_in.astype(bf16)
    b_in2 = b_in.astype(f32).reshape(1, IN_COLS)

    xn2, xt = _prenorm(x2, norm_g.astype(f32).reshape(1, D), B, L)
    xn3 = xn2.reshape(B, L, D)

    kr, ki, ks = _filters(L, tab, hf_w1, hf_b1, hf_w2, hf_b2, hf_w3, hf_b3, hf_w4, hf_freq)
    gh = _hyena(xn3, w_in_b, b_in2, conv_w.astype(f32), conv_b.astype(f32).reshape(1, -1),
                tab, kr, ki, ks, hy_skip.astype(f32))

    gq = jnp.tile(q_norm_g.astype(f32), HEADS_PER_GROUP).reshape(1, -1)
    gk = jnp.tile(k_norm_g.astype(f32), HEADS_PER_GROUP).reshape(1, -1)
    head = np.arange(AT_WIDTH) // HEAD_DIM
    hsum = jnp.asarray((head[:, None] == head[None, :]).astype(np.float32)).astype(bf16)
    ga = _attention(xt, w_in_b, b_in2, gq, gk, hsum)

    out = _final(x2, xn2, gh.reshape(B * L, HY_WIDTH), ga.reshape(B * L, AT_WIDTH),
                 w_in_b[:, O_MG:], b_in2[:, O_MG:],
                 w_hy_out.astype(bf16), w_at_out.astype(bf16), w_out.astype(bf16))
    return out.reshape(B, L, D)


def kernel(x, norm_g, w_in, b_in, conv_w, conv_b, hf_w1, hf_b1, hf_w2, hf_b2, hf_w3, hf_b3, hf_w4,
           hf_freq, hy_skip, q_norm_g, k_norm_g, w_hy_out, w_at_out, w_out):
    depth = norm_g.shape[0]
    for i in range(depth):
        x = _layer(x, norm_g[i], w_in[i], b_in[i], conv_w[i], conv_b[i], hf_w1[i], hf_b1[i], hf_w2[i],
                   hf_b2[i], hf_w3[i], hf_b3[i], hf_w4[i], hf_freq[i], hy_skip[i], q_norm_g[i],
                   k_norm_g[i], w_hy_out[i], w_at_out[i], w_out[i])
    return x
```

```python
import functools
import math

import jax
import jax.numpy as jnp
import numpy as np
from jax import lax
from jax.experimental import pallas as pl
from jax.experimental.pallas import tpu as pltpu

D_MODEL = 1024
HY_WIDTH = 768
HY_ORDER = 2
HY_SHORT_CONV = 3
HY_EMB_DIM = 33
HY_FILTER_HIDDEN = 64
HY_FAST_DECAY = 0.3
HY_SLOW_DECAY = 1.5
HY_DECAY_TARGET = 1e-2
HY_MOD_SHIFT = 0.0
HEAD_DIM = 64
HEADS_PER_GROUP = 8
DILATED_GROUPS = ((128, 1), (512, 4), (2048, 16))
N_GROUPS = 3
AT_QKV = N_GROUPS * HEADS_PER_GROUP * HEAD_DIM
AT_WIDTH = HEADS_PER_GROUP * HEAD_DIM
NORM_EPS = 1e-6
NEG_INF = -1e30

O_HY = 0
O_HGATE = 3 * HY_WIDTH
O_QKV = O_HGATE + HY_WIDTH
O_AGATE = O_QKV + 3 * AT_QKV
O_MG = O_AGATE + AT_WIDTH
IN_COLS = O_MG + 2 * D_MODEL

LANES = 128
MXU_DIM = 256
VMEM_LIMIT = 56 * 1024 * 1024

HY_CT = 256
HY_RC = 512
HY_PH = 4
RSQRT2 = math.sqrt(0.5)
HALO = 8
AT_RC = 512
EMB_PAD = 128
Q_TILE = 128
BAND_HALF = 64
GROUP_ORDER = (2, 1, 0)
PERM_DILATIONS = (16, 4, 1)
NAT = 2
PN_RC = 512
PAIR = 2 * HEAD_DIM

f32 = jnp.float32
bf16 = jnp.bfloat16


def _dot(a, b):
    return jnp.dot(a, b, preferred_element_type=f32)


def _const_spec(shape, index_map):
    return pl.BlockSpec(shape, index_map, pipeline_mode=pl.Buffered(1))


@functools.lru_cache(maxsize=None)
def _dft_tables(L):
    n = 2 * L
    f = np.arange(L // HY_PH, dtype=np.int64)[:, None]
    m = np.arange(L // HY_PH, dtype=np.int64)[None, :]
    fwd = []
    for p in range(HY_PH):
        ang = ((f * (HY_PH * m + p)) % n).astype(np.float64) * (2.0 * np.pi / n)
        fwd += [np.cos(ang), np.sin(ang)]
    return np.stack(fwd + [t.T for t in fwd]).astype(np.float32)


def _butterfly(A, B):
    e = (A[0] + A[2], A[0] - A[2], A[1] + A[3], A[1] - A[3])
    f = (B[0] + B[2], B[0] - B[2], B[1] + B[3], B[1] - B[3])
    return e, f


def _spectrum_cos(e, f):
    return (e[0] + e[2], e[1] - f[3], e[0] - e[2], e[1] + f[3])


def _spectrum_sin(e, f):
    return (f[0] + f[2], f[1] + e[3], f[2] - f[0], e[3] - f[1])


def _odd_bins(r):
    d, s = (r[1] - r[3]) * RSQRT2, (r[1] + r[3]) * RSQRT2
    return (r[0] + d, s + r[2]), (r[0] - d, s - r[2])


def _phase_major(a, L):
    return np.concatenate([a[p::HY_PH] for p in range(HY_PH)], axis=0)


@functools.lru_cache(maxsize=None)
def _filter_embedding(L):
    t = np.linspace(0.0, 1.0, L)[:, None]
    bands = (HY_EMB_DIM - 1) // 2
    w = 2.0 * np.pi * np.arange(L)[:, None] / L
    f = np.linspace(1e-4, bands - 1, bands)[None, :]
    z = np.concatenate([t, np.cos(f * w), -np.sin(f * w)], axis=-1)
    zp = np.zeros((L, EMB_PAD), np.float64)
    zp[:, :HY_EMB_DIM] = z
    return zp.astype(np.float32)


@functools.lru_cache(maxsize=None)
def _decay_rates():
    max_decay = math.log(HY_DECAY_TARGET) / HY_FAST_DECAY
    min_decay = math.log(HY_DECAY_TARGET) / HY_SLOW_DECAY
    return np.abs(np.linspace(min_decay, max_decay, HY_WIDTH))[None, :].astype(np.float32)


def _alibi_slope(h):
    return 2.0 ** (-8.0 * (h + 1) / HEADS_PER_GROUP)


@functools.lru_cache(maxsize=None)
def _attn_dist(n, dilation, window):
    half = window // (2 * dilation)
    assert half == BAND_HALF
    tq = min(Q_TILE, n)
    w = min(2 * Q_TILE, n)
    masked = NEG_INF / _alibi_slope(HEADS_PER_GROUP - 1)
    offs = sorted({q0 - min(max(q0 - half, 0), n - w) for q0 in range(0, n, tq)})
    assert offs == [BAND_HALF * i for i in range(len(offs))]
    out = np.zeros((len(offs), w, tq), np.float32)
    for ci, off in enumerate(offs):
        rel = np.arange(tq)[None, :] + off - np.arange(w)[:, None]
        out[ci] = np.where(np.abs(rel) <= half, -dilation * np.abs(rel), masked)
    return out


@functools.lru_cache(maxsize=None)
def _slope_eye():
    eye = np.eye(Q_TILE, dtype=np.float32)
    return np.stack([np.concatenate([_alibi_slope(2 * p) * eye, _alibi_slope(2 * p + 1) * eye], axis=0)
                     for p in range(HEADS_PER_GROUP // 2)])


def _prenorm_kernel(x_ref, g_ref, o_ref, st_ref):
    L = x_ref.shape[1]
    nt = D_MODEL // LANES
    for r0 in range(0, L, PN_RC):
        rows = slice(r0, r0 + PN_RC)
        x = x_ref[0, rows, :]
        ms = jnp.mean(x * x, axis=-1, keepdims=True)
        xn = x * lax.rsqrt(ms + NORM_EPS) * g_ref[...]
        o_ref[PERM_DILATIONS.index(1), 0, rows, :] = xn.astype(o_ref.dtype)
        for c in range(nt):
            st_ref[c, rows, :] = xn[:, c * LANES:(c + 1) * LANES]
    for i, d in enumerate(PERM_DILATIONS):
        if d == 1:
            continue
        n = L // d

        def gather(r, carry, i=i, d=d, n=n):
            dst = pl.ds(pl.multiple_of(r * n, n), n)
            for c in range(nt):
                o_ref[i, 0, dst, c * LANES:(c + 1) * LANES] = st_ref[c, pl.ds(r, n, stride=d), :].astype(o_ref.dtype)
            return carry

        lax.fori_loop(0, d, gather, 0)


def _prenorm(x, g):
    B, L, D = x.shape
    return pl.pallas_call(
        _prenorm_kernel,
        grid=(B,),
        in_specs=[pl.BlockSpec((1, L, D), lambda b: (b, 0, 0)),
                  pl.BlockSpec((1, D), lambda b: (0, 0))],
        out_specs=pl.BlockSpec((len(PERM_DILATIONS), 1, L, D), lambda b: (0, b, 0, 0)),
        out_shape=jax.ShapeDtypeStruct((len(PERM_DILATIONS), B, L, D), bf16),
        scratch_shapes=[pltpu.VMEM((D // LANES, L, LANES), f32)],
        compiler_params=pltpu.CompilerParams(dimension_semantics=("arbitrary",),
                                             vmem_limit_bytes=VMEM_LIMIT),
        name="prenorm",
    )(x, g)


def _filters_kernel(z_ref, w1_ref, b1_ref, w2_ref, b2_ref, w3_ref, b3_ref, fr_ref, w4f_ref, w4b_ref,
                    t_ref, rate_ref, tab_ref, kr_ref, ki_ref, ks_ref, h3_ref):
    L = z_ref.shape[0]
    M = L // HY_PH
    hi = lax.Precision.HIGHEST

    @pl.when((pl.program_id(0) == 0) & (pl.program_id(1) == 0))
    def _():
        fr = fr_ref[...]
        h = jnp.sin(fr * (jnp.dot(z_ref[...], w1_ref[...], precision=hi, preferred_element_type=f32)
                          + b1_ref[...]))
        h = jnp.sin(fr * (jnp.dot(h, w2_ref[...], precision=hi, preferred_element_type=f32) + b2_ref[...]))
        h = jnp.sin(fr * (jnp.dot(h, w3_ref[...], precision=hi, preferred_element_type=f32) + b3_ref[...]))
        h3_ref[...] = h

    h3 = h3_ref[...]
    decay = jnp.exp(-t_ref[...] * rate_ref[...]) + HY_MOD_SHIFT
    hf = jnp.dot(h3, w4f_ref[...], precision=hi, preferred_element_type=f32) * decay
    hb = jnp.dot(h3, w4b_ref[...], precision=hi, preferred_element_type=f32) * decay
    hb0 = hb[0:1, :]
    hs = hf + hb
    hd = hb - hf
    n = 2 * L
    row = lax.broadcasted_iota(jnp.int32, (M, HY_CT), 0)
    sgn = jnp.where((row & 1) == 1, -1.0, 1.0).astype(f32)

    def transform(x):
        xp = [x[p * M:(p + 1) * M] for p in range(HY_PH)]
        A = [_dot(tab_ref[2 * p], xp[p].astype(bf16)) for p in range(HY_PH)]
        B = [_dot(tab_ref[2 * p + 1], xp[p].astype(bf16)) for p in range(HY_PH)]
        r = [jnp.sum(xp[p] * sgn, axis=0, keepdims=True) for p in range(HY_PH)]
        return _butterfly(A, B), _odd_bins(r)

    (es, fs), odd_s = transform(hs)
    (ed, fd), odd_d = transform(hd)
    kr = _spectrum_cos(es, fs)
    ki = _spectrum_sin(ed, fd)
    two = 2.0 / n
    edge = jnp.where(row == 0, 1.0 / n, two).astype(f32)
    once = jnp.where(row == 0, 0.0, two).astype(f32)
    for cls, wgt in enumerate((edge, two, edge, once)):
        kr_ref[0, cls] = (kr[cls] - hb0) * wgt
        ki_ref[0, cls] = ki[cls] * wgt
    for j in range(2):
        ks_ref[0, 2 * j:2 * j + 1, :] = (odd_s[j][0] - hb0) * two
        ks_ref[0, 2 * j + 1:2 * j + 2, :] = odd_d[j][1] * two


def _filters(L, tab, w1, b1, w2, b2, w3, b3, w4, freq):
    M = L // HY_PH
    z = jnp.asarray(_phase_major(_filter_embedding(L), L))
    t = jnp.asarray(_phase_major(np.linspace(0.0, 1.0, L)[:, None].astype(np.float32), L))
    rate = jnp.asarray(_decay_rates())
    w1p = jnp.zeros((EMB_PAD, HY_FILTER_HIDDEN), f32).at[:HY_EMB_DIM].set(w1.astype(f32))
    nct = HY_WIDTH // HY_CT
    row = lambda a: a.astype(f32).reshape(1, -1)
    full = lambda shape: pl.BlockSpec(shape, lambda o, j: (0,) * len(shape))
    H = HY_FILTER_HIDDEN
    kspec = pl.BlockSpec((1, HY_PH, M, HY_CT), lambda o, j: (o, 0, 0, j))
    kshape = jax.ShapeDtypeStruct((HY_ORDER, HY_PH, M, HY_WIDTH), f32)
    return pl.pallas_call(
        _filters_kernel,
        grid=(HY_ORDER, nct),
        in_specs=[full((L, EMB_PAD)), full((EMB_PAD, H)), full((1, H)), full((H, H)), full((1, H)),
                  full((H, H)), full((1, H)), full((1, H)),
                  pl.BlockSpec((H, HY_CT), lambda o, j: (0, 2 * nct * o + j)),
                  pl.BlockSpec((H, HY_CT), lambda o, j: (0, 2 * nct * o + nct + j)),
                  full((L, 1)),
                  pl.BlockSpec((1, HY_CT), lambda o, j: (0, j)),
                  _const_spec(tab.shape, lambda o, j: (0, 0, 0))],
        out_specs=[kspec, kspec, pl.BlockSpec((1, 4, HY_CT), lambda o, j: (o, 0, j))],
        out_shape=[kshape, kshape, jax.ShapeDtypeStruct((HY_ORDER, 4, HY_WIDTH), f32)],
        scratch_shapes=[pltpu.VMEM((L, H), f32)],
        compiler_params=pltpu.CompilerParams(dimension_semantics=("arbitrary", "arbitrary"),
                                             vmem_limit_bytes=VMEM_LIMIT),
        name="hyena_filters",
    )(z, w1p, row(b1), w2.astype(f32), row(b2), w3.astype(f32), row(b3), row(freq),
      w4.astype(f32), w4.astype(f32), t, rate, tab)


def _hyena_kernel(xn_ref, wv_ref, wx1_ref, wx2_ref, wg_ref, bv_ref, bx1_ref, bx2_ref, bg_ref,
                  cwv_ref, cwx1_ref, cwx2_ref, cbv_ref, cbx1_ref, cbx2_ref,
                  tab_ref, kr_ref, ki_ref, ks_ref, skip_ref, o_ref,
                  z_ref, u_ref, ub_ref, pq_ref, x_ref):
    L = xn_ref.shape[1]
    M = L // HY_PH
    nlt = HY_CT // LANES
    row_chunks = [slice(r, r + HY_RC) for r in range(0, L, HY_RC)]
    chunks = [slice(r, r + HY_RC) for r in range(0, M, HY_RC)]
    row = lax.broadcasted_iota(jnp.int32, (HY_RC, HY_CT), 0)
    sgn = jnp.where((row & 1) == 1, -1.0, 1.0).astype(f32)
    for lt in range(nlt):
        z_ref[lt, 0:HALO] = jnp.zeros((HALO, LANES), f32)
        z_ref[lt, L + HALO:L + 2 * HALO] = jnp.zeros((HALO, LANES), f32)

    def stage_natural(val, c):
        for lt in range(nlt):
            z_ref[lt, HALO + c.start:HALO + c.stop, :] = val[:, lt * LANES:(lt + 1) * LANES]

    def phase_rows(p, c, shift=0):
        src = pl.ds(HALO + p + shift + HY_PH * c.start, HY_RC, stride=HY_PH)
        return jnp.concatenate([z_ref[lt, src, :] for lt in range(nlt)], axis=1)

    def proj_conv(dst_ref, w_ref, b_ref, cw_ref, cb_ref):
        for c in row_chunks:
            stage_natural(_dot(xn_ref[0, c, :], w_ref[...]) + b_ref[...], c)
        for p in range(HY_PH):
            for c in chunks:
                dst_ref[p, c] = (cb_ref[...] + phase_rows(p, c, -1) * cw_ref[0:1, :]
                                 + phase_rows(p, c) * cw_ref[1:2, :] + phase_rows(p, c, 1) * cw_ref[2:3, :])

    def long_conv(o):
        r = [jnp.zeros((1, HY_CT), f32) for _ in range(HY_PH)]
        for p in range(HY_PH):
            for c in chunks:
                u = u_ref[p, c]
                ub_ref[p, c] = u.astype(bf16)
                r[p] = r[p] + jnp.sum(u * sgn, axis=0, keepdims=True)
        for c in chunks:
            A = [_dot(tab_ref[2 * p, c, :], ub_ref[p]) for p in range(HY_PH)]
            B = [_dot(tab_ref[2 * p + 1, c, :], ub_ref[p]) for p in range(HY_PH)]
            e, f = _butterfly(A, B)
            a, b = _spectrum_cos(e, f), _spectrum_sin(e, f)
            P, Q = [], []
            for cls in range(HY_PH):
                kr, ki = kr_ref[o, cls, c, :], ki_ref[o, cls, c, :]
                P.append(a[cls] * kr + b[cls] * ki)
                Q.append(b[cls] * kr - a[cls] * ki)
            g = (P[0] + P[2], P[0] - P[2], P[1] + P[3], P[3] - P[1])
            h = (Q[0] - Q[2], Q[0] + Q[2], Q[1] - Q[3], Q[1] + Q[3])
            X = (g[0] + g[2], g[1] + h[3], g[0] - g[2], g[1] - h[3])
            Y = (h[0] + h[2], h[1] + g[3], h[0] - h[2], h[1] - g[3])
            for p in range(HY_PH):
                pq_ref[2 * p, c] = X[p].astype(bf16)
                pq_ref[2 * p + 1, c] = Y[p].astype(bf16)
        pq_odd = []
        for j, (a_o, b_o) in enumerate(_odd_bins(r)):
            kr, ki = ks_ref[o, 2 * j:2 * j + 1, :], ks_ref[o, 2 * j + 1:2 * j + 2, :]
            pq_odd.append((a_o * kr + b_o * ki, b_o * kr - a_o * ki))
        (p1, q1), (p3, q3) = pq_odd
        odd = (p1 + p3, (p1 + q1 - p3 + q3) * RSQRT2, q1 - q3, (q1 - p1 + p3 + q3) * RSQRT2)
        skip = skip_ref[o:o + 1, :]
        nt = 2 * HY_PH
        for p in range(HY_PH):
            for c in chunks:
                y = _dot(tab_ref[nt + 2 * p, c, :], pq_ref[2 * p]) + _dot(tab_ref[nt + 2 * p + 1, c, :], pq_ref[2 * p + 1])
                u_ref[p, c] = x_ref[p, c] * (y + sgn * odd[p] + u_ref[p, c] * skip)

    proj_conv(u_ref, wv_ref, bv_ref, cwv_ref, cbv_ref)
    proj_conv(x_ref, wx1_ref, bx1_ref, cwx1_ref, cbx1_ref)
    long_conv(0)
    proj_conv(x_ref, wx2_ref, bx2_ref, cwx2_ref, cbx2_ref)
    long_conv(1)
    for p in range(HY_PH):
        for c in chunks:
            dst = pl.ds(HALO + p + HY_PH * c.start, HY_RC, stride=HY_PH)
            y = u_ref[p, c]
            for lt in range(nlt):
                z_ref[lt, dst, :] = y[:, lt * LANES:(lt + 1) * LANES]
    for c in row_chunks:
        y = jnp.concatenate([z_ref[lt, HALO + c.start:HALO + c.stop, :] for lt in range(nlt)], axis=1)
        g = _dot(xn_ref[0, c, :], wg_ref[...]) + bg_ref[...]
        o_ref[0, c, :] = (y * (g * jax.nn.sigmoid(g))).astype(o_ref.dtype)


def _hyena(xs, w_in_b, b_in, conv_w, conv_b, tab, kr, ki, ks, skip):
    _, B, L, _ = xs.shape
    M = L // HY_PH
    nct = HY_WIDTH // HY_CT
    hg = O_HGATE // HY_CT

    def col(k):
        return lambda j, b: (0, k * nct + j)

    wspec = lambda k: _const_spec((D_MODEL, HY_CT), col(k))
    bspec = lambda k: pl.BlockSpec((1, HY_CT), col(k))
    cwspec = lambda k: pl.BlockSpec((HY_SHORT_CONV, HY_CT), col(k))
    kspec = _const_spec((HY_ORDER, HY_PH, M, HY_CT), lambda j, b: (0, 0, 0, j))
    return pl.pallas_call(
        _hyena_kernel,
        grid=(nct, B),
        in_specs=[pl.BlockSpec((None, 1, L, D_MODEL), lambda j, b: (NAT, b, 0, 0)),
                  wspec(0), wspec(1), wspec(2), _const_spec((D_MODEL, HY_CT), lambda j, b: (0, hg + j)),
                  bspec(0), bspec(1), bspec(2), pl.BlockSpec((1, HY_CT), lambda j, b: (0, hg + j)),
                  cwspec(0), cwspec(1), cwspec(2), bspec(0), bspec(1), bspec(2),
                  _const_spec(tab.shape, lambda j, b: (0, 0, 0)),
                  kspec, kspec,
                  pl.BlockSpec((HY_ORDER, 4, HY_CT), lambda j, b: (0, 0, j)),
                  pl.BlockSpec((HY_ORDER, HY_CT), lambda j, b: (0, j))],
        out_specs=pl.BlockSpec((1, L, HY_CT), lambda j, b: (b, 0, j)),
        out_shape=jax.ShapeDtypeStruct((B, L, HY_WIDTH), bf16),
        scratch_shapes=[pltpu.VMEM((HY_CT // LANES, L + 2 * HALO, LANES), f32), pltpu.VMEM((HY_PH, M, HY_CT), f32),
                        pltpu.VMEM((HY_PH, M, HY_CT), bf16), pltpu.VMEM((2 * HY_PH, M, HY_CT), bf16),
                        pltpu.VMEM((HY_PH, M, HY_CT), f32)],
        compiler_params=pltpu.CompilerParams(dimension_semantics=("arbitrary", "arbitrary"),
                                             vmem_limit_bytes=VMEM_LIMIT),
        name="hyena_mixer",
    )(xs, w_in_b, w_in_b, w_in_b, w_in_b, b_in, b_in, b_in, b_in,
      conv_w, conv_w, conv_w, conv_b, conv_b, conv_b, tab, kr, ki, ks, skip)


def _attn_kernel(xs_ref, wq_ref, wk_ref, wv_ref, bq_ref, bk_ref, bv_ref, wag_ref, bag_ref,
                 gq_ref, gk_ref, hsum_ref, sl_ref, d0_ref, d1_ref, d2_ref, o_ref,
                 qs_ref, ks_ref, vs_ref, acc_ref, mx_ref, den_ref):
    L = xs_ref.shape[2]
    gw = HEADS_PER_GROUP * HEAD_DIM
    npair = HEADS_PER_GROUP // 2
    dist_refs = (d0_ref, d1_ref, d2_ref)
    tq = Q_TILE
    first = lax.broadcasted_iota(jnp.int32, (tq, PAIR), 1) < HEAD_DIM
    nt_dims = (((1,), (1,)), ((), ()))

    def normed(x, w_ref, b_ref, g_ref):
        z = _dot(x, w_ref[...]) + b_ref[...]
        z2 = (z * z).astype(bf16)
        ssq = jnp.concatenate([_dot(z2[:, c:c + MXU_DIM], hsum_ref[...]) for c in range(0, gw, MXU_DIM)], axis=1)
        return z * lax.rsqrt(ssq * (1.0 / HEAD_DIM) + NORM_EPS) * g_ref[...]

    def group(gi):
        _, d = DILATED_GROUPS[GROUP_ORDER[gi]]
        n = L // d
        w = min(2 * tq, n)
        per_class = n // tq
        dist_ref = dist_refs[GROUP_ORDER[gi]]

        for r0 in range(0, L, AT_RC):
            rows = slice(r0, r0 + AT_RC)
            x = xs_ref[0, 0, rows, :]
            qs_ref[rows] = (normed(x, wq_ref, bq_ref, gq_ref) * (HEAD_DIM ** -0.5)).astype(bf16)
            ks_ref[rows] = normed(x, wk_ref, bk_ref, gk_ref).astype(bf16)
            vs_ref[rows] = (_dot(x, wv_ref[...]) + bv_ref[...]).astype(bf16)

        def tile(idx, carry):
            r = idx // per_class
            t = idx % per_class
            q0 = pl.multiple_of(idx * tq, tq)
            koff = jnp.clip(t * tq - BAND_HALF, 0, n - w)
            dist_t = dist_ref[(t * tq - koff) // BAND_HALF]
            k0 = pl.multiple_of(r * n + koff, BAND_HALF)
            nat = pl.ds(t * tq * d + r, tq, stride=d) if d > 1 else pl.ds(q0, tq)
            for p in range(npair):
                pc = slice(p * PAIR, (p + 1) * PAIR)
                q = qs_ref[pl.ds(q0, tq), pc]
                zero = jnp.zeros_like(q)
                qq = jnp.concatenate([jnp.where(first, q, zero), jnp.where(first, zero, q)], axis=0)
                lhs = jnp.concatenate([qq, sl_ref[p]], axis=1)
                rhs_t = jnp.concatenate([ks_ref[pl.ds(k0, w), pc], dist_t], axis=1)
                s = lax.dot_general(lhs, rhs_t, nt_dims, preferred_element_type=f32)
                m = jnp.max(s, axis=-1, keepdims=True)
                pr = jnp.exp(s - m).astype(bf16)
                rhs = jnp.concatenate([vs_ref[pl.ds(k0, w), pc], jnp.ones((w, PAIR), bf16)], axis=1)
                ov = _dot(pr, rhs)
                num = jnp.where(first, ov[0:tq, 0:PAIR], ov[tq:2 * tq, 0:PAIR])
                den = jnp.where(first, ov[0:tq, PAIR:2 * PAIR], ov[tq:2 * tq, PAIR:2 * PAIR])
                mb = jnp.where(first, m[0:tq], m[tq:2 * tq])
                if gi == 0:
                    acc_ref[p, nat, :] = num
                    den_ref[p, nat, :] = den
                    mx_ref[p, nat, :] = mb
                else:
                    m_old = mx_ref[p, nat, :]
                    m_new = jnp.maximum(m_old, mb)
                    a = jnp.exp(m_old - m_new)
                    b = jnp.exp(mb - m_new)
                    acc_ref[p, nat, :] = acc_ref[p, nat, :] * a + num * b
                    den_ref[p, nat, :] = den_ref[p, nat, :] * a + den * b
                    mx_ref[p, nat, :] = m_new
            return carry

        lax.fori_loop(0, L // tq, tile, 0, unroll=8)

    for gi in range(N_GROUPS):
        pl.when(pl.program_id(1) == gi)(functools.partial(group, gi))

    @pl.when(pl.program_id(1) == N_GROUPS - 1)
    def _():
        for r0 in range(0, L, AT_RC):
            rows = slice(r0, r0 + AT_RC)
            ag = _dot(xs_ref[0, 0, rows, :], wag_ref[...]) + bag_ref[...]
            o = jnp.concatenate([acc_ref[p, rows, :] / den_ref[p, rows, :] for p in range(npair)], axis=1)
            o_ref[0, rows, :] = (o * (ag * jax.nn.sigmoid(ag))).astype(o_ref.dtype)


def _attention(xs, w_in_b, b_in, gq, gk, hsum):
    _, B, L, _ = xs.shape
    assert PERM_DILATIONS == tuple(DILATED_GROUPS[g][1] for g in GROUP_ORDER) and PERM_DILATIONS[-1] == 1
    assert GROUP_ORDER == tuple(N_GROUPS - 1 - i for i in range(N_GROUPS))
    assert L % AT_RC == 0 and all(L % (Q_TILE * d) == 0 for _, d in DILATED_GROUPS)
    gw = HEADS_PER_GROUP * HEAD_DIM
    npair = HEADS_PER_GROUP // 2
    dists = [jnp.asarray(_attn_dist(L // d, d, window)).astype(bf16) for window, d in DILATED_GROUPS]
    sl = jnp.asarray(_slope_eye()).astype(bf16)
    agb = O_AGATE // gw

    def col(k):
        return lambda b, i: (0, (O_QKV + k * AT_QKV) // gw + (N_GROUPS - 1 - i))

    wspec = lambda k: pl.BlockSpec((D_MODEL, gw), col(k))
    bspec = lambda k: pl.BlockSpec((1, gw), col(k))
    vec = pl.BlockSpec((1, gw), lambda b, i: (0, 0))
    acc = pltpu.VMEM((npair, L, PAIR), f32)
    return pl.pallas_call(
        _attn_kernel,
        grid=(B, N_GROUPS),
        in_specs=[pl.BlockSpec((1, 1, L, D_MODEL), lambda b, i: (i, b, 0, 0)),
                  wspec(0), wspec(1), wspec(2), bspec(0), bspec(1), bspec(2),
                  _const_spec((D_MODEL, gw), lambda b, i: (0, agb)), pl.BlockSpec((1, gw), lambda b, i: (0, agb)),
                  vec, vec, _const_spec(hsum.shape, lambda b, i: (0, 0)), _const_spec(sl.shape, lambda b, i: (0, 0, 0))]
                 + [_const_spec(t.shape, lambda b, i: (0, 0, 0)) for t in dists],
        out_specs=pl.BlockSpec((1, L, gw), lambda b, i: (b, 0, 0)),
        out_shape=jax.ShapeDtypeStruct((B, L, gw), bf16),
        scratch_shapes=[pltpu.VMEM((L, gw), bf16), pltpu.VMEM((L, gw), bf16), pltpu.VMEM((L, gw), bf16),
                        acc, acc, acc],
        compiler_params=pltpu.CompilerParams(dimension_semantics=("arbitrary", "arbitrary"),
                                             vmem_limit_bytes=VMEM_LIMIT),
        name="dilated_attention",
    )(xs, w_in_b, w_in_b, w_in_b, b_in, b_in, b_in, w_in_b, b_in, gq, gk, hsum, sl, *dists)


def _final_kernel(x_ref, xn_ref, gh_ref, ga_ref, wg_ref, bg_ref, why_ref, wat_ref, wout_ref, out_ref):
    gates = _dot(xn_ref[...], wg_ref[...]) + bg_ref[...]
    u_h = _dot(gh_ref[...], why_ref[...])
    u_a = _dot(ga_ref[...], wat_ref[...])
    merged = jax.nn.sigmoid(gates[:, 0:D_MODEL]) * u_h + jax.nn.sigmoid(gates[:, D_MODEL:]) * u_a
    out_ref[...] = x_ref[...] + _dot(merged.astype(bf16), wout_ref[...])


def _final(x2, xs2, gh2, ga2, wg, bg, why, wat, wout):
    rows = x2.shape[0]
    tm = 512
    rspec = lambda c: pl.BlockSpec((tm, c), lambda i: (i, 0))
    cspec = lambda a: _const_spec(a.shape, lambda i: (0, 0))
    return pl.pallas_call(
        _final_kernel,
        grid=(rows // tm,),
        in_specs=[rspec(D_MODEL), pl.BlockSpec((None, tm, D_MODEL), lambda i: (NAT, i, 0)), rspec(HY_WIDTH), rspec(AT_WIDTH),
                  cspec(wg), cspec(bg), cspec(why), cspec(wat), cspec(wout)],
        out_specs=rspec(D_MODEL),
        out_shape=jax.ShapeDtypeStruct((rows, D_MODEL), f32),
        compiler_params=pltpu.CompilerParams(dimension_semantics=("arbitrary",),
                                             vmem_limit_bytes=VMEM_LIMIT),
        name="merge_output",
    )(x2, xs2, gh2, ga2, wg, bg, why, wat, wout)


def _layer(x, norm_g, w_in, b_in, conv_w, conv_b, hf_w1, hf_b1, hf_w2, hf_b2, hf_w3, hf_b3, hf_w4,
           hf_freq, hy_skip, q_norm_g, k_norm_g, w_hy_out, w_at_out, w_out):
    B, L, D = x.shape
    x2 = x.reshape(B * L, D)
    tab = jnp.asarray(_dft_tables(L)).astype(bf16)
    w_in_b = w_in.astype(bf16)
    b_in2 = b_in.astype(f32).reshape(1, IN_COLS)

    xs = _prenorm(x, norm_g.astype(f32).reshape(1, D))

    kr, ki, ks = _filters(L, tab, hf_w1, hf_b1, hf_w2, hf_b2, hf_w3, hf_b3, hf_w4, hf_freq)
    gh = _hyena(xs, w_in_b, b_in2, conv_w.astype(f32), conv_b.astype(f32).reshape(1, -1),
                tab, kr, ki, ks, hy_skip.astype(f32))

    gq = jnp.tile(q_norm_g.astype(f32), HEADS_PER_GROUP).reshape(1, -1)
    gk = jnp.tile(k_norm_g.astype(f32), HEADS_PER_GROUP).reshape(1, -1)
    head = np.arange(MXU_DIM) // HEAD_DIM
    hsum = jnp.asarray((head[:, None] == head[None, :]).astype(np.float32)).astype(bf16)
    ga = _attention(xs, w_in_b, b_in2, gq, gk, hsum)

    out = _final(x2, xs.reshape(len(PERM_DILATIONS), B * L, D), gh.reshape(B * L, HY_WIDTH), ga.reshape(B * L, AT_WIDTH),
                 w_in_b[:, O_MG:], b_in2[:, O_MG:],
                 w_hy_out.astype(bf16), w_at_out.astype(bf16), w_out.astype(bf16))
    return out.reshape(B, L, D)


def kernel(x, norm_g, w_in, b_in, conv_w, conv_b, hf_w1, hf_b1, hf_w2, hf_b2, hf_w3, hf_b3, hf_w4,
           hf_freq, hy_skip, q_norm_g, k_norm_g, w_hy_out, w_at_out, w_out):
    depth = norm_g.shape[0]
    for i in range(depth):
        x = _layer(x, norm_g[i], w_in[i], b_in[i], conv_w[i], conv_b[i], hf_w1[i], hf_b1[i], hf_w2[i],
                   hf_b2[i], hf_w3[i], hf_b3[i], hf_w4[i], hf_freq[i], hy_skip[i], q_norm_g[i],
                   k_norm_g[i], w_hy_out[i], w_at_out[i], w_out[i])
    return x
```

```python
import functools
import math

import jax
import jax.numpy as jnp
import numpy as np
from jax import lax
from jax.experimental import pallas as pl
from jax.experimental.pallas import tpu as pltpu

D_MODEL = 1024
HY_WIDTH = 768
HY_ORDER = 2
HY_SHORT_CONV = 3
HY_EMB_DIM = 33
HY_FILTER_HIDDEN = 64
HY_FAST_DECAY = 0.3
HY_SLOW_DECAY = 1.5
HY_DECAY_TARGET = 1e-2
HY_MOD_SHIFT = 0.0
HEAD_DIM = 64
HEADS_PER_GROUP = 8
DILATED_GROUPS = ((128, 1), (512, 4), (2048, 16))
N_GROUPS = 3
AT_QKV = N_GROUPS * HEADS_PER_GROUP * HEAD_DIM
AT_WIDTH = HEADS_PER_GROUP * HEAD_DIM
NORM_EPS = 1e-6
NEG_INF = -1e30

O_HY = 0
O_HGATE = 3 * HY_WIDTH
O_QKV = O_HGATE + HY_WIDTH
O_AGATE = O_QKV + 3 * AT_QKV
O_MG = O_AGATE + AT_WIDTH
IN_COLS = O_MG + 2 * D_MODEL

LANES = 128
MXU_DIM = 256
VMEM_LIMIT = 56 * 1024 * 1024

HY_CT = 256
HY_RC = 512
HY_PH = 4
RSQRT2 = math.sqrt(0.5)
HALO = 8
AT_RC = 512
EMB_PAD = 128
Q_TILE = 128
BAND_HALF = 64
GROUP_ORDER = (2, 1, 0)
PERM_DILATIONS = (16, 4, 1)
NAT = 2
PN_RC = 512
PAIR = 2 * HEAD_DIM

f32 = jnp.float32
bf16 = jnp.bfloat16


def _dot(a, b):
    return jnp.dot(a, b, preferred_element_type=f32)


def _const_spec(shape, index_map):
    return pl.BlockSpec(shape, index_map, pipeline_mode=pl.Buffered(1))


@functools.lru_cache(maxsize=None)
def _dft_tables(L):
    n = 2 * L
    f = np.arange(L // HY_PH, dtype=np.int64)[:, None]
    m = np.arange(L // HY_PH, dtype=np.int64)[None, :]
    fwd = []
    for p in range(HY_PH):
        ang = ((f * (HY_PH * m + p)) % n).astype(np.float64) * (2.0 * np.pi / n)
        fwd += [np.cos(ang), np.sin(ang)]
    return np.stack(fwd + [t.T for t in fwd]).astype(np.float32)


def _butterfly(A, B):
    e = (A[0] + A[2], A[0] - A[2], A[1] + A[3], A[1] - A[3])
    f = (B[0] + B[2], B[0] - B[2], B[1] + B[3], B[1] - B[3])
    return e, f


def _spectrum_cos(e, f):
    return (e[0] + e[2], e[1] - f[3], e[0] - e[2], e[1] + f[3])


def _spectrum_sin(e, f):
    return (f[0] + f[2], f[1] + e[3], f[2] - f[0], e[3] - f[1])


def _odd_bins(r):
    d, s = (r[1] - r[3]) * RSQRT2, (r[1] + r[3]) * RSQRT2
    return (r[0] + d, s + r[2]), (r[0] - d, s - r[2])


def _phase_major(a, L):
    return np.concatenate([a[p::HY_PH] for p in range(HY_PH)], axis=0)


@functools.lru_cache(maxsize=None)
def _filter_embedding(L):
    t = np.linspace(0.0, 1.0, L)[:, None]
    bands = (HY_EMB_DIM - 1) // 2
    w = 2.0 * np.pi * np.arange(L)[:, None] / L
    f = np.linspace(1e-4, bands - 1, bands)[None, :]
    z = np.concatenate([t, np.cos(f * w), -np.sin(f * w)], axis=-1)
    zp = np.zeros((L, EMB_PAD), np.float64)
    zp[:, :HY_EMB_DIM] = z
    return zp.astype(np.float32)


@functools.lru_cache(maxsize=None)
def _decay_rates():
    max_decay = math.log(HY_DECAY_TARGET) / HY_FAST_DECAY
    min_decay = math.log(HY_DECAY_TARGET) / HY_SLOW_DECAY
    return np.abs(np.linspace(min_decay, max_decay, HY_WIDTH))[None, :].astype(np.float32)


def _alibi_slope(h):
    return 2.0 ** (-8.0 * (h + 1) / HEADS_PER_GROUP)


@functools.lru_cache(maxsize=None)
def _attn_dist(n, dilation, window):
    half = window // (2 * dilation)
    assert half == BAND_HALF
    tq = min(Q_TILE, n)
    w = min(2 * Q_TILE, n)
    masked = NEG_INF / _alibi_slope(HEADS_PER_GROUP - 1)
    offs = sorted({q0 - min(max(q0 - half, 0), n - w) for q0 in range(0, n, tq)})
    assert offs == [BAND_HALF * i for i in range(len(offs))]
    out = np.zeros((len(offs), w, tq), np.float32)
    for ci, off in enumerate(offs):
        rel = np.arange(tq)[None, :] + off - np.arange(w)[:, None]
        out[ci] = np.where(np.abs(rel) <= half, -dilation * np.abs(rel), masked)
    return out


@functools.lru_cache(maxsize=None)
def _slope_eye():
    eye = np.eye(Q_TILE, dtype=np.float32)
    return np.stack([np.concatenate([_alibi_slope(2 * p) * eye, _alibi_slope(2 * p + 1) * eye], axis=0)
                     for p in range(HEADS_PER_GROUP // 2)])


def _prenorm_kernel(x_ref, g_ref, o_ref, st_ref):
    L = x_ref.shape[1]
    nt = D_MODEL // LANES
    for r0 in range(0, L, PN_RC):
        rows = slice(r0, r0 + PN_RC)
        x = x_ref[0, rows, :]
        ms = jnp.mean(x * x, axis=-1, keepdims=True)
        xn = x * lax.rsqrt(ms + NORM_EPS) * g_ref[...]
        o_ref[PERM_DILATIONS.index(1), 0, rows, :] = xn.astype(o_ref.dtype)
        for c in range(nt):
            st_ref[c, rows, :] = xn[:, c * LANES:(c + 1) * LANES]
    for i, d in enumerate(PERM_DILATIONS):
        if d == 1:
            continue
        n = L // d

        def gather(r, carry, i=i, d=d, n=n):
            dst = pl.ds(pl.multiple_of(r * n, n), n)
            for c in range(nt):
                o_ref[i, 0, dst, c * LANES:(c + 1) * LANES] = st_ref[c, pl.ds(r, n, stride=d), :].astype(o_ref.dtype)
            return carry

        lax.fori_loop(0, d, gather, 0)


def _prenorm(x, g):
    B, L, D = x.shape
    return pl.pallas_call(
        _prenorm_kernel,
        grid=(B,),
        in_specs=[pl.BlockSpec((1, L, D), lambda b: (b, 0, 0)),
                  pl.BlockSpec((1, D), lambda b: (0, 0))],
        out_specs=pl.BlockSpec((len(PERM_DILATIONS), 1, L, D), lambda b: (0, b, 0, 0)),
        out_shape=jax.ShapeDtypeStruct((len(PERM_DILATIONS), B, L, D), bf16),
        scratch_shapes=[pltpu.VMEM((D // LANES, L, LANES), f32)],
        compiler_params=pltpu.CompilerParams(dimension_semantics=("arbitrary",),
                                             vmem_limit_bytes=VMEM_LIMIT),
        name="prenorm",
    )(x, g)


def _filters_kernel(z_ref, w1_ref, b1_ref, w2_ref, b2_ref, w3_ref, b3_ref, fr_ref, w4f_ref, w4b_ref,
                    t_ref, rate_ref, tab_ref, kr_ref, ki_ref, ks_ref, h3_ref):
    L = z_ref.shape[0]
    M = L // HY_PH
    hi = lax.Precision.HIGHEST

    @pl.when((pl.program_id(0) == 0) & (pl.program_id(1) == 0))
    def _():
        fr = fr_ref[...]
        h = jnp.sin(fr * (jnp.dot(z_ref[...], w1_ref[...], precision=hi, preferred_element_type=f32)
                          + b1_ref[...]))
        h = jnp.sin(fr * (jnp.dot(h, w2_ref[...], precision=hi, preferred_element_type=f32) + b2_ref[...]))
        h = jnp.sin(fr * (jnp.dot(h, w3_ref[...], precision=hi, preferred_element_type=f32) + b3_ref[...]))
        h3_ref[...] = h

    h3 = h3_ref[...]
    decay = jnp.exp(-t_ref[...] * rate_ref[...]) + HY_MOD_SHIFT
    hf = jnp.dot(h3, w4f_ref[...], precision=hi, preferred_element_type=f32) * decay
    hb = jnp.dot(h3, w4b_ref[...], precision=hi, preferred_element_type=f32) * decay
    hb0 = hb[0:1, :]
    hs = hf + hb
    hd = hb - hf
    n = 2 * L
    row = lax.broadcasted_iota(jnp.int32, (M, HY_CT), 0)
    sgn = jnp.where((row & 1) == 1, -1.0, 1.0).astype(f32)

    def transform(x):
        xp = [x[p * M:(p + 1) * M] for p in range(HY_PH)]
        A = [_dot(tab_ref[2 * p], xp[p].astype(bf16)) for p in range(HY_PH)]
        B = [_dot(tab_ref[2 * p + 1], xp[p].astype(bf16)) for p in range(HY_PH)]
        r = [jnp.sum(xp[p] * sgn, axis=0, keepdims=True) for p in range(HY_PH)]
        return _butterfly(A, B), _odd_bins(r)

    (es, fs), odd_s = transform(hs)
    (ed, fd), odd_d = transform(hd)
    kr = _spectrum_cos(es, fs)
    ki = _spectrum_sin(ed, fd)
    two = 2.0 / n
    edge = jnp.where(row == 0, 1.0 / n, two).astype(f32)
    once = jnp.where(row == 0, 0.0, two).astype(f32)
    for cls, wgt in enumerate((edge, two, edge, once)):
        kr_ref[0, cls] = (kr[cls] - hb0) * wgt
        ki_ref[0, cls] = ki[cls] * wgt
    for j in range(2):
        ks_ref[0, 2 * j:2 * j + 1, :] = (odd_s[j][0] - hb0) * two
        ks_ref[0, 2 * j + 1:2 * j + 2, :] = odd_d[j][1] * two


def _filters(L, tab, w1, b1, w2, b2, w3, b3, w4, freq):
    M = L // HY_PH
    z = jnp.asarray(_phase_major(_filter_embedding(L), L))
    t = jnp.asarray(_phase_major(np.linspace(0.0, 1.0, L)[:, None].astype(np.float32), L))
    rate = jnp.asarray(_decay_rates())
    w1p = jnp.zeros((EMB_PAD, HY_FILTER_HIDDEN), f32).at[:HY_EMB_DIM].set(w1.astype(f32))
    nct = HY_WIDTH // HY_CT
    row = lambda a: a.astype(f32).reshape(1, -1)
    full = lambda shape: pl.BlockSpec(shape, lambda o, j: (0,) * len(shape))
    H = HY_FILTER_HIDDEN
    kspec = pl.BlockSpec((1, HY_PH, M, HY_CT), lambda o, j: (o, 0, 0, j))
    kshape = jax.ShapeDtypeStruct((HY_ORDER, HY_PH, M, HY_WIDTH), f32)
    return pl.pallas_call(
        _filters_kernel,
        grid=(HY_ORDER, nct),
        in_specs=[full((L, EMB_PAD)), full((EMB_PAD, H)), full((1, H)), full((H, H)), full((1, H)),
                  full((H, H)), full((1, H)), full((1, H)),
                  pl.BlockSpec((H, HY_CT), lambda o, j: (0, 2 * nct * o + j)),
                  pl.BlockSpec((H, HY_CT), lambda o, j: (0, 2 * nct * o + nct + j)),
                  full((L, 1)),
                  pl.BlockSpec((1, HY_CT), lambda o, j: (0, j)),
                  _const_spec(tab.shape, lambda o, j: (0, 0, 0))],
        out_specs=[kspec, kspec, pl.BlockSpec((1, 4, HY_CT), lambda o, j: (o, 0, j))],
        out_shape=[kshape, kshape, jax.ShapeDtypeStruct((HY_ORDER, 4, HY_WIDTH), f32)],
        scratch_shapes=[pltpu.VMEM((L, H), f32)],
        compiler_params=pltpu.CompilerParams(dimension_semantics=("arbitrary", "arbitrary"),
                                             vmem_limit_bytes=VMEM_LIMIT),
        name="hyena_filters",
    )(z, w1p, row(b1), w2.astype(f32), row(b2), w3.astype(f32), row(b3), row(freq),
      w4.astype(f32), w4.astype(f32), t, rate, tab)


def _hyena_kernel(xn_ref, wv_ref, wx1_ref, wx2_ref, wg_ref, bv_ref, bx1_ref, bx2_ref, bg_ref,
                  cwv_ref, cwx1_ref, cwx2_ref, cbv_ref, cbx1_ref, cbx2_ref,
                  tab_ref, kr_ref, ki_ref, ks_ref, skip_ref, o_ref,
                  z_ref, u_ref, ub_ref, pq_ref, x_ref):
    L = xn_ref.shape[1]
    M = L // HY_PH
    nlt = HY_CT // LANES
    row_chunks = [slice(r, r + HY_RC) for r in range(0, L, HY_RC)]
    chunks = [slice(r, r + HY_RC) for r in range(0, M, HY_RC)]
    row = lax.broadcasted_iota(jnp.int32, (HY_RC, HY_CT), 0)
    sgn = jnp.where((row & 1) == 1, -1.0, 1.0).astype(f32)
    for lt in range(nlt):
        z_ref[lt, 0:HALO] = jnp.zeros((HALO, LANES), f32)
        z_ref[lt, L + HALO:L + 2 * HALO] = jnp.zeros((HALO, LANES), f32)

    def stage_natural(val, c):
        for lt in range(nlt):
            z_ref[lt, HALO + c.start:HALO + c.stop, :] = val[:, lt * LANES:(lt + 1) * LANES]

    def phase_rows(p, c, shift=0):
        src = pl.ds(HALO + p + shift + HY_PH * c.start, HY_RC, stride=HY_PH)
        return jnp.concatenate([z_ref[lt, src, :] for lt in range(nlt)], axis=1)

    def proj_conv(dst_ref, w_ref, b_ref, cw_ref, cb_ref):
        for c in row_chunks:
            stage_natural(_dot(xn_ref[0, c, :], w_ref[...]) + b_ref[...], c)
        for p in range(HY_PH):
            for c in chunks:
                dst_ref[p, c] = (cb_ref[...] + phase_rows(p, c, -1) * cw_ref[0:1, :]
                                 + phase_rows(p, c) * cw_ref[1:2, :] + phase_rows(p, c, 1) * cw_ref[2:3, :])

    def long_conv(o):
        r = [jnp.zeros((1, HY_CT), f32) for _ in range(HY_PH)]
        for p in range(HY_PH):
            for c in chunks:
                u = u_ref[p, c]
                ub_ref[p, c] = u.astype(bf16)
                r[p] = r[p] + jnp.sum(u * sgn, axis=0, keepdims=True)
        for c in chunks:
            A = [_dot(tab_ref[2 * p, c, :], ub_ref[p]) for p in range(HY_PH)]
            B = [_dot(tab_ref[2 * p + 1, c, :], ub_ref[p]) for p in range(HY_PH)]
            e, f = _butterfly(A, B)
            a, b = _spectrum_cos(e, f), _spectrum_sin(e, f)
            P, Q = [], []
            for cls in range(HY_PH):
                kr, ki = kr_ref[o, cls, c, :], ki_ref[o, cls, c, :]
                P.append(a[cls] * kr + b[cls] * ki)
                Q.append(b[cls] * kr - a[cls] * ki)
            g = (P[0] + P[2], P[0] - P[2], P[1] + P[3], P[3] - P[1])
            h = (Q[0] - Q[2], Q[0] + Q[2], Q[1] - Q[3], Q[1] + Q[3])
            X = (g[0] + g[2], g[1] + h[3], g[0] - g[2], g[1] - h[3])
            Y = (h[0] + h[2], h[1] + g[3], h[0] - h[2], h[1] - g[3])
            for p in range(HY_PH):
                pq_ref[2 * p, c] = X[p].astype(bf16)
                pq_ref[2 * p + 1, c] = Y[p].astype(bf16)
        pq_odd = []
        for j, (a_o, b_o) in enumerate(_odd_bins(r)):
            kr, ki = ks_ref[o, 2 * j:2 * j + 1, :], ks_ref[o, 2 * j + 1:2 * j + 2, :]
            pq_odd.append((a_o * kr + b_o * ki, b_o * kr - a_o * ki))
        (p1, q1), (p3, q3) = pq_odd
        odd = (p1 + p3, (p1 + q1 - p3 + q3) * RSQRT2, q1 - q3, (q1 - p1 + p3 + q3) * RSQRT2)
        skip = skip_ref[o:o + 1, :]
        nt = 2 * HY_PH
        for p in range(HY_PH):
            for c in chunks:
                y = _dot(tab_ref[nt + 2 * p, c, :], pq_ref[2 * p]) + _dot(tab_ref[nt + 2 * p + 1, c, :], pq_ref[2 * p + 1])
                u_ref[p, c] = x_ref[p, c] * (y + sgn * odd[p] + u_ref[p, c] * skip)

    proj_conv(u_ref, wv_ref, bv_ref, cwv_ref, cbv_ref)
    proj_conv(x_ref, wx1_ref, bx1_ref, cwx1_ref, cbx1_ref)
    long_conv(0)
    proj_conv(x_ref, wx2_ref, bx2_ref, cwx2_ref, cbx2_ref)
    long_conv(1)
    for p in range(HY_PH):
        for c in chunks:
            dst = pl.ds(HALO + p + HY_PH * c.start, HY_RC, stride=HY_PH)
            y = u_ref[p, c]
            for lt in range(nlt):
                z_ref[lt, dst, :] = y[:, lt * LANES:(lt + 1) * LANES]
    for c in row_chunks:
        y = jnp.concatenate([z_ref[lt, HALO + c.start:HALO + c.stop, :] for lt in range(nlt)], axis=1)
        g = _dot(xn_ref[0, c, :], wg_ref[...]) + bg_ref[...]
        o_ref[0, c, :] = (y * (g * jax.nn.sigmoid(g))).astype(o_ref.dtype)


def _hyena(xs, w_in_b, b_in, conv_w, conv_b, tab, kr, ki, ks, skip):
    _, B, L, _ = xs.shape
    M = L // HY_PH
    nct = HY_WIDTH // HY_CT
    hg = O_HGATE // HY_CT

    def col(k):
        return lambda j, b: (0, k * nct + j)

    wspec = lambda k: _const_spec((D_MODEL, HY_CT), col(k))
    bspec = lambda k: pl.BlockSpec((1, HY_CT), col(k))
    cwspec = lambda k: pl.BlockSpec((HY_SHORT_CONV, HY_CT), col(k))
    kspec = _const_spec((HY_ORDER, HY_PH, M, HY_CT), lambda j, b: (0, 0, 0, j))
    return pl.pallas_call(
        _hyena_kernel,
        grid=(nct, B),
        in_specs=[pl.BlockSpec((None, 1, L, D_MODEL), lambda j, b: (NAT, b, 0, 0)),
                  wspec(0), wspec(1), wspec(2), _const_spec((D_MODEL, HY_CT), lambda j, b: (0, hg + j)),
                  bspec(0), bspec(1), bspec(2), pl.BlockSpec((1, HY_CT), lambda j, b: (0, hg + j)),
                  cwspec(0), cwspec(1), cwspec(2), bspec(0), bspec(1), bspec(2),
                  _const_spec(tab.shape, lambda j, b: (0, 0, 0)),
                  kspec, kspec,
                  pl.BlockSpec((HY_ORDER, 4, HY_CT), lambda j, b: (0, 0, j)),
                  pl.BlockSpec((HY_ORDER, HY_CT), lambda j, b: (0, j))],
        out_specs=pl.BlockSpec((1, L, HY_CT), lambda j, b: (b, 0, j)),
        out_shape=jax.ShapeDtypeStruct((B, L, HY_WIDTH), bf16),
        scratch_shapes=[pltpu.VMEM((HY_CT // LANES, L + 2 * HALO, LANES), f32), pltpu.VMEM((HY_PH, M, HY_CT), f32),
                        pltpu.VMEM((HY_PH, M, HY_CT), bf16), pltpu.VMEM((2 * HY_PH, M, HY_CT), bf16),
                        pltpu.VMEM((HY_PH, M, HY_CT), f32)],
        compiler_params=pltpu.CompilerParams(dimension_semantics=("arbitrary", "arbitrary"),
                                             vmem_limit_bytes=VMEM_LIMIT),
        name="hyena_mixer",
    )(xs, w_in_b, w_in_b, w_in_b, w_in_b, b_in, b_in, b_in, b_in,
      conv_w, conv_w, conv_w, conv_b, conv_b, conv_b, tab, kr, ki, ks, skip)


def _attn_kernel(xs_ref, wq_ref, wk_ref, wv_ref, bq_ref, bk_ref, bv_ref, wag_ref, bag_ref,
                 gq_ref, gk_ref, hsum_ref, sl_ref, d0_ref, d1_ref, d2_ref, o_ref,
                 qs_ref, ks_ref, vs_ref, acc_ref, mx_ref, den_ref):
    L = xs_ref.shape[2]
    gw = HEADS_PER_GROUP * HEAD_DIM
    npair = HEADS_PER_GROUP // 2
    dist_refs = (d0_ref, d1_ref, d2_ref)
    tq = Q_TILE
    first = lax.broadcasted_iota(jnp.int32, (tq, PAIR), 1) < HEAD_DIM
    nt_dims = (((1,), (1,)), ((), ()))

    def normed(x, w_ref, b_ref, g_ref):
        z = _dot(x, w_ref[...]) + b_ref[...]
        z2 = (z * z).astype(bf16)
        ssq = jnp.concatenate([_dot(z2[:, c:c + MXU_DIM], hsum_ref[...]) for c in range(0, gw, MXU_DIM)], axis=1)
        return z * lax.rsqrt(ssq * (1.0 / HEAD_DIM) + NORM_EPS) * g_ref[...]

    def group(gi):
        _, d = DILATED_GROUPS[GROUP_ORDER[gi]]
        n = L // d
        w = min(2 * tq, n)
        per_class = n // tq
        dist_ref = dist_refs[GROUP_ORDER[gi]]

        for r0 in range(0, L, AT_RC):
            rows = slice(r0, r0 + AT_RC)
            x = xs_ref[0, 0, rows, :]
            qs_ref[rows] = (normed(x, wq_ref, bq_ref, gq_ref) * (HEAD_DIM ** -0.5)).astype(bf16)
            ks_ref[rows] = normed(x, wk_ref, bk_ref, gk_ref).astype(bf16)
            vs_ref[rows] = (_dot(x, wv_ref[...]) + bv_ref[...]).astype(bf16)

        def tile(idx):
            r, t = divmod(idx, per_class)
            q0 = idx * tq
            koff = min(max(t * tq - BAND_HALF, 0), n - w)
            dist_t = dist_ref[(t * tq - koff) // BAND_HALF]
            k0 = r * n + koff
            nat = pl.ds(t * tq * d + r, tq, stride=d) if d > 1 else pl.ds(q0, tq)
            for p in range(npair):
                pc = slice(p * PAIR, (p + 1) * PAIR)
                q = qs_ref[pl.ds(q0, tq), pc]
                zero = jnp.zeros_like(q)
                qq = jnp.concatenate([jnp.where(first, q, zero), jnp.where(first, zero, q)], axis=0)
                lhs = jnp.concatenate([qq, sl_ref[p]], axis=1)
                rhs_t = jnp.concatenate([ks_ref[pl.ds(k0, w), pc], dist_t], axis=1)
                s = lax.dot_general(lhs, rhs_t, nt_dims, preferred_element_type=f32)
                m = jnp.max(s, axis=-1, keepdims=True)
                pr = jnp.exp(s - m).astype(bf16)
                rhs = jnp.concatenate([vs_ref[pl.ds(k0, w), pc], jnp.ones((w, PAIR), bf16)], axis=1)
                ov = _dot(pr, rhs)
                num = jnp.where(first, ov[0:tq, 0:PAIR], ov[tq:2 * tq, 0:PAIR])
                den = jnp.where(first, ov[0:tq, PAIR:2 * PAIR], ov[tq:2 * tq, PAIR:2 * PAIR])
                mb = jnp.where(first, m[0:tq], m[tq:2 * tq])
                if gi == 0:
                    acc_ref[p, nat, :] = num
                    den_ref[p, nat, :] = den
                    mx_ref[p, nat, :] = mb
                else:
                    m_old = mx_ref[p, nat, :]
                    m_new = jnp.maximum(m_old, mb)
                    a = jnp.exp(m_old - m_new)
                    b = jnp.exp(mb - m_new)
                    acc_ref[p, nat, :] = acc_ref[p, nat, :] * a + num * b
                    den_ref[p, nat, :] = den_ref[p, nat, :] * a + den * b
                    mx_ref[p, nat, :] = m_new

        for idx in range(L // tq):
            tile(idx)

    for gi in range(N_GROUPS):
        pl.when(pl.program_id(1) == gi)(functools.partial(group, gi))

    @pl.when(pl.program_id(1) == N_GROUPS - 1)
    def _():
        for r0 in range(0, L, AT_RC):
            rows = slice(r0, r0 + AT_RC)
            ag = _dot(xs_ref[0, 0, rows, :], wag_ref[...]) + bag_ref[...]
            o = jnp.concatenate([acc_ref[p, rows, :] / den_ref[p, rows, :] for p in range(npair)], axis=1)
            o_ref[0, rows, :] = (o * (ag * jax.nn.sigmoid(ag))).astype(o_ref.dtype)


def _attention(xs, w_in_b, b_in, gq, gk, hsum):
    _, B, L, _ = xs.shape
    assert PERM_DILATIONS == tuple(DILATED_GROUPS[g][1] for g in GROUP_ORDER) and PERM_DILATIONS[-1] == 1
    assert GROUP_ORDER == tuple(N_GROUPS - 1 - i for i in range(N_GROUPS))
    assert L % AT_RC == 0 and all(L % (Q_TILE * d) == 0 for _, d in DILATED_GROUPS)
    gw = HEADS_PER_GROUP * HEAD_DIM
    npair = HEADS_PER_GROUP // 2
    dists = [jnp.asarray(_attn_dist(L // d, d, window)).astype(bf16) for window, d in DILATED_GROUPS]
    sl = jnp.asarray(_slope_eye()).astype(bf16)
    agb = O_AGATE // gw

    def col(k):
        return lambda b, i: (0, (O_QKV + k * AT_QKV) // gw + (N_GROUPS - 1 - i))

    wspec = lambda k: pl.BlockSpec((D_MODEL, gw), col(k))
    bspec = lambda k: pl.BlockSpec((1, gw), col(k))
    vec = pl.BlockSpec((1, gw), lambda b, i: (0, 0))
    acc = pltpu.VMEM((npair, L, PAIR), f32)
    return pl.pallas_call(
        _attn_kernel,
        grid=(B, N_GROUPS),
        in_specs=[pl.BlockSpec((1, 1, L, D_MODEL), lambda b, i: (i, b, 0, 0)),
                  wspec(0), wspec(1), wspec(2), bspec(0), bspec(1), bspec(2),
                  _const_spec((D_MODEL, gw), lambda b, i: (0, agb)), pl.BlockSpec((1, gw), lambda b, i: (0, agb)),
                  vec, vec, _const_spec(hsum.shape, lambda b, i: (0, 0)), _const_spec(sl.shape, lambda b, i: (0, 0, 0))]
                 + [_const_spec(t.shape, lambda b, i: (0, 0, 0)) for t in dists],
        out_specs=pl.BlockSpec((1, L, gw), lambda b, i: (b, 0, 0)),
        out_shape=jax.ShapeDtypeStruct((B, L, gw), bf16),
        scratch_shapes=[pltpu.VMEM((L, gw), bf16), pltpu.VMEM((L, gw), bf16), pltpu.VMEM((L, gw), bf16),
                        acc, acc, acc],
        compiler_params=pltpu.CompilerParams(dimension_semantics=("arbitrary", "arbitrary"),
                                             vmem_limit_bytes=VMEM_LIMIT),
        name="dilated_attention",
    )(xs, w_in_b, w_in_b, w_in_b, b_in, b_in, b_in, w_in_b, b_in, gq, gk, hsum, sl, *dists)


def _final_kernel(x_ref, xn_ref, gh_ref, ga_ref, wg_ref, bg_ref, why_ref, wat_ref, wout_ref, out_ref):
    gates = _dot(xn_ref[...], wg_ref[...]) + bg_ref[...]
    u_h = _dot(gh_ref[...], why_ref[...])
    u_a = _dot(ga_ref[...], wat_ref[...])
    merged = jax.nn.sigmoid(gates[:, 0:D_MODEL]) * u_h + jax.nn.sigmoid(gates[:, D_MODEL:]) * u_a
    out_ref[...] = x_ref[...] + _dot(merged.astype(bf16), wout_ref[...])


def _final(x2, xs2, gh2, ga2, wg, bg, why, wat, wout):
    rows = x2.shape[0]
    tm = 512
    rspec = lambda c: pl.BlockSpec((tm, c), lambda i: (i, 0))
    cspec = lambda a: _const_spec(a.shape, lambda i: (0, 0))
    return pl.pallas_call(
        _final_kernel,
        grid=(rows // tm,),
        in_specs=[rspec(D_MODEL), pl.BlockSpec((None, tm, D_MODEL), lambda i: (NAT, i, 0)), rspec(HY_WIDTH), rspec(AT_WIDTH),
                  cspec(wg), cspec(bg), cspec(why), cspec(wat), cspec(wout)],
        out_specs=rspec(D_MODEL),
        out_shape=jax.ShapeDtypeStruct((rows, D_MODEL), f32),
        compiler_params=pltpu.CompilerParams(dimension_semantics=("arbitrary",),
                                             vmem_limit_bytes=VMEM_LIMIT),
        name="merge_output",
    )(x2, xs2, gh2, ga2, wg, bg, why, wat, wout)


def _layer(x, norm_g, w_in, b_in, conv_w, conv_b, hf_w1, hf_b1, hf_w2, hf_b2, hf_w3, hf_b3, hf_w4,
           hf_freq, hy_skip, q_norm_g, k_norm_g, w_hy_out, w_at_out, w_out):
    B, L, D = x.shape
    x2 = x.reshape(B * L, D)
    tab = jnp.asarray(_dft_tables(L)).astype(bf16)
    w_in_b = w_in.astype(bf16)
    b_in2 = b_in.astype(f32).reshape(1, IN_COLS)

    xs = _prenorm(x, norm_g.astype(f32).reshape(1, D))

    kr, ki, ks = _filters(L, tab, hf_w1, hf_b1, hf_w2, hf_b2, hf_w3, hf_b3, hf_w4, hf_freq)
    gh = _hyena(xs, w_in_b, b_in2, conv_w.astype(f32), conv_b.astype(f32).reshape(1, -1),
                tab, kr, ki, ks, hy_skip.astype(f32))

    gq = jnp.tile(q_norm_g.astype(f32), HEADS_PER_GROUP).reshape(1, -1)
    gk = jnp.tile(k_norm_g.astype(f32), HEADS_PER_GROUP).reshape(1, -1)
    head = np.arange(MXU_DIM) // HEAD_DIM
    hsum = jnp.asarray((head[:, None] == head[None, :]).astype(np.float32)).astype(bf16)
    ga = _attention(xs, w_in_b, b_in2, gq, gk, hsum)

    out = _final(x2, xs.reshape(len(PERM_DILATIONS), B * L, D), gh.reshape(B * L, HY_WIDTH), ga.reshape(B * L, AT_WIDTH),
                 w_in_b[:, O_MG:], b_in2[:, O_MG:],
                 w_hy_out.astype(bf16), w_at_out.astype(bf16), w_out.astype(bf16))
    return out.reshape(B, L, D)


def kernel(x, norm_g, w_in, b_in, conv_w, conv_b, hf_w1, hf_b1, hf_w2, hf_b2, hf_w3, hf_b3, hf_w4,
           hf_freq, hy_skip, q_norm_g, k_norm_g, w_hy_out, w_at_out, w_out):
    depth = norm_g.shape[0]
    for i in range(depth):
        x = _layer(x, norm_g[i], w_in[i], b_in[i], conv_w[i], conv_b[i], hf_w1[i], hf_b1[i], hf_w2[i],
                   hf_b2[i], hf_w3[i], hf_b3[i], hf_w4[i], hf_freq[i], hy_skip[i], q_norm_g[i],
                   k_norm_g[i], w_hy_out[i], w_at_out[i], w_out[i])
    return x
```

```python
import functools
import math

import jax
import jax.numpy as jnp
import numpy as np
from jax import lax
from jax.experimental import pallas as pl
from jax.experimental.pallas import tpu as pltpu

D_MODEL = 1024
HY_WIDTH = 768
HY_ORDER = 2
HY_SHORT_CONV = 3
HY_EMB_DIM = 33
HY_FILTER_HIDDEN = 64
HY_FAST_DECAY = 0.3
HY_SLOW_DECAY = 1.5
HY_DECAY_TARGET = 1e-2
HY_MOD_SHIFT = 0.0
HEAD_DIM = 64
HEADS_PER_GROUP = 8
DILATED_GROUPS = ((128, 1), (512, 4), (2048, 16))
N_GROUPS = 3
AT_QKV = N_GROUPS * HEADS_PER_GROUP * HEAD_DIM
AT_WIDTH = HEADS_PER_GROUP * HEAD_DIM
NORM_EPS = 1e-6
NEG_INF = -1e30

O_HY = 0
O_HGATE = 3 * HY_WIDTH
O_QKV = O_HGATE + HY_WIDTH
O_AGATE = O_QKV + 3 * AT_QKV
O_MG = O_AGATE + AT_WIDTH
IN_COLS = O_MG + 2 * D_MODEL

LANES = 128
MXU_DIM = 256
VMEM_LIMIT = 56 * 1024 * 1024

HY_CT = 256
HY_RC = 512
HY_PH = 4
RSQRT2 = math.sqrt(0.5)
HALO = 8
AT_RC = 512
EMB_PAD = 128
Q_TILE = 128
BAND_HALF = 64
GROUP_ORDER = (2, 1, 0)
PERM_DILATIONS = (16, 4, 1)
NAT = 2
PN_RC = 512
PAIR = 2 * HEAD_DIM

f32 = jnp.float32
bf16 = jnp.bfloat16


def _dot(a, b):
    return jnp.dot(a, b, preferred_element_type=f32)


def _const_spec(shape, index_map):
    return pl.BlockSpec(shape, index_map, pipeline_mode=pl.Buffered(1))


@functools.lru_cache(maxsize=None)
def _dft_tables(L):
    n = 2 * L
    f = np.arange(L // HY_PH, dtype=np.int64)[:, None]
    m = np.arange(L // HY_PH, dtype=np.int64)[None, :]
    fwd = []
    for p in range(HY_PH):
        ang = ((f * (HY_PH * m + p)) % n).astype(np.float64) * (2.0 * np.pi / n)
        fwd += [np.cos(ang), np.sin(ang)]
    return np.stack(fwd + [t.T for t in fwd]).astype(np.float32)


def _butterfly(A, B):
    e = (A[0] + A[2], A[0] - A[2], A[1] + A[3], A[1] - A[3])
    f = (B[0] + B[2], B[0] - B[2], B[1] + B[3], B[1] - B[3])
    return e, f


def _spectrum_cos(e, f):
    return (e[0] + e[2], e[1] - f[3], e[0] - e[2], e[1] + f[3])


def _spectrum_sin(e, f):
    return (f[0] + f[2], f[1] + e[3], f[2] - f[0], e[3] - f[1])


def _odd_bins(r):
    d, s = (r[1] - r[3]) * RSQRT2, (r[1] + r[3]) * RSQRT2
    return (r[0] + d, s + r[2]), (r[0] - d, s - r[2])


def _phase_major(a, L):
    return np.concatenate([a[p::HY_PH] for p in range(HY_PH)], axis=0)


@functools.lru_cache(maxsize=None)
def _filter_embedding(L):
    t = np.linspace(0.0, 1.0, L)[:, None]
    bands = (HY_EMB_DIM - 1) // 2
    w = 2.0 * np.pi * np.arange(L)[:, None] / L
    f = np.linspace(1e-4, bands - 1, bands)[None, :]
    z = np.concatenate([t, np.cos(f * w), -np.sin(f * w)], axis=-1)
    zp = np.zeros((L, EMB_PAD), np.float64)
    zp[:, :HY_EMB_DIM] = z
    return zp.astype(np.float32)


@functools.lru_cache(maxsize=None)
def _decay_rates():
    max_decay = math.log(HY_DECAY_TARGET) / HY_FAST_DECAY
    min_decay = math.log(HY_DECAY_TARGET) / HY_SLOW_DECAY
    return np.abs(np.linspace(min_decay, max_decay, HY_WIDTH))[None, :].astype(np.float32)


def _alibi_slope(h):
    return 2.0 ** (-8.0 * (h + 1) / HEADS_PER_GROUP)


@functools.lru_cache(maxsize=None)
def _attn_dist(n, dilation, window):
    half = window // (2 * dilation)
    assert half == BAND_HALF
    tq = min(Q_TILE, n)
    w = min(2 * Q_TILE, n)
    masked = NEG_INF / _alibi_slope(HEADS_PER_GROUP - 1)
    offs = sorted({q0 - min(max(q0 - half, 0), n - w) for q0 in range(0, n, tq)})
    assert offs == [BAND_HALF * i for i in range(len(offs))]
    out = np.zeros((len(offs), w, tq), np.float32)
    for ci, off in enumerate(offs):
        rel = np.arange(tq)[None, :] + off - np.arange(w)[:, None]
        out[ci] = np.where(np.abs(rel) <= half, -dilation * np.abs(rel), masked)
    return out


@functools.lru_cache(maxsize=None)
def _slope_eye():
    eye = np.eye(Q_TILE, dtype=np.float32)
    return np.stack([np.concatenate([_alibi_slope(2 * p) * eye, _alibi_slope(2 * p + 1) * eye], axis=0)
                     for p in range(HEADS_PER_GROUP // 2)])


def _prenorm_kernel(x_ref, g_ref, o_ref, st_ref, st4_ref):
    L = o_ref.shape[2]
    half = x_ref.shape[1]
    nt = D_MODEL // LANES
    base = pl.multiple_of(pl.program_id(1) * half, half)
    for r0 in range(0, half, PN_RC):
        x = x_ref[0, r0:r0 + PN_RC, :]
        ms = jnp.mean(x * x, axis=-1, keepdims=True)
        xn = x * lax.rsqrt(ms + NORM_EPS) * g_ref[...]
        rows = pl.ds(base + r0, PN_RC)
        o_ref[NAT, 0, rows, :] = xn.astype(o_ref.dtype)
        for c in range(nt):
            st_ref[c, rows, :] = xn[:, c * LANES:(c + 1) * LANES]

    @pl.when(pl.program_id(1) == pl.num_programs(1) - 1)
    def _():
        d4, d16 = PERM_DILATIONS.index(4), PERM_DILATIONS.index(16)
        n4, n16 = L // 4, L // 16

        def gather4(r, carry):
            dst = pl.ds(pl.multiple_of(r * n4, n4), n4)
            for c in range(nt):
                v = st_ref[c, pl.ds(r, n4, stride=4), :]
                st4_ref[c, dst, :] = v
                o_ref[d4, 0, dst, c * LANES:(c + 1) * LANES] = v.astype(o_ref.dtype)
            return carry

        lax.fori_loop(0, 4, gather4, 0)

        def gather16(r, carry):
            src = pl.ds((r % 4) * n4 + r // 4, n16, stride=4)
            dst = pl.ds(pl.multiple_of(r * n16, n16), n16)
            for c in range(nt):
                o_ref[d16, 0, dst, c * LANES:(c + 1) * LANES] = st4_ref[c, src, :].astype(o_ref.dtype)
            return carry

        lax.fori_loop(0, 16, gather16, 0)


def _prenorm(x, g):
    B, L, D = x.shape
    assert sorted(PERM_DILATIONS) == [1, 4, 16] and PERM_DILATIONS[NAT] == 1
    halves = 2
    st = pltpu.VMEM((D // LANES, L, LANES), f32)
    return pl.pallas_call(
        _prenorm_kernel,
        grid=(B, halves),
        in_specs=[pl.BlockSpec((1, L // halves, D), lambda b, h: (b, h, 0)),
                  pl.BlockSpec((1, D), lambda b, h: (0, 0))],
        out_specs=pl.BlockSpec((len(PERM_DILATIONS), 1, L, D), lambda b, h: (0, b, 0, 0)),
        out_shape=jax.ShapeDtypeStruct((len(PERM_DILATIONS), B, L, D), bf16),
        scratch_shapes=[st, st],
        compiler_params=pltpu.CompilerParams(dimension_semantics=("arbitrary", "arbitrary"),
                                             vmem_limit_bytes=VMEM_LIMIT),
        name="prenorm",
    )(x, g)


def _filters_kernel(z_ref, w1_ref, b1_ref, w2_ref, b2_ref, w3_ref, b3_ref, fr_ref, w4f_ref, w4b_ref,
                    t_ref, rate_ref, tab_ref, kr_ref, ki_ref, ks_ref, h3_ref):
    L = z_ref.shape[0]
    M = L // HY_PH
    hi = lax.Precision.HIGHEST

    @pl.when((pl.program_id(0) == 0) & (pl.program_id(1) == 0))
    def _():
        fr = fr_ref[...]
        h = jnp.sin(fr * (jnp.dot(z_ref[...], w1_ref[...], precision=hi, preferred_element_type=f32)
                          + b1_ref[...]))
        h = jnp.sin(fr * (jnp.dot(h, w2_ref[...], precision=hi, preferred_element_type=f32) + b2_ref[...]))
        h = jnp.sin(fr * (jnp.dot(h, w3_ref[...], precision=hi, preferred_element_type=f32) + b3_ref[...]))
        h3_ref[...] = h

    h3 = h3_ref[...]
    decay = jnp.exp(-t_ref[...] * rate_ref[...]) + HY_MOD_SHIFT
    hf = jnp.dot(h3, w4f_ref[...], precision=hi, preferred_element_type=f32) * decay
    hb = jnp.dot(h3, w4b_ref[...], precision=hi, preferred_element_type=f32) * decay
    hb0 = hb[0:1, :]
    hs = hf + hb
    hd = hb - hf
    n = 2 * L
    row = lax.broadcasted_iota(jnp.int32, (M, HY_CT), 0)
    sgn = jnp.where((row & 1) == 1, -1.0, 1.0).astype(f32)

    def transform(x):
        xp = [x[p * M:(p + 1) * M] for p in range(HY_PH)]
        A = [_dot(tab_ref[2 * p], xp[p].astype(bf16)) for p in range(HY_PH)]
        B = [_dot(tab_ref[2 * p + 1], xp[p].astype(bf16)) for p in range(HY_PH)]
        r = [jnp.sum(xp[p] * sgn, axis=0, keepdims=True) for p in range(HY_PH)]
        return _butterfly(A, B), _odd_bins(r)

    (es, fs), odd_s = transform(hs)
    (ed, fd), odd_d = transform(hd)
    kr = _spectrum_cos(es, fs)
    ki = _spectrum_sin(ed, fd)
    two = 2.0 / n
    edge = jnp.where(row == 0, 1.0 / n, two).astype(f32)
    once = jnp.where(row == 0, 0.0, two).astype(f32)
    for cls, wgt in enumerate((edge, two, edge, once)):
        kr_ref[0, cls] = (kr[cls] - hb0) * wgt
        ki_ref[0, cls] = ki[cls] * wgt
    for j in range(2):
        ks_ref[0, 2 * j:2 * j + 1, :] = (odd_s[j][0] - hb0) * two
        ks_ref[0, 2 * j + 1:2 * j + 2, :] = odd_d[j][1] * two


def _filters(L, tab, w1, b1, w2, b2, w3, b3, w4, freq):
    M = L // HY_PH
    z = jnp.asarray(_phase_major(_filter_embedding(L), L))
    t = jnp.asarray(_phase_major(np.linspace(0.0, 1.0, L)[:, None].astype(np.float32), L))
    rate = jnp.asarray(_decay_rates())
    w1p = jnp.zeros((EMB_PAD, HY_FILTER_HIDDEN), f32).at[:HY_EMB_DIM].set(w1.astype(f32))
    nct = HY_WIDTH // HY_CT
    row = lambda a: a.astype(f32).reshape(1, -1)
    full = lambda shape: pl.BlockSpec(shape, lambda o, j: (0,) * len(shape))
    H = HY_FILTER_HIDDEN
    kspec = pl.BlockSpec((1, HY_PH, M, HY_CT), lambda o, j: (o, 0, 0, j))
    kshape = jax.ShapeDtypeStruct((HY_ORDER, HY_PH, M, HY_WIDTH), f32)
    return pl.pallas_call(
        _filters_kernel,
        grid=(HY_ORDER, nct),
        in_specs=[full((L, EMB_PAD)), full((EMB_PAD, H)), full((1, H)), full((H, H)), full((1, H)),
                  full((H, H)), full((1, H)), full((1, H)),
                  pl.BlockSpec((H, HY_CT), lambda o, j: (0, 2 * nct * o + j)),
                  pl.BlockSpec((H, HY_CT), lambda o, j: (0, 2 * nct * o + nct + j)),
                  full((L, 1)),
                  pl.BlockSpec((1, HY_CT), lambda o, j: (0, j)),
                  _const_spec(tab.shape, lambda o, j: (0, 0, 0))],
        out_specs=[kspec, kspec, pl.BlockSpec((1, 4, HY_CT), lambda o, j: (o, 0, j))],
        out_shape=[kshape, kshape, jax.ShapeDtypeStruct((HY_ORDER, 4, HY_WIDTH), f32)],
        scratch_shapes=[pltpu.VMEM((L, H), f32)],
        compiler_params=pltpu.CompilerParams(dimension_semantics=("arbitrary", "arbitrary"),
                                             vmem_limit_bytes=VMEM_LIMIT),
        name="hyena_filters",
    )(z, w1p, row(b1), w2.astype(f32), row(b2), w3.astype(f32), row(b3), row(freq),
      w4.astype(f32), w4.astype(f32), t, rate, tab)


def _hyena_kernel(xn_ref, wv_ref, wx1_ref, wx2_ref, wg_ref, bv_ref, bx1_ref, bx2_ref, bg_ref,
                  cwv_ref, cwx1_ref, cwx2_ref, cbv_ref, cbx1_ref, cbx2_ref,
                  tab_ref, kr_ref, ki_ref, ks_ref, skip_ref, o_ref,
                  z_ref, u_ref, ub_ref, pq_ref, x_ref):
    L = xn_ref.shape[1]
    M = L // HY_PH
    nlt = HY_CT // LANES
    row_chunks = [slice(r, r + HY_RC) for r in range(0, L, HY_RC)]
    chunks = [slice(r, r + HY_RC) for r in range(0, M, HY_RC)]
    row = lax.broadcasted_iota(jnp.int32, (HY_RC, HY_CT), 0)
    sgn = jnp.where((row & 1) == 1, -1.0, 1.0).astype(f32)
    for lt in range(nlt):
        z_ref[lt, 0:HALO] = jnp.zeros((HALO, LANES), f32)
        z_ref[lt, L + HALO:L + 2 * HALO] = jnp.zeros((HALO, LANES), f32)

    def stage_natural(val, c):
        for lt in range(nlt):
            z_ref[lt, HALO + c.start:HALO + c.stop, :] = val[:, lt * LANES:(lt + 1) * LANES]

    def phase_rows(p, c, shift=0):
        src = pl.ds(HALO + p + shift + HY_PH * c.start, HY_RC, stride=HY_PH)
        return jnp.concatenate([z_ref[lt, src, :] for lt in range(nlt)], axis=1)

    def proj_conv(dst_ref, w_ref, b_ref, cw_ref, cb_ref):
        for c in row_chunks:
            stage_natural(_dot(xn_ref[0, c, :], w_ref[...]) + b_ref[...], c)
        for p in range(HY_PH):
            for c in chunks:
                dst_ref[p, c] = (cb_ref[...] + phase_rows(p, c, -1) * cw_ref[0:1, :]
                                 + phase_rows(p, c) * cw_ref[1:2, :] + phase_rows(p, c, 1) * cw_ref[2:3, :])

    def long_conv(o):
        r = [jnp.zeros((1, HY_CT), f32) for _ in range(HY_PH)]
        for p in range(HY_PH):
            for c in chunks:
                u = u_ref[p, c]
                ub_ref[p, c] = u.astype(bf16)
                r[p] = r[p] + jnp.sum(u * sgn, axis=0, keepdims=True)
        for c in chunks:
            A = [_dot(tab_ref[2 * p, c, :], ub_ref[p]) for p in range(HY_PH)]
            B = [_dot(tab_ref[2 * p + 1, c, :], ub_ref[p]) for p in range(HY_PH)]
            e, f = _butterfly(A, B)
            a, b = _spectrum_cos(e, f), _spectrum_sin(e, f)
            P, Q = [], []
            for cls in range(HY_PH):
                kr, ki = kr_ref[o, cls, c, :], ki_ref[o, cls, c, :]
                P.append(a[cls] * kr + b[cls] * ki)
                Q.append(b[cls] * kr - a[cls] * ki)
            g = (P[0] + P[2], P[0] - P[2], P[1] + P[3], P[3] - P[1])
            h = (Q[0] - Q[2], Q[0] + Q[2], Q[1] - Q[3], Q[1] + Q[3])
            X = (g[0] + g[2], g[1] + h[3], g[0] - g[2], g[1] - h[3])
            Y = (h[0] + h[2], h[1] + g[3], h[0] - h[2], h[1] - g[3])
            for p in range(HY_PH):
                pq_ref[2 * p, c] = X[p].astype(bf16)
                pq_ref[2 * p + 1, c] = Y[p].astype(bf16)
        pq_odd = []
        for j, (a_o, b_o) in enumerate(_odd_bins(r)):
            kr, ki = ks_ref[o, 2 * j:2 * j + 1, :], ks_ref[o, 2 * j + 1:2 * j + 2, :]
            pq_odd.append((a_o * kr + b_o * ki, b_o * kr - a_o * ki))
        (p1, q1), (p3, q3) = pq_odd
        odd = (p1 + p3, (p1 + q1 - p3 + q3) * RSQRT2, q1 - q3, (q1 - p1 + p3 + q3) * RSQRT2)
        skip = skip_ref[o:o + 1, :]
        nt = 2 * HY_PH
        for p in range(HY_PH):
            for c in chunks:
                y = _dot(tab_ref[nt + 2 * p, c, :], pq_ref[2 * p]) + _dot(tab_ref[nt + 2 * p + 1, c, :], pq_ref[2 * p + 1])
                u_ref[p, c] = x_ref[p, c] * (y + sgn * odd[p] + u_ref[p, c] * skip)

    proj_conv(u_ref, wv_ref, bv_ref, cwv_ref, cbv_ref)
    proj_conv(x_ref, wx1_ref, bx1_ref, cwx1_ref, cbx1_ref)
    long_conv(0)
    proj_conv(x_ref, wx2_ref, bx2_ref, cwx2_ref, cbx2_ref)
    long_conv(1)
    for p in range(HY_PH):
        for c in chunks:
            dst = pl.ds(HALO + p + HY_PH * c.start, HY_RC, stride=HY_PH)
            y = u_ref[p, c]
            for lt in range(nlt):
                z_ref[lt, dst, :] = y[:, lt * LANES:(lt + 1) * LANES]
    for c in row_chunks:
        y = jnp.concatenate([z_ref[lt, HALO + c.start:HALO + c.stop, :] for lt in range(nlt)], axis=1)
        g = _dot(xn_ref[0, c, :], wg_ref[...]) + bg_ref[...]
        o_ref[0, c, :] = (y * (g * jax.nn.sigmoid(g))).astype(o_ref.dtype)


def _hyena(xs, w_in_b, b_in, conv_w, conv_b, tab, kr, ki, ks, skip):
    _, B, L, _ = xs.shape
    M = L // HY_PH
    nct = HY_WIDTH // HY_CT
    hg = O_HGATE // HY_CT

    def col(k):
        return lambda j, b: (0, k * nct + j)

    wspec = lambda k: _const_spec((D_MODEL, HY_CT), col(k))
    bspec = lambda k: pl.BlockSpec((1, HY_CT), col(k))
    cwspec = lambda k: pl.BlockSpec((HY_SHORT_CONV, HY_CT), col(k))
    kspec = _const_spec((HY_ORDER, HY_PH, M, HY_CT), lambda j, b: (0, 0, 0, j))
    return pl.pallas_call(
        _hyena_kernel,
        grid=(nct, B),
        in_specs=[pl.BlockSpec((None, 1, L, D_MODEL), lambda j, b: (NAT, b, 0, 0)),
                  wspec(0), wspec(1), wspec(2), _const_spec((D_MODEL, HY_CT), lambda j, b: (0, hg + j)),
                  bspec(0), bspec(1), bspec(2), pl.BlockSpec((1, HY_CT), lambda j, b: (0, hg + j)),
                  cwspec(0), cwspec(1), cwspec(2), bspec(0), bspec(1), bspec(2),
                  _const_spec(tab.shape, lambda j, b: (0, 0, 0)),
                  kspec, kspec,
                  pl.BlockSpec((HY_ORDER, 4, HY_CT), lambda j, b: (0, 0, j)),
                  pl.BlockSpec((HY_ORDER, HY_CT), lambda j, b: (0, j))],
        out_specs=pl.BlockSpec((1, L, HY_CT), lambda j, b: (b, 0, j)),
        out_shape=jax.ShapeDtypeStruct((B, L, HY_WIDTH), bf16),
        scratch_shapes=[pltpu.VMEM((HY_CT // LANES, L + 2 * HALO, LANES), f32), pltpu.VMEM((HY_PH, M, HY_CT), f32),
                        pltpu.VMEM((HY_PH, M, HY_CT), bf16), pltpu.VMEM((2 * HY_PH, M, HY_CT), bf16),
                        pltpu.VMEM((HY_PH, M, HY_CT), f32)],
        compiler_params=pltpu.CompilerParams(dimension_semantics=("arbitrary", "arbitrary"),
                                             vmem_limit_bytes=VMEM_LIMIT),
        name="hyena_mixer",
    )(xs, w_in_b, w_in_b, w_in_b, w_in_b, b_in, b_in, b_in, b_in,
      conv_w, conv_w, conv_w, conv_b, conv_b, conv_b, tab, kr, ki, ks, skip)


def _attn_kernel(xs_ref, wq_ref, wk_ref, wv_ref, bq_ref, bk_ref, bv_ref, wag_ref, bag_ref,
                 gq_ref, gk_ref, hsum_ref, sl_ref, d0_ref, d1_ref, d2_ref, o_ref,
                 qs_ref, ks_ref, vs_ref, acc_ref, mx_ref, den_ref):
    L = xs_ref.shape[2]
    gw = HEADS_PER_GROUP * HEAD_DIM
    npair = HEADS_PER_GROUP // 2
    dist_refs = (d0_ref, d1_ref, d2_ref)
    tq = Q_TILE
    first = lax.broadcasted_iota(jnp.int32, (tq, PAIR), 1) < HEAD_DIM
    nt_dims = (((1,), (1,)), ((), ()))

    def normed(x, w_ref, b_ref, g_ref):
        z = _dot(x, w_ref[...]) + b_ref[...]
        z2 = (z * z).astype(bf16)
        ssq = jnp.concatenate([_dot(z2[:, c:c + MXU_DIM], hsum_ref[...]) for c in range(0, gw, MXU_DIM)], axis=1)
        return z * lax.rsqrt(ssq * (1.0 / HEAD_DIM) + NORM_EPS) * g_ref[...]

    def group(gi):
        _, d = DILATED_GROUPS[GROUP_ORDER[gi]]
        n = L // d
        w = min(2 * tq, n)
        per_class = n // tq
        dist_ref = dist_refs[GROUP_ORDER[gi]]

        for r0 in range(0, L, AT_RC):
            rows = slice(r0, r0 + AT_RC)
            x = xs_ref[0, 0, rows, :]
            qs_ref[rows] = (normed(x, wq_ref, bq_ref, gq_ref) * (HEAD_DIM ** -0.5)).astype(bf16)
            ks_ref[rows] = normed(x, wk_ref, bk_ref, gk_ref).astype(bf16)
            vs_ref[rows] = (_dot(x, wv_ref[...]) + bv_ref[...]).astype(bf16)

        def tile(idx, carry):
            r = idx // per_class
            t = idx % per_class
            q0 = pl.multiple_of(idx * tq, tq)
            koff = jnp.clip(t * tq - BAND_HALF, 0, n - w)
            dist_t = dist_ref[(t * tq - koff) // BAND_HALF]
            k0 = pl.multiple_of(r * n + koff, BAND_HALF)
            nat = pl.ds(t * tq * d + r, tq, stride=d) if d > 1 else pl.ds(q0, tq)
            for p in range(npair):
                pc = slice(p * PAIR, (p + 1) * PAIR)
                q = qs_ref[pl.ds(q0, tq), pc]
                zero = jnp.zeros_like(q)
                qq = jnp.concatenate([jnp.where(first, q, zero), jnp.where(first, zero, q)], axis=0)
                lhs = jnp.concatenate([qq, sl_ref[p]], axis=1)
                rhs_t = jnp.concatenate([ks_ref[pl.ds(k0, w), pc], dist_t], axis=1)
                s = lax.dot_general(lhs, rhs_t, nt_dims, preferred_element_type=f32)
                m = jnp.max(s, axis=-1, keepdims=True)
                pr = jnp.exp(s - m).astype(bf16)
                rhs = jnp.concatenate([vs_ref[pl.ds(k0, w), pc], jnp.ones((w, PAIR), bf16)], axis=1)
                ov = _dot(pr, rhs)
                num = jnp.where(first, ov[0:tq, 0:PAIR], ov[tq:2 * tq, 0:PAIR])
                den = jnp.where(first, ov[0:tq, PAIR:2 * PAIR], ov[tq:2 * tq, PAIR:2 * PAIR])
                mb = jnp.where(first, m[0:tq], m[tq:2 * tq])
                if gi == 0:
                    acc_ref[p, nat, :] = num
                    den_ref[p, nat, :] = den
                    mx_ref[p, nat, :] = mb
                else:
                    m_old = mx_ref[p, nat, :]
                    m_new = jnp.maximum(m_old, mb)
                    a = jnp.exp(m_old - m_new)
                    b = jnp.exp(mb - m_new)
                    acc_ref[p, nat, :] = acc_ref[p, nat, :] * a + num * b
                    den_ref[p, nat, :] = den_ref[p, nat, :] * a + den * b
                    mx_ref[p, nat, :] = m_new
            return carry

        lax.fori_loop(0, L // tq, tile, 0, unroll=8)

    for gi in range(N_GROUPS):
        pl.when(pl.program_id(1) == gi)(functools.partial(group, gi))

    @pl.when(pl.program_id(1) == N_GROUPS - 1)
    def _():
        for r0 in range(0, L, AT_RC):
            rows = slice(r0, r0 + AT_RC)
            ag = _dot(xs_ref[0, 0, rows, :], wag_ref[...]) + bag_ref[...]
            o = jnp.concatenate([acc_ref[p, rows, :] / den_ref[p, rows, :] for p in range(npair)], axis=1)
            o_ref[0, rows, :] = (o * (ag * jax.nn.sigmoid(ag))).astype(o_ref.dtype)


def _attention(xs, w_in_b, b_in, gq, gk, hsum):
    _, B, L, _ = xs.shape
    assert PERM_DILATIONS == tuple(DILATED_GROUPS[g][1] for g in GROUP_ORDER) and PERM_DILATIONS[-1] == 1
    assert GROUP_ORDER == tuple(N_GROUPS - 1 - i for i in range(N_GROUPS))
    assert L % AT_RC == 0 and all(L % (Q_TILE * d) == 0 for _, d in DILATED_GROUPS)
    gw = HEADS_PER_GROUP * HEAD_DIM
    npair = HEADS_PER_GROUP // 2
    dists = [jnp.asarray(_attn_dist(L // d, d, window)).astype(bf16) for window, d in DILATED_GROUPS]
    sl = jnp.asarray(_slope_eye()).astype(bf16)
    agb = O_AGATE // gw

    def col(k):
        return lambda b, i: (0, (O_QKV + k * AT_QKV) // gw + (N_GROUPS - 1 - i))

    wspec = lambda k: pl.BlockSpec((D_MODEL, gw), col(k))
    bspec = lambda k: pl.BlockSpec((1, gw), col(k))
    vec = pl.BlockSpec((1, gw), lambda b, i: (0, 0))
    acc = pltpu.VMEM((npair, L, PAIR), f32)
    return pl.pallas_call(
        _attn_kernel,
        grid=(B, N_GROUPS),
        in_specs=[pl.BlockSpec((1, 1, L, D_MODEL), lambda b, i: (i, b, 0, 0)),
                  wspec(0), wspec(1), wspec(2), bspec(0), bspec(1), bspec(2),
                  _const_spec((D_MODEL, gw), lambda b, i: (0, agb)), pl.BlockSpec((1, gw), lambda b, i: (0, agb)),
                  vec, vec, _const_spec(hsum.shape, lambda b, i: (0, 0)), _const_spec(sl.shape, lambda b, i: (0, 0, 0))]
                 + [_const_spec(t.shape, lambda b, i: (0, 0, 0)) for t in dists],
        out_specs=pl.BlockSpec((1, L, gw), lambda b, i: (b, 0, 0)),
        out_shape=jax.ShapeDtypeStruct((B, L, gw), bf16),
        scratch_shapes=[pltpu.VMEM((L, gw), bf16), pltpu.VMEM((L, gw), bf16), pltpu.VMEM((L, gw), bf16),
                        acc, acc, acc],
        compiler_params=pltpu.CompilerParams(dimension_semantics=("arbitrary", "arbitrary"),
                                             vmem_limit_bytes=VMEM_LIMIT),
        name="dilated_attention",
    )(xs, w_in_b, w_in_b, w_in_b, b_in, b_in, b_in, w_in_b, b_in, gq, gk, hsum, sl, *dists)


def _final_kernel(x_ref, xn_ref, gh_ref, ga_ref, wg_ref, bg_ref, why_ref, wat_ref, wout_ref, out_ref):
    gates = _dot(xn_ref[...], wg_ref[...]) + bg_ref[...]
    u_h = _dot(gh_ref[...], why_ref[...])
    u_a = _dot(ga_ref[...], wat_ref[...])
    merged = jax.nn.sigmoid(gates[:, 0:D_MODEL]) * u_h + jax.nn.sigmoid(gates[:, D_MODEL:]) * u_a
    out_ref[...] = x_ref[...] + _dot(merged.astype(bf16), wout_ref[...])


def _final(x2, xs2, gh2, ga2, wg, bg, why, wat, wout):
    rows = x2.shape[0]
    tm = 512
    rspec = lambda c: pl.BlockSpec((tm, c), lambda i: (i, 0))
    cspec = lambda a: _const_spec(a.shape, lambda i: (0, 0))
    return pl.pallas_call(
        _final_kernel,
        grid=(rows // tm,),
        in_specs=[rspec(D_MODEL), pl.BlockSpec((None, tm, D_MODEL), lambda i: (NAT, i, 0)), rspec(HY_WIDTH), rspec(AT_WIDTH),
                  cspec(wg), cspec(bg), cspec(why), cspec(wat), cspec(wout)],
        out_specs=rspec(D_MODEL),
        out_shape=jax.ShapeDtypeStruct((rows, D_MODEL), f32),
        compiler_params=pltpu.CompilerParams(dimension_semantics=("arbitrary",),
                                             vmem_limit_bytes=VMEM_LIMIT),
        name="merge_output",
    )(x2, xs2, gh2, ga2, wg, bg, why, wat, wout)


def _layer(x, norm_g, w_in, b_in, conv_w, conv_b, hf_w1, hf_b1, hf_w2, hf_b2, hf_w3, hf_b3, hf_w4,
           hf_freq, hy_skip, q_norm_g, k_norm_g, w_hy_out, w_at_out, w_out):
    B, L, D = x.shape
    x2 = x.reshape(B * L, D)
    tab = jnp.asarray(_dft_tables(L)).astype(bf16)
    w_in_b = w_in.astype(bf16)
    b_in2 = b_in.astype(f32).reshape(1, IN_COLS)

    xs = _prenorm(x, norm_g.astype(f32).reshape(1, D))

    kr, ki, ks = _filters(L, tab, hf_w1, hf_b1, hf_w2, hf_b2, hf_w3, hf_b3, hf_w4, hf_freq)
    gh = _hyena(xs, w_in_b, b_in2, conv_w.astype(f32), conv_b.astype(f32).reshape(1, -1),
                tab, kr, ki, ks, hy_skip.astype(f32))

    gq = jnp.tile(q_norm_g.astype(f32), HEADS_PER_GROUP).reshape(1, -1)
    gk = jnp.tile(k_norm_g.astype(f32), HEADS_PER_GROUP).reshape(1, -1)
    head = np.arange(MXU_DIM) // HEAD_DIM
    hsum = jnp.asarray((head[:, None] == head[None, :]).astype(np.float32)).astype(bf16)
    ga = _attention(xs, w_in_b, b_in2, gq, gk, hsum)

    out = _final(x2, xs.reshape(len(PERM_DILATIONS), B * L, D), gh.reshape(B * L, HY_WIDTH), ga.reshape(B * L, AT_WIDTH),
                 w_in_b[:, O_MG:], b_in2[:, O_MG:],
                 w_hy_out.astype(bf16), w_at_out.astype(bf16), w_out.astype(bf16))
    return out.reshape(B, L, D)


def kernel(x, norm_g, w_in, b_in, conv_w, conv_b, hf_w1, hf_b1, hf_w2, hf_b2, hf_w3, hf_b3, hf_w4,
           hf_freq, hy_skip, q_norm_g, k_norm_g, w_hy_out, w_at_out, w_out):
    depth = norm_g.shape[0]
    for i in range(depth):
        x = _layer(x, norm_g[i], w_in[i], b_in[i], conv_w[i], conv_b[i], hf_w1[i], hf_b1[i], hf_w2[i],
                   hf_b2[i], hf_w3[i], hf_b3[i], hf_w4[i], hf_freq[i], hy_skip[i], q_norm_g[i],
                   k_norm_g[i], w_hy_out[i], w_at_out[i], w_out[i])
    return x
```

```python
import functools
import math

import jax
import jax.numpy as jnp
import numpy as np
from jax import lax
from jax.experimental import pallas as pl
from jax.experimental.pallas import tpu as pltpu

D_MODEL = 1024
HY_WIDTH = 768
HY_ORDER = 2
HY_SHORT_CONV = 3
HY_EMB_DIM = 33
HY_FILTER_HIDDEN = 64
HY_FAST_DECAY = 0.3
HY_SLOW_DECAY = 1.5
HY_DECAY_TARGET = 1e-2
HY_MOD_SHIFT = 0.0
HEAD_DIM = 64
HEADS_PER_GROUP = 8
DILATED_GROUPS = ((128, 1), (512, 4), (2048, 16))
N_GROUPS = 3
AT_QKV = N_GROUPS * HEADS_PER_GROUP * HEAD_DIM
AT_WIDTH = HEADS_PER_GROUP * HEAD_DIM
NORM_EPS = 1e-6
NEG_INF = -1e30

O_HY = 0
O_HGATE = 3 * HY_WIDTH
O_QKV = O_HGATE + HY_WIDTH
O_AGATE = O_QKV + 3 * AT_QKV
O_MG = O_AGATE + AT_WIDTH
IN_COLS = O_MG + 2 * D_MODEL

LANES = 128
MXU_DIM = 256
VMEM_LIMIT = 56 * 1024 * 1024

HY_CT = 256
HY_RC = 512
HY_PH = 4
RSQRT2 = math.sqrt(0.5)
HALO = 8
AT_RC = 512
EMB_PAD = 128
Q_TILE = 128
BAND_HALF = 64
GROUP_ORDER = (2, 1, 0)
PERM_DILATIONS = (16, 4, 1)
NAT = 2
PN_RC = 512
PAIR = 2 * HEAD_DIM

f32 = jnp.float32
bf16 = jnp.bfloat16


def _dot(a, b):
    return jnp.dot(a, b, preferred_element_type=f32)


def _const_spec(shape, index_map):
    return pl.BlockSpec(shape, index_map, pipeline_mode=pl.Buffered(1))


@functools.lru_cache(maxsize=None)
def _dft_tables(L):
    n = 2 * L
    f = np.arange(L // HY_PH, dtype=np.int64)[:, None]
    m = np.arange(L // HY_PH, dtype=np.int64)[None, :]
    fwd = []
    for p in range(HY_PH):
        ang = ((f * (HY_PH * m + p)) % n).astype(np.float64) * (2.0 * np.pi / n)
        fwd += [np.cos(ang), np.sin(ang)]
    return np.stack(fwd + [t.T for t in fwd]).astype(np.float32)


def _butterfly(A, B):
    e = (A[0] + A[2], A[0] - A[2], A[1] + A[3], A[1] - A[3])
    f = (B[0] + B[2], B[0] - B[2], B[1] + B[3], B[1] - B[3])
    return e, f


def _spectrum_cos(e, f):
    return (e[0] + e[2], e[1] - f[3], e[0] - e[2], e[1] + f[3])


def _spectrum_sin(e, f):
    return (f[0] + f[2], f[1] + e[3], f[2] - f[0], e[3] - f[1])


def _odd_bins(r):
    d, s = (r[1] - r[3]) * RSQRT2, (r[1] + r[3]) * RSQRT2
    return (r[0] + d, s + r[2]), (r[0] - d, s - r[2])


def _phase_major(a, L):
    return np.concatenate([a[p::HY_PH] for p in range(HY_PH)], axis=0)


@functools.lru_cache(maxsize=None)
def _filter_embedding(L):
    t = np.linspace(0.0, 1.0, L)[:, None]
    bands = (HY_EMB_DIM - 1) // 2
    w = 2.0 * np.pi * np.arange(L)[:, None] / L
    f = np.linspace(1e-4, bands - 1, bands)[None, :]
    z = np.concatenate([t, np.cos(f * w), -np.sin(f * w)], axis=-1)
    zp = np.zeros((L, EMB_PAD), np.float64)
    zp[:, :HY_EMB_DIM] = z
    return zp.astype(np.float32)


@functools.lru_cache(maxsize=None)
def _decay_rates():
    max_decay = math.log(HY_DECAY_TARGET) / HY_FAST_DECAY
    min_decay = math.log(HY_DECAY_TARGET) / HY_SLOW_DECAY
    return np.abs(np.linspace(min_decay, max_decay, HY_WIDTH))[None, :].astype(np.float32)


def _alibi_slope(h):
    return 2.0 ** (-8.0 * (h + 1) / HEADS_PER_GROUP)


@functools.lru_cache(maxsize=None)
def _attn_dist(n, dilation, window):
    half = window // (2 * dilation)
    assert half == BAND_HALF
    tq = min(Q_TILE, n)
    w = min(2 * Q_TILE, n)
    masked = NEG_INF / _alibi_slope(HEADS_PER_GROUP - 1)
    offs = sorted({q0 - min(max(q0 - half, 0), n - w) for q0 in range(0, n, tq)})
    assert offs == [BAND_HALF * i for i in range(len(offs))]
    out = np.zeros((len(offs), w, tq), np.float32)
    for ci, off in enumerate(offs):
        rel = np.arange(tq)[None, :] + off - np.arange(w)[:, None]
        out[ci] = np.where(np.abs(rel) <= half, -dilation * np.abs(rel), masked)
    return out


@functools.lru_cache(maxsize=None)
def _slope_eye():
    eye = np.eye(Q_TILE, dtype=np.float32)
    return np.stack([np.concatenate([_alibi_slope(2 * p) * eye, _alibi_slope(2 * p + 1) * eye], axis=0)
                     for p in range(HEADS_PER_GROUP // 2)])


@functools.lru_cache(maxsize=None)
def _residue_perm(n):
    p = np.zeros((n, n), np.float32)
    i = np.arange(n)
    p[(i % 4) * (n // 4) + i // 4, i] = 1.0
    return p


def _prenorm_kernel(x_ref, g_ref, perm_ref, o_ref, st_ref):
    L = x_ref.shape[1]
    nt = D_MODEL // LANES
    d4, d16 = PERM_DILATIONS.index(4), PERM_DILATIONS.index(16)
    n4, n16 = L // 4, L // 16
    for r0 in range(0, L, PN_RC):
        rows = slice(r0, r0 + PN_RC)
        x = x_ref[0, rows, :]
        ms = jnp.mean(x * x, axis=-1, keepdims=True)
        xn = x * lax.rsqrt(ms + NORM_EPS) * g_ref[...]
        o_ref[NAT, 0, rows, :] = xn.astype(o_ref.dtype)
        for c in range(nt):
            st_ref[c, rows, :] = xn[:, c * LANES:(c + 1) * LANES]

    def gather(r, carry):
        dst = pl.ds(pl.multiple_of(r * n4, n4), n4)
        for c in range(nt):
            o_ref[d4, 0, dst, c * LANES:(c + 1) * LANES] = st_ref[c, pl.ds(r, n4, stride=4), :].astype(o_ref.dtype)
        return carry

    lax.fori_loop(0, 4, gather, 0)
    for r4 in range(4):
        y = _dot(perm_ref[...], o_ref[d4, 0, r4 * n4:(r4 + 1) * n4, :]).astype(o_ref.dtype)
        for q in range(4):
            o_ref[d16, 0, (r4 + 4 * q) * n16:(r4 + 4 * q + 1) * n16, :] = y[q * n16:(q + 1) * n16]


def _prenorm(x, g):
    B, L, D = x.shape
    assert sorted(PERM_DILATIONS) == [1, 4, 16] and PERM_DILATIONS[NAT] == 1
    perm = jnp.asarray(_residue_perm(L // 4)).astype(bf16)
    return pl.pallas_call(
        _prenorm_kernel,
        grid=(B,),
        in_specs=[pl.BlockSpec((1, L, D), lambda b: (b, 0, 0)),
                  pl.BlockSpec((1, D), lambda b: (0, 0)),
                  _const_spec(perm.shape, lambda b: (0, 0))],
        out_specs=pl.BlockSpec((len(PERM_DILATIONS), 1, L, D), lambda b: (0, b, 0, 0)),
        out_shape=jax.ShapeDtypeStruct((len(PERM_DILATIONS), B, L, D), bf16),
        scratch_shapes=[pltpu.VMEM((D // LANES, L, LANES), f32)],
        compiler_params=pltpu.CompilerParams(dimension_semantics=("arbitrary",),
                                             vmem_limit_bytes=VMEM_LIMIT),
        name="prenorm",
    )(x, g, perm)


def _filters_kernel(z_ref, w1_ref, b1_ref, w2_ref, b2_ref, w3_ref, b3_ref, fr_ref, w4f_ref, w4b_ref,
                    t_ref, rate_ref, tab_ref, kr_ref, ki_ref, ks_ref, h3_ref):
    L = z_ref.shape[0]
    M = L // HY_PH
    hi = lax.Precision.HIGHEST

    @pl.when((pl.program_id(0) == 0) & (pl.program_id(1) == 0))
    def _():
        fr = fr_ref[...]
        h = jnp.sin(fr * (jnp.dot(z_ref[...], w1_ref[...], precision=hi, preferred_element_type=f32)
                          + b1_ref[...]))
        h = jnp.sin(fr * (jnp.dot(h, w2_ref[...], precision=hi, preferred_element_type=f32) + b2_ref[...]))
        h = jnp.sin(fr * (jnp.dot(h, w3_ref[...], precision=hi, preferred_element_type=f32) + b3_ref[...]))
        h3_ref[...] = h

    h3 = h3_ref[...]
    decay = jnp.exp(-t_ref[...] * rate_ref[...]) + HY_MOD_SHIFT
    hf = jnp.dot(h3, w4f_ref[...], precision=hi, preferred_element_type=f32) * decay
    hb = jnp.dot(h3, w4b_ref[...], precision=hi, preferred_element_type=f32) * decay
    hb0 = hb[0:1, :]
    hs = hf + hb
    hd = hb - hf
    n = 2 * L
    row = lax.broadcasted_iota(jnp.int32, (M, HY_CT), 0)
    sgn = jnp.where((row & 1) == 1, -1.0, 1.0).astype(f32)

    def transform(x):
        xp = [x[p * M:(p + 1) * M] for p in range(HY_PH)]
        A = [_dot(tab_ref[2 * p], xp[p].astype(bf16)) for p in range(HY_PH)]
        B = [_dot(tab_ref[2 * p + 1], xp[p].astype(bf16)) for p in range(HY_PH)]
        r = [jnp.sum(xp[p] * sgn, axis=0, keepdims=True) for p in range(HY_PH)]
        return _butterfly(A, B), _odd_bins(r)

    (es, fs), odd_s = transform(hs)
    (ed, fd), odd_d = transform(hd)
    kr = _spectrum_cos(es, fs)
    ki = _spectrum_sin(ed, fd)
    two = 2.0 / n
    edge = jnp.where(row == 0, 1.0 / n, two).astype(f32)
    once = jnp.where(row == 0, 0.0, two).astype(f32)
    for cls, wgt in enumerate((edge, two, edge, once)):
        kr_ref[0, cls] = (kr[cls] - hb0) * wgt
        ki_ref[0, cls] = ki[cls] * wgt
    for j in range(2):
        ks_ref[0, 2 * j:2 * j + 1, :] = (odd_s[j][0] - hb0) * two
        ks_ref[0, 2 * j + 1:2 * j + 2, :] = odd_d[j][1] * two


def _filters(L, tab, w1, b1, w2, b2, w3, b3, w4, freq):
    M = L // HY_PH
    z = jnp.asarray(_phase_major(_filter_embedding(L), L))
    t = jnp.asarray(_phase_major(np.linspace(0.0, 1.0, L)[:, None].astype(np.float32), L))
    rate = jnp.asarray(_decay_rates())
    w1p = jnp.zeros((EMB_PAD, HY_FILTER_HIDDEN), f32).at[:HY_EMB_DIM].set(w1.astype(f32))
    nct = HY_WIDTH // HY_CT
    row = lambda a: a.astype(f32).reshape(1, -1)
    full = lambda shape: pl.BlockSpec(shape, lambda o, j: (0,) * len(shape))
    H = HY_FILTER_HIDDEN
    kspec = pl.BlockSpec((1, HY_PH, M, HY_CT), lambda o, j: (o, 0, 0, j))
    kshape = jax.ShapeDtypeStruct((HY_ORDER, HY_PH, M, HY_WIDTH), f32)
    return pl.pallas_call(
        _filters_kernel,
        grid=(HY_ORDER, nct),
        in_specs=[full((L, EMB_PAD)), full((EMB_PAD, H)), full((1, H)), full((H, H)), full((1, H)),
                  full((H, H)), full((1, H)), full((1, H)),
                  pl.BlockSpec((H, HY_CT), lambda o, j: (0, 2 * nct * o + j)),
                  pl.BlockSpec((H, HY_CT), lambda o, j: (0, 2 * nct * o + nct + j)),
                  full((L, 1)),
                  pl.BlockSpec((1, HY_CT), lambda o, j: (0, j)),
                  _const_spec(tab.shape, lambda o, j: (0, 0, 0))],
        out_specs=[kspec, kspec, pl.BlockSpec((1, 4, HY_CT), lambda o, j: (o, 0, j))],
        out_shape=[kshape, kshape, jax.ShapeDtypeStruct((HY_ORDER, 4, HY_WIDTH), f32)],
        scratch_shapes=[pltpu.VMEM((L, H), f32)],
        compiler_params=pltpu.CompilerParams(dimension_semantics=("arbitrary", "arbitrary"),
                                             vmem_limit_bytes=VMEM_LIMIT),
        name="hyena_filters",
    )(z, w1p, row(b1), w2.astype(f32), row(b2), w3.astype(f32), row(b3), row(freq),
      w4.astype(f32), w4.astype(f32), t, rate, tab)


def _hyena_kernel(xn_ref, wv_ref, wx1_ref, wx2_ref, wg_ref, bv_ref, bx1_ref, bx2_ref, bg_ref,
                  cwv_ref, cwx1_ref, cwx2_ref, cbv_ref, cbx1_ref, cbx2_ref,
                  tab_ref, kr_ref, ki_ref, ks_ref, skip_ref, o_ref,
                  z_ref, u_ref, ub_ref, pq_ref, x_ref):
    L = xn_ref.shape[1]
    M = L // HY_PH
    nlt = HY_CT // LANES
    row_chunks = [slice(r, r + HY_RC) for r in range(0, L, HY_RC)]
    chunks = [slice(r, r + HY_RC) for r in range(0, M, HY_RC)]
    row = lax.broadcasted_iota(jnp.int32, (HY_RC, HY_CT), 0)
    sgn = jnp.where((row & 1) == 1, -1.0, 1.0).astype(f32)
    for lt in range(nlt):
        z_ref[lt, 0:HALO] = jnp.zeros((HALO, LANES), f32)
        z_ref[lt, L + HALO:L + 2 * HALO] = jnp.zeros((HALO, LANES), f32)

    def stage_natural(val, c):
        for lt in range(nlt):
            z_ref[lt, HALO + c.start:HALO + c.stop, :] = val[:, lt * LANES:(lt + 1) * LANES]

    def phase_rows(p, c, shift=0):
        src = pl.ds(HALO + p + shift + HY_PH * c.start, HY_RC, stride=HY_PH)
        return jnp.concatenate([z_ref[lt, src, :] for lt in range(nlt)], axis=1)

    def proj_conv(dst_ref, w_ref, b_ref, cw_ref, cb_ref):
        for c in row_chunks:
            stage_natural(_dot(xn_ref[0, c, :], w_ref[...]) + b_ref[...], c)
        for p in range(HY_PH):
            for c in chunks:
                dst_ref[p, c] = (cb_ref[...] + phase_rows(p, c, -1) * cw_ref[0:1, :]
                                 + phase_rows(p, c) * cw_ref[1:2, :] + phase_rows(p, c, 1) * cw_ref[2:3, :])

    def long_conv(o):
        r = [jnp.zeros((1, HY_CT), f32) for _ in range(HY_PH)]
        for p in range(HY_PH):
            for c in chunks:
                u = u_ref[p, c]
                ub_ref[p, c] = u.astype(bf16)
                r[p] = r[p] + jnp.sum(u * sgn, axis=0, keepdims=True)
        for c in chunks:
            A = [_dot(tab_ref[2 * p, c, :], ub_ref[p]) for p in range(HY_PH)]
            B = [_dot(tab_ref[2 * p + 1, c, :], ub_ref[p]) for p in range(HY_PH)]
            e, f = _butterfly(A, B)
            a, b = _spectrum_cos(e, f), _spectrum_sin(e, f)
            P, Q = [], []
            for cls in range(HY_PH):
                kr, ki = kr_ref[o, cls, c, :], ki_ref[o, cls, c, :]
                P.append(a[cls] * kr + b[cls] * ki)
                Q.append(b[cls] * kr - a[cls] * ki)
            g = (P[0] + P[2], P[0] - P[2], P[1] + P[3], P[3] - P[1])
            h = (Q[0] - Q[2], Q[0] + Q[2], Q[1] - Q[3], Q[1] + Q[3])
            X = (g[0] + g[2], g[1] + h[3], g[0] - g[2], g[1] - h[3])
            Y = (h[0] + h[2], h[1] + g[3], h[0] - h[2], h[1] - g[3])
            for p in range(HY_PH):
                pq_ref[2 * p, c] = X[p].astype(bf16)
                pq_ref[2 * p + 1, c] = Y[p].astype(bf16)
        pq_odd = []
        for j, (a_o, b_o) in enumerate(_odd_bins(r)):
            kr, ki = ks_ref[o, 2 * j:2 * j + 1, :], ks_ref[o, 2 * j + 1:2 * j + 2, :]
            pq_odd.append((a_o * kr + b_o * ki, b_o * kr - a_o * ki))
        (p1, q1), (p3, q3) = pq_odd
        odd = (p1 + p3, (p1 + q1 - p3 + q3) * RSQRT2, q1 - q3, (q1 - p1 + p3 + q3) * RSQRT2)
        skip = skip_ref[o:o + 1, :]
        nt = 2 * HY_PH
        for p in range(HY_PH):
            for c in chunks:
                y = _dot(tab_ref[nt + 2 * p, c, :], pq_ref[2 * p]) + _dot(tab_ref[nt + 2 * p + 1, c, :], pq_ref[2 * p + 1])
                u_ref[p, c] = x_ref[p, c] * (y + sgn * odd[p] + u_ref[p, c] * skip)

    proj_conv(u_ref, wv_ref, bv_ref, cwv_ref, cbv_ref)
    proj_conv(x_ref, wx1_ref, bx1_ref, cwx1_ref, cbx1_ref)
    long_conv(0)
    proj_conv(x_ref, wx2_ref, bx2_ref, cwx2_ref, cbx2_ref)
    long_conv(1)
    for p in range(HY_PH):
        for c in chunks:
            dst = pl.ds(HALO + p + HY_PH * c.start, HY_RC, stride=HY_PH)
            y = u_ref[p, c]
            for lt in range(nlt):
                z_ref[lt, dst, :] = y[:, lt * LANES:(lt + 1) * LANES]
    for c in row_chunks:
        y = jnp.concatenate([z_ref[lt, HALO + c.start:HALO + c.stop, :] for lt in range(nlt)], axis=1)
        g = _dot(xn_ref[0, c, :], wg_ref[...]) + bg_ref[...]
        o_ref[0, c, :] = (y * (g * jax.nn.sigmoid(g))).astype(o_ref.dtype)


def _hyena(xs, w_in_b, b_in, conv_w, conv_b, tab, kr, ki, ks, skip):
    _, B, L, _ = xs.shape
    M = L // HY_PH
    nct = HY_WIDTH // HY_CT
    hg = O_HGATE // HY_CT

    def col(k):
        return lambda j, b: (0, k * nct + j)

    wspec = lambda k: _const_spec((D_MODEL, HY_CT), col(k))
    bspec = lambda k: pl.BlockSpec((1, HY_CT), col(k))
    cwspec = lambda k: pl.BlockSpec((HY_SHORT_CONV, HY_CT), col(k))
    kspec = _const_spec((HY_ORDER, HY_PH, M, HY_CT), lambda j, b: (0, 0, 0, j))
    return pl.pallas_call(
        _hyena_kernel,
        grid=(nct, B),
        in_specs=[pl.BlockSpec((None, 1, L, D_MODEL), lambda j, b: (NAT, b, 0, 0)),
                  wspec(0), wspec(1), wspec(2), _const_spec((D_MODEL, HY_CT), lambda j, b: (0, hg + j)),
                  bspec(0), bspec(1), bspec(2), pl.BlockSpec((1, HY_CT), lambda j, b: (0, hg + j)),
                  cwspec(0), cwspec(1), cwspec(2), bspec(0), bspec(1), bspec(2),
                  _const_spec(tab.shape, lambda j, b: (0, 0, 0)),
                  kspec, kspec,
                  pl.BlockSpec((HY_ORDER, 4, HY_CT), lambda j, b: (0, 0, j)),
                  pl.BlockSpec((HY_ORDER, HY_CT), lambda j, b: (0, j))],
        out_specs=pl.BlockSpec((1, L, HY_CT), lambda j, b: (b, 0, j)),
        out_shape=jax.ShapeDtypeStruct((B, L, HY_WIDTH), bf16),
        scratch_shapes=[pltpu.VMEM((HY_CT // LANES, L + 2 * HALO, LANES), f32), pltpu.VMEM((HY_PH, M, HY_CT), f32),
                        pltpu.VMEM((HY_PH, M, HY_CT), bf16), pltpu.VMEM((2 * HY_PH, M, HY_CT), bf16),
                        pltpu.VMEM((HY_PH, M, HY_CT), f32)],
        compiler_params=pltpu.CompilerParams(dimension_semantics=("arbitrary", "arbitrary"),
                                             vmem_limit_bytes=VMEM_LIMIT),
        name="hyena_mixer",
    )(xs, w_in_b, w_in_b, w_in_b, w_in_b, b_in, b_in, b_in, b_in,
      conv_w, conv_w, conv_w, conv_b, conv_b, conv_b, tab, kr, ki, ks, skip)


def _attn_kernel(xs_ref, wq_ref, wk_ref, wv_ref, bq_ref, bk_ref, bv_ref, wag_ref, bag_ref,
                 gq_ref, gk_ref, hsum_ref, sl_ref, d0_ref, d1_ref, d2_ref, o_ref,
                 qs_ref, ks_ref, vs_ref, acc_ref, mx_ref, den_ref):
    L = xs_ref.shape[2]
    gw = HEADS_PER_GROUP * HEAD_DIM
    npair = HEADS_PER_GROUP // 2
    dist_refs = (d0_ref, d1_ref, d2_ref)
    tq = Q_TILE
    first = lax.broadcasted_iota(jnp.int32, (tq, PAIR), 1) < HEAD_DIM
    nt_dims = (((1,), (1,)), ((), ()))

    def normed(x, w_ref, b_ref, g_ref):
        z = _dot(x, w_ref[...]) + b_ref[...]
        z2 = (z * z).astype(bf16)
        ssq = jnp.concatenate([_dot(z2[:, c:c + MXU_DIM], hsum_ref[...]) for c in range(0, gw, MXU_DIM)], axis=1)
        return z * lax.rsqrt(ssq * (1.0 / HEAD_DIM) + NORM_EPS) * g_ref[...]

    def group(gi):
        _, d = DILATED_GROUPS[GROUP_ORDER[gi]]
        n = L // d
        w = min(2 * tq, n)
        per_class = n // tq
        dist_ref = dist_refs[GROUP_ORDER[gi]]

        for r0 in range(0, L, AT_RC):
            rows = slice(r0, r0 + AT_RC)
            x = xs_ref[0, 0, rows, :]
            qs_ref[rows] = (normed(x, wq_ref, bq_ref, gq_ref) * (HEAD_DIM ** -0.5)).astype(bf16)
            ks_ref[rows] = normed(x, wk_ref, bk_ref, gk_ref).astype(bf16)
            vs_ref[rows] = (_dot(x, wv_ref[...]) + bv_ref[...]).astype(bf16)

        def tile(idx, carry):
            r = idx // per_class
            t = idx % per_class
            q0 = pl.multiple_of(idx * tq, tq)
            koff = jnp.clip(t * tq - BAND_HALF, 0, n - w)
            dist_t = dist_ref[(t * tq - koff) // BAND_HALF]
            k0 = pl.multiple_of(r * n + koff, BAND_HALF)
            nat = pl.ds(t * tq * d + r, tq, stride=d) if d > 1 else pl.ds(q0, tq)
            for p in range(npair):
                pc = slice(p * PAIR, (p + 1) * PAIR)
                q = qs_ref[pl.ds(q0, tq), pc]
                zero = jnp.zeros_like(q)
                qq = jnp.concatenate([jnp.where(first, q, zero), jnp.where(first, zero, q)], axis=0)
                lhs = jnp.concatenate([qq, sl_ref[p]], axis=1)
                rhs_t = jnp.concatenate([ks_ref[pl.ds(k0, w), pc], dist_t], axis=1)
                s = lax.dot_general(lhs, rhs_t, nt_dims, preferred_element_type=f32)
                m = jnp.max(s, axis=-1, keepdims=True)
                pr = jnp.exp(s - m).astype(bf16)
                rhs = jnp.concatenate([vs_ref[pl.ds(k0, w), pc], jnp.ones((w, PAIR), bf16)], axis=1)
                ov = _dot(pr, rhs)
                num = jnp.where(first, ov[0:tq, 0:PAIR], ov[tq:2 * tq, 0:PAIR])
                den = jnp.where(first, ov[0:tq, PAIR:2 * PAIR], ov[tq:2 * tq, PAIR:2 * PAIR])
                mb = jnp.where(first, m[0:tq], m[tq:2 * tq])
                if gi == 0:
                    acc_ref[p, nat, :] = num
                    den_ref[p, nat, :] = den
                    mx_ref[p, nat, :] = mb
                else:
                    m_old = mx_ref[p, nat, :]
                    m_new = jnp.maximum(m_old, mb)
                    a = jnp.exp(m_old - m_new)
                    b = jnp.exp(mb - m_new)
                    acc_ref[p, nat, :] = acc_ref[p, nat, :] * a + num * b
                    den_ref[p, nat, :] = den_ref[p, nat, :] * a + den * b
                    mx_ref[p, nat, :] = m_new
            return carry

        lax.fori_loop(0, L // tq, tile, 0, unroll=8)

    for gi in range(N_GROUPS):
        pl.when(pl.program_id(1) == gi)(functools.partial(group, gi))

    @pl.when(pl.program_id(1) == N_GROUPS - 1)
    def _():
        for r0 in range(0, L, AT_RC):
            rows = slice(r0, r0 + AT_RC)
            ag = _dot(xs_ref[0, 0, rows, :], wag_ref[...]) + bag_ref[...]
            o = jnp.concatenate([acc_ref[p, rows, :] / den_ref[p, rows, :] for p in range(npair)], axis=1)
            o_ref[0, rows, :] = (o * (ag * jax.nn.sigmoid(ag))).astype(o_ref.dtype)


def _attention(xs, w_in_b, b_in, gq, gk, hsum):
    _, B, L, _ = xs.shape
    assert PERM_DILATIONS == tuple(DILATED_GROUPS[g][1] for g in GROUP_ORDER) and PERM_DILATIONS[-1] == 1
    assert GROUP_ORDER == tuple(N_GROUPS - 1 - i for i in range(N_GROUPS))
    assert L % AT_RC == 0 and all(L % (Q_TILE * d) == 0 for _, d in DILATED_GROUPS)
    gw = HEADS_PER_GROUP * HEAD_DIM
    npair = HEADS_PER_GROUP // 2
    dists = [jnp.asarray(_attn_dist(L // d, d, window)).astype(bf16) for window, d in DILATED_GROUPS]
    sl = jnp.asarray(_slope_eye()).astype(bf16)
    agb = O_AGATE // gw

    def col(k):
        return lambda b, i: (0, (O_QKV + k * AT_QKV) // gw + (N_GROUPS - 1 - i))

    wspec = lambda k: pl.BlockSpec((D_MODEL, gw), col(k))
    bspec = lambda k: pl.BlockSpec((1, gw), col(k))
    vec = pl.BlockSpec((1, gw), lambda b, i: (0, 0))
    acc = pltpu.VMEM((npair, L, PAIR), f32)
    return pl.pallas_call(
        _attn_kernel,
        grid=(B, N_GROUPS),
        in_specs=[pl.BlockSpec((1, 1, L, D_MODEL), lambda b, i: (i, b, 0, 0)),
                  wspec(0), wspec(1), wspec(2), bspec(0), bspec(1), bspec(2),
                  _const_spec((D_MODEL, gw), lambda b, i: (0, agb)), pl.BlockSpec((1, gw), lambda b, i: (0, agb)),
                  vec, vec, _const_spec(hsum.shape, lambda b, i: (0, 0)), _const_spec(sl.shape, lambda b, i: (0, 0, 0))]
                 + [_const_spec(t.shape, lambda b, i: (0, 0, 0)) for t in dists],
        out_specs=pl.BlockSpec((1, L, gw), lambda b, i: (b, 0, 0)),
        out_shape=jax.ShapeDtypeStruct((B, L, gw), bf16),
        scratch_shapes=[pltpu.VMEM((L, gw), bf16), pltpu.VMEM((L, gw), bf16), pltpu.VMEM((L, gw), bf16),
                        acc, acc, acc],
        compiler_params=pltpu.CompilerParams(dimension_semantics=("arbitrary", "arbitrary"),
                                             vmem_limit_bytes=VMEM_LIMIT),
        name="dilated_attention",
    )(xs, w_in_b, w_in_b, w_in_b, b_in, b_in, b_in, w_in_b, b_in, gq, gk, hsum, sl, *dists)


def _final_kernel(x_ref, xn_ref, gh_ref, ga_ref, wg_ref, bg_ref, why_ref, wat_ref, wout_ref, out_ref):
    gates = _dot(xn_ref[...], wg_ref[...]) + bg_ref[...]
    u_h = _dot(gh_ref[...], why_ref[...])
    u_a = _dot(ga_ref[...], wat_ref[...])
    merged = jax.nn.sigmoid(gates[:, 0:D_MODEL]) * u_h + jax.nn.sigmoid(gates[:, D_MODEL:]) * u_a
    out_ref[...] = x_ref[...] + _dot(merged.astype(bf16), wout_ref[...])


def _final(x2, xs2, gh2, ga2, wg, bg, why, wat, wout):
    rows = x2.shape[0]
    tm = 512
    rspec = lambda c: pl.BlockSpec((tm, c), lambda i: (i, 0))
    cspec = lambda a: _const_spec(a.shape, lambda i: (0, 0))
    return pl.pallas_call(
        _final_kernel,
        grid=(rows // tm,),
        in_specs=[rspec(D_MODEL), pl.BlockSpec((None, tm, D_MODEL), lambda i: (NAT, i, 0)), rspec(HY_WIDTH), rspec(AT_WIDTH),
                  cspec(wg), cspec(bg), cspec(why), cspec(wat), cspec(wout)],
        out_specs=rspec(D_MODEL),
        out_shape=jax.ShapeDtypeStruct((rows, D_MODEL), f32),
        compiler_params=pltpu.CompilerParams(dimension_semantics=("arbitrary",),
                                             vmem_limit_bytes=VMEM_LIMIT),
        name="merge_output",
    )(x2, xs2, gh2, ga2, wg, bg, why, wat, wout)


def _layer(x, norm_g, w_in, b_in, conv_w, conv_b, hf_w1, hf_b1, hf_w2, hf_b2, hf_w3, hf_b3, hf_w4,
           hf_freq, hy_skip, q_norm_g, k_norm_g, w_hy_out, w_at_out, w_out):
    B, L, D = x.shape
    x2 = x.reshape(B * L, D)
    tab = jnp.asarray(_dft_tables(L)).astype(bf16)
    w_in_b = w_in.astype(bf16)
    b_in2 = b_in.astype(f32).reshape(1, IN_COLS)

    xs = _prenorm(x, norm_g.astype(f32).reshape(1, D))

    kr, ki, ks = _filters(L, tab, hf_w1, hf_b1, hf_w2, hf_b2, hf_w3, hf_b3, hf_w4, hf_freq)
    gh = _hyena(xs, w_in_b, b_in2, conv_w.astype(f32), conv_b.astype(f32).reshape(1, -1),
                tab, kr, ki, ks, hy_skip.astype(f32))

    gq = jnp.tile(q_norm_g.astype(f32), HEADS_PER_GROUP).reshape(1, -1)
    gk = jnp.tile(k_norm_g.astype(f32), HEADS_PER_GROUP).reshape(1, -1)
    head = np.arange(MXU_DIM) // HEAD_DIM
    hsum = jnp.asarray((head[:, None] == head[None, :]).astype(np.float32)).astype(bf16)
    ga = _attention(xs, w_in_b, b_in2, gq, gk, hsum)

    out = _final(x2, xs.reshape(len(PERM_DILATIONS), B * L, D), gh.reshape(B * L, HY_WIDTH), ga.reshape(B * L, AT_WIDTH),
                 w_in_b[:, O_MG:], b_in2[:, O_MG:],
                 w_hy_out.astype(bf16), w_at_out.astype(bf16), w_out.astype(bf16))
    return out.reshape(B, L, D)


def kernel(x, norm_g, w_in, b_in, conv_w, conv_b, hf_w1, hf_b1, hf_w2, hf_b2, hf_w3, hf_b3, hf_w4,
           hf_freq, hy_skip, q_norm_g, k_norm_g, w_hy_out, w_at_out, w_out):
    depth = norm_g.shape[0]
    for i in range(depth):
        x = _layer(x, norm_g[i], w_in[i], b_in[i], conv_w[i], conv_b[i], hf_w1[i], hf_b1[i], hf_w2[i],
                   hf_b2[i], hf_w3[i], hf_b3[i], hf_w4[i], hf_freq[i], hy_skip[i], q_norm_g[i],
                   k_norm_g[i], w_hy_out[i], w_at_out[i], w_out[i])
    return x
```

```python
import functools
import math

import jax
import jax.numpy as jnp
import numpy as np
from jax import lax
from jax.experimental import pallas as pl
from jax.experimental.pallas import tpu as pltpu

D_MODEL = 1024
HY_WIDTH = 768
HY_ORDER = 2
HY_SHORT_CONV = 3
HY_EMB_DIM = 33
HY_FILTER_HIDDEN = 64
HY_FAST_DECAY = 0.3
HY_SLOW_DECAY = 1.5
HY_DECAY_TARGET = 1e-2
HY_MOD_SHIFT = 0.0
HEAD_DIM = 64
HEADS_PER_GROUP = 8
DILATED_GROUPS = ((128, 1), (512, 4), (2048, 16))
N_GROUPS = 3
AT_QKV = N_GROUPS * HEADS_PER_GROUP * HEAD_DIM
AT_WIDTH = HEADS_PER_GROUP * HEAD_DIM
NORM_EPS = 1e-6
NEG_INF = -1e30

O_HGATE = 3 * HY_WIDTH
O_QKV = O_HGATE + HY_WIDTH
O_AGATE = O_QKV + 3 * AT_QKV
O_MG = O_AGATE + AT_WIDTH
IN_COLS = O_MG + 2 * D_MODEL

LANES = 128
MXU_DIM = 256
VMEM_LIMIT = 56 * 1024 * 1024

HY_CT = 256
HY_RC = 512
HY_PH = 4
RSQRT2 = math.sqrt(0.5)
HALO = 8
AT_RC = 512
EMB_PAD = 128
Q_TILE = 128
BAND_HALF = 64
GROUP_ORDER = (2, 1, 0)
PERM_DILATIONS = (16, 4, 1)
NAT = 2
PN_RC = 512
PAIR = 2 * HEAD_DIM

f32 = jnp.float32
bf16 = jnp.bfloat16


def _dot(a, b):
    return jnp.dot(a, b, preferred_element_type=f32)


def _const_spec(shape, index_map):
    return pl.BlockSpec(shape, index_map, pipeline_mode=pl.Buffered(1))


@functools.lru_cache(maxsize=None)
def _dft_tables(L):
    n = 2 * L
    f = np.arange(L // HY_PH, dtype=np.int64)[:, None]
    m = np.arange(L // HY_PH, dtype=np.int64)[None, :]
    fwd = []
    for p in range(HY_PH):
        ang = ((f * (HY_PH * m + p)) % n).astype(np.float64) * (2.0 * np.pi / n)
        fwd += [np.cos(ang), np.sin(ang)]
    return np.stack(fwd + [t.T for t in fwd]).astype(np.float32)


def _butterfly(A, B):
    e = (A[0] + A[2], A[0] - A[2], A[1] + A[3], A[1] - A[3])
    f = (B[0] + B[2], B[0] - B[2], B[1] + B[3], B[1] - B[3])
    return e, f


def _spectrum_cos(e, f):
    return (e[0] + e[2], e[1] - f[3], e[0] - e[2], e[1] + f[3])


def _spectrum_sin(e, f):
    return (f[0] + f[2], f[1] + e[3], f[2] - f[0], e[3] - f[1])


def _odd_bins(r):
    d, s = (r[1] - r[3]) * RSQRT2, (r[1] + r[3]) * RSQRT2
    return (r[0] + d, s + r[2]), (r[0] - d, s - r[2])


def _phase_major(a, L):
    return np.concatenate([a[p::HY_PH] for p in range(HY_PH)], axis=0)


@functools.lru_cache(maxsize=None)
def _filter_embedding(L):
    t = np.linspace(0.0, 1.0, L)[:, None]
    bands = (HY_EMB_DIM - 1) // 2
    w = 2.0 * np.pi * np.arange(L)[:, None] / L
    f = np.linspace(1e-4, bands - 1, bands)[None, :]
    z = np.concatenate([t, np.cos(f * w), -np.sin(f * w)], axis=-1)
    zp = np.zeros((L, EMB_PAD), np.float64)
    zp[:, :HY_EMB_DIM] = z
    return zp.astype(np.float32)


@functools.lru_cache(maxsize=None)
def _decay_rates():
    max_decay = math.log(HY_DECAY_TARGET) / HY_FAST_DECAY
    min_decay = math.log(HY_DECAY_TARGET) / HY_SLOW_DECAY
    return np.abs(np.linspace(min_decay, max_decay, HY_WIDTH))[None, :].astype(np.float32)


def _alibi_slope(h):
    return 2.0 ** (-8.0 * (h + 1) / HEADS_PER_GROUP)


@functools.lru_cache(maxsize=None)
def _attn_dist(n, dilation, window):
    half = window // (2 * dilation)
    assert half == BAND_HALF
    tq = min(Q_TILE, n)
    w = min(2 * Q_TILE, n)
    masked = NEG_INF / _alibi_slope(HEADS_PER_GROUP - 1)
    offs = sorted({q0 - min(max(q0 - half, 0), n - w) for q0 in range(0, n, tq)})
    assert offs == [BAND_HALF * i for i in range(len(offs))]
    out = np.zeros((len(offs), w, tq), np.float32)
    for ci, off in enumerate(offs):
        rel = np.arange(tq)[None, :] + off - np.arange(w)[:, None]
        out[ci] = np.where(np.abs(rel) <= half, -dilation * np.abs(rel), masked)
    return out


@functools.lru_cache(maxsize=None)
def _slope_eye():
    eye = np.eye(Q_TILE, dtype=np.float32)
    return np.stack([np.concatenate([_alibi_slope(2 * p) * eye, _alibi_slope(2 * p + 1) * eye], axis=0)
                     for p in range(HEADS_PER_GROUP // 2)])


@functools.lru_cache(maxsize=None)
def _residue_perm(n):
    p = np.zeros((n, n), np.float32)
    i = np.arange(n)
    p[(i % 4) * (n // 4) + i // 4, i] = 1.0
    return p


def _prenorm_kernel(x_ref, g_ref, perm_ref, o_ref, st_ref):
    L = x_ref.shape[1]
    nt = D_MODEL // LANES
    d4, d16 = PERM_DILATIONS.index(4), PERM_DILATIONS.index(16)
    n4, n16 = L // 4, L // 16
    for r0 in range(0, L, PN_RC):
        rows = slice(r0, r0 + PN_RC)
        x = x_ref[0, rows, :]
        ms = jnp.mean(x * x, axis=-1, keepdims=True)
        xn = x * lax.rsqrt(ms + NORM_EPS) * g_ref[...]
        o_ref[NAT, 0, rows, :] = xn.astype(o_ref.dtype)
        for c in range(nt):
            st_ref[c, rows, :] = xn[:, c * LANES:(c + 1) * LANES]

    def gather(r, carry):
        dst = pl.ds(pl.multiple_of(r * n4, n4), n4)
        for c in range(nt):
            o_ref[d4, 0, dst, c * LANES:(c + 1) * LANES] = st_ref[c, pl.ds(r, n4, stride=4), :].astype(o_ref.dtype)
        return carry

    lax.fori_loop(0, 4, gather, 0)
    for r4 in range(4):
        y = _dot(perm_ref[...], o_ref[d4, 0, r4 * n4:(r4 + 1) * n4, :]).astype(o_ref.dtype)
        for q in range(4):
            o_ref[d16, 0, (r4 + 4 * q) * n16:(r4 + 4 * q + 1) * n16, :] = y[q * n16:(q + 1) * n16]


def _prenorm(x, g):
    B, L, D = x.shape
    assert sorted(PERM_DILATIONS) == [1, 4, 16] and PERM_DILATIONS[NAT] == 1
    perm = jnp.asarray(_residue_perm(L // 4)).astype(bf16)
    return pl.pallas_call(
        _prenorm_kernel,
        grid=(B,),
        in_specs=[pl.BlockSpec((1, L, D), lambda b: (b, 0, 0)),
                  pl.BlockSpec((1, D), lambda b: (0, 0)),
                  _const_spec(perm.shape, lambda b: (0, 0))],
        out_specs=pl.BlockSpec((len(PERM_DILATIONS), 1, L, D), lambda b: (0, b, 0, 0)),
        out_shape=jax.ShapeDtypeStruct((len(PERM_DILATIONS), B, L, D), bf16),
        scratch_shapes=[pltpu.VMEM((D // LANES, L, LANES), f32)],
        compiler_params=pltpu.CompilerParams(dimension_semantics=("arbitrary",),
                                             vmem_limit_bytes=VMEM_LIMIT),
        name="prenorm",
    )(x, g, perm)


def _filters_kernel(z_ref, w1_ref, b1_ref, w2_ref, b2_ref, w3_ref, b3_ref, fr_ref, w4f_ref, w4b_ref,
                    t_ref, rate_ref, tab_ref, kr_ref, ki_ref, ks_ref, h3_ref):
    L = z_ref.shape[0]
    M = L // HY_PH
    hi = lax.Precision.HIGHEST

    @pl.when((pl.program_id(0) == 0) & (pl.program_id(1) == 0))
    def _():
        fr = fr_ref[...]
        h = jnp.sin(fr * (jnp.dot(z_ref[...], w1_ref[...], precision=hi, preferred_element_type=f32)
                          + b1_ref[...]))
        h = jnp.sin(fr * (jnp.dot(h, w2_ref[...], precision=hi, preferred_element_type=f32) + b2_ref[...]))
        h = jnp.sin(fr * (jnp.dot(h, w3_ref[...], precision=hi, preferred_element_type=f32) + b3_ref[...]))
        h3_ref[...] = h

    h3 = h3_ref[...]
    decay = jnp.exp(-t_ref[...] * rate_ref[...]) + HY_MOD_SHIFT
    hf = jnp.dot(h3, w4f_ref[...], precision=hi, preferred_element_type=f32) * decay
    hb = jnp.dot(h3, w4b_ref[...], precision=hi, preferred_element_type=f32) * decay
    hb0 = hb[0:1, :]
    hs = hf + hb
    hd = hb - hf
    n = 2 * L
    row = lax.broadcasted_iota(jnp.int32, (M, HY_CT), 0)
    sgn = jnp.where((row & 1) == 1, -1.0, 1.0).astype(f32)

    def transform(x):
        xp = [x[p * M:(p + 1) * M] for p in range(HY_PH)]
        A = [_dot(tab_ref[2 * p], xp[p].astype(bf16)) for p in range(HY_PH)]
        B = [_dot(tab_ref[2 * p + 1], xp[p].astype(bf16)) for p in range(HY_PH)]
        r = [jnp.sum(xp[p] * sgn, axis=0, keepdims=True) for p in range(HY_PH)]
        return _butterfly(A, B), _odd_bins(r)

    (es, fs), odd_s = transform(hs)
    (ed, fd), odd_d = transform(hd)
    kr = _spectrum_cos(es, fs)
    ki = _spectrum_sin(ed, fd)
    two = 2.0 / n
    edge = jnp.where(row == 0, 1.0 / n, two).astype(f32)
    once = jnp.where(row == 0, 0.0, two).astype(f32)
    for cls, wgt in enumerate((edge, two, edge, once)):
        kr_ref[0, cls] = (kr[cls] - hb0) * wgt
        ki_ref[0, cls] = ki[cls] * wgt
    for j in range(2):
        ks_ref[0, 2 * j:2 * j + 1, :] = (odd_s[j][0] - hb0) * two
        ks_ref[0, 2 * j + 1:2 * j + 2, :] = odd_d[j][1] * two


def _filters(L, tab, w1, b1, w2, b2, w3, b3, w4, freq):
    M = L // HY_PH
    z = jnp.asarray(_phase_major(_filter_embedding(L), L))
    t = jnp.asarray(_phase_major(np.linspace(0.0, 1.0, L)[:, None].astype(np.float32), L))
    rate = jnp.asarray(_decay_rates())
    w1p = jnp.zeros((EMB_PAD, HY_FILTER_HIDDEN), f32).at[:HY_EMB_DIM].set(w1.astype(f32))
    nct = HY_WIDTH // HY_CT
    row = lambda a: a.astype(f32).reshape(1, -1)
    full = lambda shape: pl.BlockSpec(shape, lambda o, j: (0,) * len(shape))
    H = HY_FILTER_HIDDEN
    kspec = pl.BlockSpec((1, HY_PH, M, HY_CT), lambda o, j: (o, 0, 0, j))
    kshape = jax.ShapeDtypeStruct((HY_ORDER, HY_PH, M, HY_WIDTH), f32)
    return pl.pallas_call(
        _filters_kernel,
        grid=(HY_ORDER, nct),
        in_specs=[full((L, EMB_PAD)), full((EMB_PAD, H)), full((1, H)), full((H, H)), full((1, H)),
                  full((H, H)), full((1, H)), full((1, H)),
                  pl.BlockSpec((H, HY_CT), lambda o, j: (0, 2 * nct * o + j)),
                  pl.BlockSpec((H, HY_CT), lambda o, j: (0, 2 * nct * o + nct + j)),
                  full((L, 1)),
                  pl.BlockSpec((1, HY_CT), lambda o, j: (0, j)),
                  _const_spec(tab.shape, lambda o, j: (0, 0, 0))],
        out_specs=[kspec, kspec, pl.BlockSpec((1, 4, HY_CT), lambda o, j: (o, 0, j))],
        out_shape=[kshape, kshape, jax.ShapeDtypeStruct((HY_ORDER, 4, HY_WIDTH), f32)],
        scratch_shapes=[pltpu.VMEM((L, H), f32)],
        compiler_params=pltpu.CompilerParams(dimension_semantics=("arbitrary", "arbitrary"),
                                             vmem_limit_bytes=VMEM_LIMIT),
        name="hyena_filters",
    )(z, w1p, row(b1), w2.astype(f32), row(b2), w3.astype(f32), row(b3), row(freq),
      w4.astype(f32), w4.astype(f32), t, rate, tab)


def _hyena_kernel(xn_ref, wv_ref, wx1_ref, wx2_ref, wg_ref, bv_ref, bx1_ref, bx2_ref, bg_ref,
                  cwv_ref, cwx1_ref, cwx2_ref, cbv_ref, cbx1_ref, cbx2_ref,
                  tab_ref, kr_ref, ki_ref, ks_ref, skip_ref, o_ref,
                  z_ref, u_ref, ub_ref, pq_ref, x_ref):
    L = xn_ref.shape[1]
    M = L // HY_PH
    nlt = HY_CT // LANES
    row_chunks = [slice(r, r + HY_RC) for r in range(0, L, HY_RC)]
    chunks = [slice(r, r + HY_RC) for r in range(0, M, HY_RC)]
    row = lax.broadcasted_iota(jnp.int32, (HY_RC, HY_CT), 0)
    sgn = jnp.where((row & 1) == 1, -1.0, 1.0).astype(f32)
    for lt in range(nlt):
        z_ref[lt, 0:HALO] = jnp.zeros((HALO, LANES), f32)
        z_ref[lt, L + HALO:L + 2 * HALO] = jnp.zeros((HALO, LANES), f32)

    def stage_natural(val, c):
        for lt in range(nlt):
            z_ref[lt, HALO + c.start:HALO + c.stop, :] = val[:, lt * LANES:(lt + 1) * LANES]

    def phase_rows(p, c, shift=0):
        src = pl.ds(HALO + p + shift + HY_PH * c.start, HY_RC, stride=HY_PH)
        return jnp.concatenate([z_ref[lt, src, :] for lt in range(nlt)], axis=1)

    def proj_conv(dst_ref, w_ref, b_ref, cw_ref, cb_ref):
        w = w_ref[...].astype(bf16)
        for c in row_chunks:
            stage_natural(_dot(xn_ref[0, c, :], w) + b_ref[...], c)
        for p in range(HY_PH):
            for c in chunks:
                dst_ref[p, c] = (cb_ref[...] + phase_rows(p, c, -1) * cw_ref[0:1, :]
                                 + phase_rows(p, c) * cw_ref[1:2, :] + phase_rows(p, c, 1) * cw_ref[2:3, :])

    def long_conv(o):
        r = [jnp.zeros((1, HY_CT), f32) for _ in range(HY_PH)]
        for p in range(HY_PH):
            for c in chunks:
                u = u_ref[p, c]
                ub_ref[p, c] = u.astype(bf16)
                r[p] = r[p] + jnp.sum(u * sgn, axis=0, keepdims=True)
        for c in chunks:
            A = [_dot(tab_ref[2 * p, c, :], ub_ref[p]) for p in range(HY_PH)]
            B = [_dot(tab_ref[2 * p + 1, c, :], ub_ref[p]) for p in range(HY_PH)]
            e, f = _butterfly(A, B)
            a, b = _spectrum_cos(e, f), _spectrum_sin(e, f)
            P, Q = [], []
            for cls in range(HY_PH):
                kr, ki = kr_ref[o, cls, c, :], ki_ref[o, cls, c, :]
                P.append(a[cls] * kr + b[cls] * ki)
                Q.append(b[cls] * kr - a[cls] * ki)
            g = (P[0] + P[2], P[0] - P[2], P[1] + P[3], P[3] - P[1])
            h = (Q[0] - Q[2], Q[0] + Q[2], Q[1] - Q[3], Q[1] + Q[3])
            X = (g[0] + g[2], g[1] + h[3], g[0] - g[2], g[1] - h[3])
            Y = (h[0] + h[2], h[1] + g[3], h[0] - h[2], h[1] - g[3])
            for p in range(HY_PH):
                pq_ref[2 * p, c] = X[p].astype(bf16)
                pq_ref[2 * p + 1, c] = Y[p].astype(bf16)
        pq_odd = []
        for j, (a_o, b_o) in enumerate(_odd_bins(r)):
            kr, ki = ks_ref[o, 2 * j:2 * j + 1, :], ks_ref[o, 2 * j + 1:2 * j + 2, :]
            pq_odd.append((a_o * kr + b_o * ki, b_o * kr - a_o * ki))
        (p1, q1), (p3, q3) = pq_odd
        odd = (p1 + p3, (p1 + q1 - p3 + q3) * RSQRT2, q1 - q3, (q1 - p1 + p3 + q3) * RSQRT2)
        skip = skip_ref[o:o + 1, :]
        nt = 2 * HY_PH
        for p in range(HY_PH):
            for c in chunks:
                y = _dot(tab_ref[nt + 2 * p, c, :], pq_ref[2 * p]) + _dot(tab_ref[nt + 2 * p + 1, c, :], pq_ref[2 * p + 1])
                u_ref[p, c] = x_ref[p, c] * (y + sgn * odd[p] + u_ref[p, c] * skip)

    proj_conv(u_ref, wv_ref, bv_ref, cwv_ref, cbv_ref)
    proj_conv(x_ref, wx1_ref, bx1_ref, cwx1_ref, cbx1_ref)
    long_conv(0)
    proj_conv(x_ref, wx2_ref, bx2_ref, cwx2_ref, cbx2_ref)
    long_conv(1)
    for p in range(HY_PH):
        for c in chunks:
            dst = pl.ds(HALO + p + HY_PH * c.start, HY_RC, stride=HY_PH)
            y = u_ref[p, c]
            for lt in range(nlt):
                z_ref[lt, dst, :] = y[:, lt * LANES:(lt + 1) * LANES]
    wg = wg_ref[...].astype(bf16)
    for c in row_chunks:
        y = jnp.concatenate([z_ref[lt, HALO + c.start:HALO + c.stop, :] for lt in range(nlt)], axis=1)
        g = _dot(xn_ref[0, c, :], wg) + bg_ref[...]
        o_ref[0, c, :] = (y * (g * jax.nn.sigmoid(g))).astype(o_ref.dtype)


def _hyena(xs, w_in, b_in, conv_w, conv_b, tab, kr, ki, ks, skip):
    _, B, L, _ = xs.shape
    M = L // HY_PH
    nct = HY_WIDTH // HY_CT
    hg = O_HGATE // HY_CT

    def col(k):
        return lambda j, b: (0, k * nct + j)

    wspec = lambda k: _const_spec((D_MODEL, HY_CT), col(k))
    bspec = lambda k: pl.BlockSpec((1, HY_CT), col(k))
    cwspec = lambda k: pl.BlockSpec((HY_SHORT_CONV, HY_CT), col(k))
    kspec = _const_spec((HY_ORDER, HY_PH, M, HY_CT), lambda j, b: (0, 0, 0, j))
    return pl.pallas_call(
        _hyena_kernel,
        grid=(nct, B),
        in_specs=[pl.BlockSpec((None, 1, L, D_MODEL), lambda j, b: (NAT, b, 0, 0)),
                  wspec(0), wspec(1), wspec(2), _const_spec((D_MODEL, HY_CT), lambda j, b: (0, hg + j)),
                  bspec(0), bspec(1), bspec(2), pl.BlockSpec((1, HY_CT), lambda j, b: (0, hg + j)),
                  cwspec(0), cwspec(1), cwspec(2), bspec(0), bspec(1), bspec(2),
                  _const_spec(tab.shape, lambda j, b: (0, 0, 0)),
                  kspec, kspec,
                  pl.BlockSpec((HY_ORDER, 4, HY_CT), lambda j, b: (0, 0, j)),
                  pl.BlockSpec((HY_ORDER, HY_CT), lambda j, b: (0, j))],
        out_specs=pl.BlockSpec((1, L, HY_CT), lambda j, b: (b, 0, j)),
        out_shape=jax.ShapeDtypeStruct((B, L, HY_WIDTH), bf16),
        scratch_shapes=[pltpu.VMEM((HY_CT // LANES, L + 2 * HALO, LANES), f32), pltpu.VMEM((HY_PH, M, HY_CT), f32),
                        pltpu.VMEM((HY_PH, M, HY_CT), bf16), pltpu.VMEM((2 * HY_PH, M, HY_CT), bf16),
                        pltpu.VMEM((HY_PH, M, HY_CT), f32)],
        compiler_params=pltpu.CompilerParams(dimension_semantics=("arbitrary", "arbitrary"),
                                             vmem_limit_bytes=VMEM_LIMIT),
        name="hyena_mixer",
    )(xs, w_in, w_in, w_in, w_in, b_in, b_in, b_in, b_in,
      conv_w, conv_w, conv_w, conv_b, conv_b, conv_b, tab, kr, ki, ks, skip)


def _attn_kernel(xs_ref, wq_ref, wk_ref, wv_ref, bq_ref, bk_ref, bv_ref, wag_ref, bag_ref,
                 gq_ref, gk_ref, hsum_ref, sl_ref, d0_ref, d1_ref, d2_ref, o_ref,
                 qs_ref, ks_ref, vs_ref, acc_ref, mx_ref, den_ref):
    L = xs_ref.shape[2]
    gw = HEADS_PER_GROUP * HEAD_DIM
    npair = HEADS_PER_GROUP // 2
    dist_refs = (d0_ref, d1_ref, d2_ref)
    tq = Q_TILE
    first = lax.broadcasted_iota(jnp.int32, (tq, PAIR), 1) < HEAD_DIM
    nt_dims = (((1,), (1,)), ((), ()))

    def normed(x, w, b_ref, g_ref):
        z = _dot(x, w) + b_ref[...]
        z2 = (z * z).astype(bf16)
        ssq = jnp.concatenate([_dot(z2[:, c:c + MXU_DIM], hsum_ref[...]) for c in range(0, gw, MXU_DIM)], axis=1)
        return z * lax.rsqrt(ssq * (1.0 / HEAD_DIM) + NORM_EPS) * g_ref[...]

    def group(gi):
        _, d = DILATED_GROUPS[GROUP_ORDER[gi]]
        n = L // d
        w = min(2 * tq, n)
        per_class = n // tq
        dist_ref = dist_refs[GROUP_ORDER[gi]]

        wq, wk, wv = (r[...].astype(bf16) for r in (wq_ref, wk_ref, wv_ref))
        for r0 in range(0, L, AT_RC):
            rows = slice(r0, r0 + AT_RC)
            x = xs_ref[0, 0, rows, :]
            qs_ref[rows] = (normed(x, wq, bq_ref, gq_ref) * (HEAD_DIM ** -0.5)).astype(bf16)
            ks_ref[rows] = normed(x, wk, bk_ref, gk_ref).astype(bf16)
            vs_ref[rows] = (_dot(x, wv) + bv_ref[...]).astype(bf16)

        def tile(idx, carry):
            r = idx // per_class
            t = idx % per_class
            q0 = pl.multiple_of(idx * tq, tq)
            koff = jnp.clip(t * tq - BAND_HALF, 0, n - w)
            dist_t = dist_ref[(t * tq - koff) // BAND_HALF]
            k0 = pl.multiple_of(r * n + koff, BAND_HALF)
            nat = pl.ds(t * tq * d + r, tq, stride=d) if d > 1 else pl.ds(q0, tq)
            for p in range(npair):
                pc = slice(p * PAIR, (p + 1) * PAIR)
                q = qs_ref[pl.ds(q0, tq), pc]
                zero = jnp.zeros_like(q)
                qq = jnp.concatenate([jnp.where(first, q, zero), jnp.where(first, zero, q)], axis=0)
                lhs = jnp.concatenate([qq, sl_ref[p]], axis=1)
                rhs_t = jnp.concatenate([ks_ref[pl.ds(k0, w), pc], dist_t], axis=1)
                s = lax.dot_general(lhs, rhs_t, nt_dims, preferred_element_type=f32)
                m = jnp.max(s, axis=-1, keepdims=True)
                pr = jnp.exp(s - m).astype(bf16)
                rhs = jnp.concatenate([vs_ref[pl.ds(k0, w), pc], jnp.ones((w, PAIR), bf16)], axis=1)
                ov = _dot(pr, rhs)
                num = jnp.where(first, ov[0:tq, 0:PAIR], ov[tq:2 * tq, 0:PAIR])
                den = jnp.where(first, ov[0:tq, PAIR:2 * PAIR], ov[tq:2 * tq, PAIR:2 * PAIR])
                mb = jnp.where(first, m[0:tq], m[tq:2 * tq])
                if gi == 0:
                    acc_ref[p, nat, :] = num
                    den_ref[p, nat, :] = den
                    mx_ref[p, nat, :] = mb
                else:
                    m_old = mx_ref[p, nat, :]
                    m_new = jnp.maximum(m_old, mb)
                    a = jnp.exp(m_old - m_new)
                    b = jnp.exp(mb - m_new)
                    acc_ref[p, nat, :] = acc_ref[p, nat, :] * a + num * b
                    den_ref[p, nat, :] = den_ref[p, nat, :] * a + den * b
                    mx_ref[p, nat, :] = m_new
            return carry

        lax.fori_loop(0, L // tq, tile, 0, unroll=8)

    for gi in range(N_GROUPS):
        pl.when(pl.program_id(1) == gi)(functools.partial(group, gi))

    @pl.when(pl.program_id(1) == N_GROUPS - 1)
    def _():
        wag = wag_ref[...].astype(bf16)
        for r0 in range(0, L, AT_RC):
            rows = slice(r0, r0 + AT_RC)
            ag = _dot(xs_ref[0, 0, rows, :], wag) + bag_ref[...]
            o = jnp.concatenate([acc_ref[p, rows, :] / den_ref[p, rows, :] for p in range(npair)], axis=1)
            o_ref[0, rows, :] = (o * (ag * jax.nn.sigmoid(ag))).astype(o_ref.dtype)


def _attention(xs, w_in, b_in, gq, gk, hsum):
    _, B, L, _ = xs.shape
    assert PERM_DILATIONS == tuple(DILATED_GROUPS[g][1] for g in GROUP_ORDER) and PERM_DILATIONS[-1] == 1
    assert GROUP_ORDER == tuple(N_GROUPS - 1 - i for i in range(N_GROUPS))
    assert L % AT_RC == 0 and all(L % (Q_TILE * d) == 0 for _, d in DILATED_GROUPS)
    gw = HEADS_PER_GROUP * HEAD_DIM
    npair = HEADS_PER_GROUP // 2
    dists = [jnp.asarray(_attn_dist(L // d, d, window)).astype(bf16) for window, d in DILATED_GROUPS]
    sl = jnp.asarray(_slope_eye()).astype(bf16)
    agb = O_AGATE // gw

    def col(k):
        return lambda b, i: (0, (O_QKV + k * AT_QKV) // gw + (N_GROUPS - 1 - i))

    wspec = lambda k: pl.BlockSpec((D_MODEL, gw), col(k))
    bspec = lambda k: pl.BlockSpec((1, gw), col(k))
    vec = pl.BlockSpec((1, gw), lambda b, i: (0, 0))
    acc = pltpu.VMEM((npair, L, PAIR), f32)
    return pl.pallas_call(
        _attn_kernel,
        grid=(B, N_GROUPS),
        in_specs=[pl.BlockSpec((1, 1, L, D_MODEL), lambda b, i: (i, b, 0, 0)),
                  wspec(0), wspec(1), wspec(2), bspec(0), bspec(1), bspec(2),
                  _const_spec((D_MODEL, gw), lambda b, i: (0, agb)), pl.BlockSpec((1, gw), lambda b, i: (0, agb)),
                  vec, vec, _const_spec(hsum.shape, lambda b, i: (0, 0)), _const_spec(sl.shape, lambda b, i: (0, 0, 0))]
                 + [_const_spec(t.shape, lambda b, i: (0, 0, 0)) for t in dists],
        out_specs=pl.BlockSpec((1, L, gw), lambda b, i: (b, 0, 0)),
        out_shape=jax.ShapeDtypeStruct((B, L, gw), bf16),
        scratch_shapes=[pltpu.VMEM((L, gw), bf16), pltpu.VMEM((L, gw), bf16), pltpu.VMEM((L, gw), bf16),
                        acc, acc, acc],
        compiler_params=pltpu.CompilerParams(dimension_semantics=("arbitrary", "arbitrary"),
                                             vmem_limit_bytes=VMEM_LIMIT),
        name="dilated_attention",
    )(xs, w_in, w_in, w_in, b_in, b_in, b_in, w_in, b_in, gq, gk, hsum, sl, *dists)


def _final_kernel(x_ref, xn_ref, gh_ref, ga_ref, wg_ref, bg_ref, why_ref, wat_ref, wout_ref, out_ref,
                  wg_b, why_b, wat_b, wout_b):
    @pl.when(pl.program_id(0) == 0)
    def _():
        for src, dst in ((wg_ref, wg_b), (why_ref, why_b), (wat_ref, wat_b), (wout_ref, wout_b)):
            dst[...] = src[...].astype(bf16)

    gates = _dot(xn_ref[...], wg_b[...]) + bg_ref[...]
    u_h = _dot(gh_ref[...], why_b[...])
    u_a = _dot(ga_ref[...], wat_b[...])
    merged = jax.nn.sigmoid(gates[:, 0:D_MODEL]) * u_h + jax.nn.sigmoid(gates[:, D_MODEL:]) * u_a
    out_ref[...] = x_ref[...] + _dot(merged.astype(bf16), wout_b[...])


def _final(x2, xs2, gh2, ga2, w_in, b_in, why, wat, wout):
    rows = x2.shape[0]
    tm = 512
    mgw = 2 * D_MODEL
    rspec = lambda c: pl.BlockSpec((tm, c), lambda i: (i, 0))
    cspec = lambda a: _const_spec(a.shape, lambda i: (0, 0))
    return pl.pallas_call(
        _final_kernel,
        grid=(rows // tm,),
        in_specs=[rspec(D_MODEL), pl.BlockSpec((None, tm, D_MODEL), lambda i: (NAT, i, 0)), rspec(HY_WIDTH), rspec(AT_WIDTH),
                  _const_spec((D_MODEL, mgw), lambda i: (0, O_MG // mgw)),
                  pl.BlockSpec((1, mgw), lambda i: (0, O_MG // mgw)), cspec(why), cspec(wat), cspec(wout)],
        out_specs=rspec(D_MODEL),
        out_shape=jax.ShapeDtypeStruct((rows, D_MODEL), f32),
        scratch_shapes=[pltpu.VMEM((D_MODEL, mgw), bf16), pltpu.VMEM(why.shape, bf16), pltpu.VMEM(wat.shape, bf16),
                        pltpu.VMEM(wout.shape, bf16)],
        compiler_params=pltpu.CompilerParams(dimension_semantics=("arbitrary",),
                                             vmem_limit_bytes=VMEM_LIMIT),
        name="merge_output",
    )(x2, xs2, gh2, ga2, w_in, b_in, why, wat, wout)


def _layer(x, norm_g, w_in, b_in, conv_w, conv_b, hf_w1, hf_b1, hf_w2, hf_b2, hf_w3, hf_b3, hf_w4,
           hf_freq, hy_skip, q_norm_g, k_norm_g, w_hy_out, w_at_out, w_out):
    B, L, D = x.shape
    x2 = x.reshape(B * L, D)
    tab = jnp.asarray(_dft_tables(L)).astype(bf16)
    w_in = w_in.astype(f32)
    b_in2 = b_in.astype(f32).reshape(1, IN_COLS)

    xs = _prenorm(x, norm_g.astype(f32).reshape(1, D))

    kr, ki, ks = _filters(L, tab, hf_w1, hf_b1, hf_w2, hf_b2, hf_w3, hf_b3, hf_w4, hf_freq)
    gh = _hyena(xs, w_in, b_in2, conv_w.astype(f32), conv_b.astype(f32).reshape(1, -1),
                tab, kr, ki, ks, hy_skip.astype(f32))

    gq = jnp.tile(q_norm_g.astype(f32), HEADS_PER_GROUP).reshape(1, -1)
    gk = jnp.tile(k_norm_g.astype(f32), HEADS_PER_GROUP).reshape(1, -1)
    head = np.arange(MXU_DIM) // HEAD_DIM
    hsum = jnp.asarray((head[:, None] == head[None, :]).astype(np.float32)).astype(bf16)
    ga = _attention(xs, w_in, b_in2, gq, gk, hsum)

    out = _final(x2, xs.reshape(len(PERM_DILATIONS), B * L, D), gh.reshape(B * L, HY_WIDTH), ga.reshape(B * L, AT_WIDTH),
                 w_in, b_in2, w_hy_out.astype(f32), w_at_out.astype(f32), w_out.astype(f32))
    return out.reshape(B, L, D)


def kernel(x, norm_g, w_in, b_in, conv_w, conv_b, hf_w1, hf_b1, hf_w2, hf_b2, hf_w3, hf_b3, hf_w4,
           hf_freq, hy_skip, q_norm_g, k_norm_g, w_hy_out, w_at_out, w_out):
    depth = norm_g.shape[0]
    for i in range(depth):
        x = _layer(x, norm_g[i], w_in[i], b_in[i], conv_w[i], conv_b[i], hf_w1[i], hf_b1[i], hf_w2[i],
                   hf_b2[i], hf_w3[i], hf_b3[i], hf_w4[i], hf_freq[i], hy_skip[i], q_norm_g[i],
                   k_norm_g[i], w_hy_out[i], w_at_out[i], w_out[i])
    return x
```

```python
import functools
import math

import jax
import jax.numpy as jnp
import numpy as np
from jax import lax
from jax.experimental import pallas as pl
from jax.experimental.pallas import tpu as pltpu

D_MODEL = 1024
HY_WIDTH = 768
HY_ORDER = 2
HY_SHORT_CONV = 3
HY_EMB_DIM = 33
HY_FILTER_HIDDEN = 64
HY_FAST_DECAY = 0.3
HY_SLOW_DECAY = 1.5
HY_DECAY_TARGET = 1e-2
HY_MOD_SHIFT = 0.0
HEAD_DIM = 64
HEADS_PER_GROUP = 8
DILATED_GROUPS = ((128, 1), (512, 4), (2048, 16))
N_GROUPS = 3
AT_QKV = N_GROUPS * HEADS_PER_GROUP * HEAD_DIM
AT_WIDTH = HEADS_PER_GROUP * HEAD_DIM
NORM_EPS = 1e-6
NEG_INF = -1e30

O_HGATE = 3 * HY_WIDTH
O_QKV = O_HGATE + HY_WIDTH
O_AGATE = O_QKV + 3 * AT_QKV
O_MG = O_AGATE + AT_WIDTH
IN_COLS = O_MG + 2 * D_MODEL

LANES = 128
MXU_DIM = 256
VMEM_LIMIT = 56 * 1024 * 1024

HY_CT = 256
HY_RC = 512
HY_FC = 256
HY_PH = 4
RSQRT2 = math.sqrt(0.5)
HALO = 8
AT_RC = 512
EMB_PAD = 128
Q_TILE = 128
BAND_HALF = 64
GROUP_ORDER = (2, 1, 0)
PERM_DILATIONS = (16, 4, 1)
NAT = 2
PN_RC = 512
PAIR = 2 * HEAD_DIM

f32 = jnp.float32
bf16 = jnp.bfloat16


def _dot(a, b):
    return jnp.dot(a, b, preferred_element_type=f32)


def _const_spec(shape, index_map):
    return pl.BlockSpec(shape, index_map, pipeline_mode=pl.Buffered(1))


@functools.lru_cache(maxsize=None)
def _dft_tables(L):
    n = 2 * L
    f = np.arange(L // HY_PH, dtype=np.int64)[:, None]
    m = np.arange(L // HY_PH, dtype=np.int64)[None, :]
    fwd = []
    for p in range(HY_PH):
        ang = ((f * (HY_PH * m + p)) % n).astype(np.float64) * (2.0 * np.pi / n)
        fwd += [np.cos(ang), np.sin(ang)]
    return np.stack(fwd + [t.T for t in fwd]).astype(np.float32)


def _butterfly(A, B):
    e = (A[0] + A[2], A[0] - A[2], A[1] + A[3], A[1] - A[3])
    f = (B[0] + B[2], B[0] - B[2], B[1] + B[3], B[1] - B[3])
    return e, f


def _spectrum_cos(e, f):
    return (e[0] + e[2], e[1] - f[3], e[0] - e[2], e[1] + f[3])


def _spectrum_sin(e, f):
    return (f[0] + f[2], f[1] + e[3], f[2] - f[0], e[3] - f[1])


def _odd_bins(r):
    d, s = (r[1] - r[3]) * RSQRT2, (r[1] + r[3]) * RSQRT2
    return (r[0] + d, s + r[2]), (r[0] - d, s - r[2])


def _phase_major(a, L):
    return np.concatenate([a[p::HY_PH] for p in range(HY_PH)], axis=0)


@functools.lru_cache(maxsize=None)
def _filter_embedding(L):
    t = np.linspace(0.0, 1.0, L)[:, None]
    bands = (HY_EMB_DIM - 1) // 2
    w = 2.0 * np.pi * np.arange(L)[:, None] / L
    f = np.linspace(1e-4, bands - 1, bands)[None, :]
    z = np.concatenate([t, np.cos(f * w), -np.sin(f * w)], axis=-1)
    zp = np.zeros((L, EMB_PAD), np.float64)
    zp[:, :HY_EMB_DIM] = z
    return zp.astype(np.float32)


@functools.lru_cache(maxsize=None)
def _decay_rates():
    max_decay = math.log(HY_DECAY_TARGET) / HY_FAST_DECAY
    min_decay = math.log(HY_DECAY_TARGET) / HY_SLOW_DECAY
    return np.abs(np.linspace(min_decay, max_decay, HY_WIDTH))[None, :].astype(np.float32)


def _alibi_slope(h):
    return 2.0 ** (-8.0 * (h + 1) / HEADS_PER_GROUP)


@functools.lru_cache(maxsize=None)
def _attn_dist(n, dilation, window):
    half = window // (2 * dilation)
    assert half == BAND_HALF
    tq = min(Q_TILE, n)
    w = min(2 * Q_TILE, n)
    masked = NEG_INF / _alibi_slope(HEADS_PER_GROUP - 1)
    offs = sorted({q0 - min(max(q0 - half, 0), n - w) for q0 in range(0, n, tq)})
    assert offs == [BAND_HALF * i for i in range(len(offs))]
    out = np.zeros((len(offs), w, tq), np.float32)
    for ci, off in enumerate(offs):
        rel = np.arange(tq)[None, :] + off - np.arange(w)[:, None]
        out[ci] = np.where(np.abs(rel) <= half, -dilation * np.abs(rel), masked)
    return out


@functools.lru_cache(maxsize=None)
def _slope_eye():
    eye = np.eye(Q_TILE, dtype=np.float32)
    return np.stack([np.concatenate([_alibi_slope(2 * p) * eye, _alibi_slope(2 * p + 1) * eye], axis=0)
                     for p in range(HEADS_PER_GROUP // 2)])


@functools.lru_cache(maxsize=None)
def _residue_perm(n):
    p = np.zeros((n, n), np.float32)
    i = np.arange(n)
    p[(i % 4) * (n // 4) + i // 4, i] = 1.0
    return p


def _prenorm_kernel(x_ref, g_ref, perm_ref, o_ref, st_ref):
    L = x_ref.shape[1]
    nt = D_MODEL // LANES
    d4, d16 = PERM_DILATIONS.index(4), PERM_DILATIONS.index(16)
    n4, n16 = L // 4, L // 16
    for r0 in range(0, L, PN_RC):
        rows = slice(r0, r0 + PN_RC)
        x = x_ref[0, rows, :]
        ms = jnp.mean(x * x, axis=-1, keepdims=True)
        xn = x * lax.rsqrt(ms + NORM_EPS) * g_ref[...]
        o_ref[NAT, 0, rows, :] = xn.astype(o_ref.dtype)
        for c in range(nt):
            st_ref[c, rows, :] = xn[:, c * LANES:(c + 1) * LANES]

    def gather(r, carry):
        dst = pl.ds(pl.multiple_of(r * n4, n4), n4)
        for c in range(nt):
            o_ref[d4, 0, dst, c * LANES:(c + 1) * LANES] = st_ref[c, pl.ds(r, n4, stride=4), :].astype(o_ref.dtype)
        return carry

    lax.fori_loop(0, 4, gather, 0)
    for r4 in range(4):
        y = _dot(perm_ref[...], o_ref[d4, 0, r4 * n4:(r4 + 1) * n4, :]).astype(o_ref.dtype)
        for q in range(4):
            o_ref[d16, 0, (r4 + 4 * q) * n16:(r4 + 4 * q + 1) * n16, :] = y[q * n16:(q + 1) * n16]


def _prenorm(x, g):
    B, L, D = x.shape
    assert sorted(PERM_DILATIONS) == [1, 4, 16] and PERM_DILATIONS[NAT] == 1
    perm = jnp.asarray(_residue_perm(L // 4)).astype(bf16)
    return pl.pallas_call(
        _prenorm_kernel,
        grid=(B,),
        in_specs=[pl.BlockSpec((1, L, D), lambda b: (b, 0, 0)),
                  pl.BlockSpec((1, D), lambda b: (0, 0)),
                  _const_spec(perm.shape, lambda b: (0, 0))],
        out_specs=pl.BlockSpec((len(PERM_DILATIONS), 1, L, D), lambda b: (0, b, 0, 0)),
        out_shape=jax.ShapeDtypeStruct((len(PERM_DILATIONS), B, L, D), bf16),
        scratch_shapes=[pltpu.VMEM((D // LANES, L, LANES), f32)],
        compiler_params=pltpu.CompilerParams(dimension_semantics=("arbitrary",),
                                             vmem_limit_bytes=VMEM_LIMIT),
        name="prenorm",
    )(x, g, perm)


def _filters_kernel(z_ref, w1_ref, b1_ref, w2_ref, b2_ref, w3_ref, b3_ref, fr_ref, w4f_ref, w4b_ref,
                    t_ref, rate_ref, tab_ref, kr_ref, ki_ref, ks_ref, h3_ref):
    L = z_ref.shape[0]
    M = L // HY_PH
    hi = lax.Precision.HIGHEST

    @pl.when((pl.program_id(0) == 0) & (pl.program_id(1) == 0))
    def _():
        fr = fr_ref[...]
        h = jnp.sin(fr * (jnp.dot(z_ref[...], w1_ref[...], precision=hi, preferred_element_type=f32)
                          + b1_ref[...]))
        h = jnp.sin(fr * (jnp.dot(h, w2_ref[...], precision=hi, preferred_element_type=f32) + b2_ref[...]))
        h = jnp.sin(fr * (jnp.dot(h, w3_ref[...], precision=hi, preferred_element_type=f32) + b3_ref[...]))
        h3_ref[...] = h

    h3 = h3_ref[...]
    decay = jnp.exp(-t_ref[...] * rate_ref[...]) + HY_MOD_SHIFT
    hf = jnp.dot(h3, w4f_ref[...], precision=hi, preferred_element_type=f32) * decay
    hb = jnp.dot(h3, w4b_ref[...], precision=hi, preferred_element_type=f32) * decay
    hb0 = hb[0:1, :]
    hs = hf + hb
    hd = hb - hf
    n = 2 * L
    row = lax.broadcasted_iota(jnp.int32, (M, HY_CT), 0)
    sgn = jnp.where((row & 1) == 1, -1.0, 1.0).astype(f32)

    def transform(x):
        xp = [x[p * M:(p + 1) * M] for p in range(HY_PH)]
        A = [_dot(tab_ref[2 * p], xp[p].astype(bf16)) for p in range(HY_PH)]
        B = [_dot(tab_ref[2 * p + 1], xp[p].astype(bf16)) for p in range(HY_PH)]
        r = [jnp.sum(xp[p] * sgn, axis=0, keepdims=True) for p in range(HY_PH)]
        return _butterfly(A, B), _odd_bins(r)

    (es, fs), odd_s = transform(hs)
    (ed, fd), odd_d = transform(hd)
    kr = _spectrum_cos(es, fs)
    ki = _spectrum_sin(ed, fd)
    two = 2.0 / n
    edge = jnp.where(row == 0, 1.0 / n, two).astype(f32)
    once = jnp.where(row == 0, 0.0, two).astype(f32)
    for cls, wgt in enumerate((edge, two, edge, once)):
        kr_ref[0, cls] = (kr[cls] - hb0) * wgt
        ki_ref[0, cls] = ki[cls] * wgt
    for j in range(2):
        ks_ref[0, 2 * j:2 * j + 1, :] = (odd_s[j][0] - hb0) * two
        ks_ref[0, 2 * j + 1:2 * j + 2, :] = odd_d[j][1] * two


def _filters(L, tab, w1, b1, w2, b2, w3, b3, w4, freq):
    M = L // HY_PH
    z = jnp.asarray(_phase_major(_filter_embedding(L), L))
    t = jnp.asarray(_phase_major(np.linspace(0.0, 1.0, L)[:, None].astype(np.float32), L))
    rate = jnp.asarray(_decay_rates())
    w1p = jnp.zeros((EMB_PAD, HY_FILTER_HIDDEN), f32).at[:HY_EMB_DIM].set(w1.astype(f32))
    nct = HY_WIDTH // HY_CT
    row = lambda a: a.astype(f32).reshape(1, -1)
    full = lambda shape: pl.BlockSpec(shape, lambda o, j: (0,) * len(shape))
    H = HY_FILTER_HIDDEN
    kspec = pl.BlockSpec((1, HY_PH, M, HY_CT), lambda o, j: (o, 0, 0, j))
    kshape = jax.ShapeDtypeStruct((HY_ORDER, HY_PH, M, HY_WIDTH), f32)
    return pl.pallas_call(
        _filters_kernel,
        grid=(HY_ORDER, nct),
        in_specs=[full((L, EMB_PAD)), full((EMB_PAD, H)), full((1, H)), full((H, H)), full((1, H)),
                  full((H, H)), full((1, H)), full((1, H)),
                  pl.BlockSpec((H, HY_CT), lambda o, j: (0, 2 * nct * o + j)),
                  pl.BlockSpec((H, HY_CT), lambda o, j: (0, 2 * nct * o + nct + j)),
                  full((L, 1)),
                  pl.BlockSpec((1, HY_CT), lambda o, j: (0, j)),
                  _const_spec(tab.shape, lambda o, j: (0, 0, 0))],
        out_specs=[kspec, kspec, pl.BlockSpec((1, 4, HY_CT), lambda o, j: (o, 0, j))],
        out_shape=[kshape, kshape, jax.ShapeDtypeStruct((HY_ORDER, 4, HY_WIDTH), f32)],
        scratch_shapes=[pltpu.VMEM((L, H), f32)],
        compiler_params=pltpu.CompilerParams(dimension_semantics=("arbitrary", "arbitrary"),
                                             vmem_limit_bytes=VMEM_LIMIT),
        name="hyena_filters",
    )(z, w1p, row(b1), w2.astype(f32), row(b2), w3.astype(f32), row(b3), row(freq),
      w4.astype(f32), w4.astype(f32), t, rate, tab)


def _hyena_kernel(xn_ref, wv_ref, wx1_ref, wx2_ref, wg_ref, bv_ref, bx1_ref, bx2_ref, bg_ref,
                  cwv_ref, cwx1_ref, cwx2_ref, cbv_ref, cbx1_ref, cbx2_ref,
                  tab_ref, kr_ref, ki_ref, ks_ref, skip_ref, o_ref,
                  z_ref, u_ref, ub_ref, pq_ref, x1_ref, x2_ref, g_ref):
    L = xn_ref.shape[1]
    M = L // HY_PH
    nlt = HY_CT // LANES
    row_chunks = [slice(r, r + HY_RC) for r in range(0, L, HY_RC)]
    chunks = [slice(r, r + HY_RC) for r in range(0, M, HY_RC)]
    fchunks = [slice(r, r + HY_FC) for r in range(0, M, HY_FC)]
    row = lax.broadcasted_iota(jnp.int32, (HY_RC, HY_CT), 0)
    sgn = jnp.where((row & 1) == 1, -1.0, 1.0).astype(f32)
    for k in range(z_ref.shape[0]):
        for lt in range(nlt):
            z_ref[k, lt, 0:HALO] = jnp.zeros((HALO, LANES), f32)
            z_ref[k, lt, L + HALO:L + 2 * HALO] = jnp.zeros((HALO, LANES), f32)

    def proj_conv(k, dst_ref, w_ref, b_ref, cw_ref, cb_ref):
        w = w_ref[...].astype(bf16)
        for c in row_chunks:
            val = _dot(xn_ref[0, c, :], w) + b_ref[...]
            for lt in range(nlt):
                z_ref[k, lt, HALO + c.start:HALO + c.stop, :] = val[:, lt * LANES:(lt + 1) * LANES]

        def phase_rows(p, c, shift=0):
            src = pl.ds(HALO + p + shift + HY_PH * c.start, HY_RC, stride=HY_PH)
            return jnp.concatenate([z_ref[k, lt, src, :] for lt in range(nlt)], axis=1)

        for p in range(HY_PH):
            for c in chunks:
                dst_ref[p, c] = (cb_ref[...] + phase_rows(p, c, -1) * cw_ref[0:1, :]
                                 + phase_rows(p, c) * cw_ref[1:2, :] + phase_rows(p, c, 1) * cw_ref[2:3, :])

    def long_conv(o, x_ref):
        r = [jnp.zeros((1, HY_CT), f32) for _ in range(HY_PH)]
        for p in range(HY_PH):
            for c in chunks:
                u = u_ref[p, c]
                ub_ref[p, c] = u.astype(bf16)
                r[p] = r[p] + jnp.sum(u * sgn, axis=0, keepdims=True)
        for c in fchunks:
            A = [_dot(tab_ref[2 * p, c, :], ub_ref[p]) for p in range(HY_PH)]
            B = [_dot(tab_ref[2 * p + 1, c, :], ub_ref[p]) for p in range(HY_PH)]
            e, f = _butterfly(A, B)
            a, b = _spectrum_cos(e, f), _spectrum_sin(e, f)
            P, Q = [], []
            for cls in range(HY_PH):
                kr, ki = kr_ref[o, cls, c, :], ki_ref[o, cls, c, :]
                P.append(a[cls] * kr + b[cls] * ki)
                Q.append(b[cls] * kr - a[cls] * ki)
            g = (P[0] + P[2], P[0] - P[2], P[1] + P[3], P[3] - P[1])
            h = (Q[0] - Q[2], Q[0] + Q[2], Q[1] - Q[3], Q[1] + Q[3])
            X = (g[0] + g[2], g[1] + h[3], g[0] - g[2], g[1] - h[3])
            Y = (h[0] + h[2], h[1] + g[3], h[0] - h[2], h[1] - g[3])
            for p in range(HY_PH):
                pq_ref[2 * p, c] = X[p].astype(bf16)
                pq_ref[2 * p + 1, c] = Y[p].astype(bf16)
        pq_odd = []
        for j, (a_o, b_o) in enumerate(_odd_bins(r)):
            kr, ki = ks_ref[o, 2 * j:2 * j + 1, :], ks_ref[o, 2 * j + 1:2 * j + 2, :]
            pq_odd.append((a_o * kr + b_o * ki, b_o * kr - a_o * ki))
        (p1, q1), (p3, q3) = pq_odd
        odd = (p1 + p3, (p1 + q1 - p3 + q3) * RSQRT2, q1 - q3, (q1 - p1 + p3 + q3) * RSQRT2)
        skip = skip_ref[o:o + 1, :]
        nt = 2 * HY_PH
        for p in range(HY_PH):
            for c in chunks:
                y = _dot(tab_ref[nt + 2 * p, c, :], pq_ref[2 * p]) + _dot(tab_ref[nt + 2 * p + 1, c, :], pq_ref[2 * p + 1])
                u_ref[p, c] = x_ref[p, c] * (y + sgn * odd[p] + u_ref[p, c] * skip)

    proj_conv(0, u_ref, wv_ref, bv_ref, cwv_ref, cbv_ref)
    proj_conv(1, x1_ref, wx1_ref, bx1_ref, cwx1_ref, cbx1_ref)
    proj_conv(2, x2_ref, wx2_ref, bx2_ref, cwx2_ref, cbx2_ref)
    wg = wg_ref[...].astype(bf16)
    for c in row_chunks:
        g = _dot(xn_ref[0, c, :], wg) + bg_ref[...]
        g_ref[c] = g * jax.nn.sigmoid(g)
    long_conv(0, x1_ref)
    long_conv(1, x2_ref)
    for p in range(HY_PH):
        for c in chunks:
            dst = pl.ds(HALO + p + HY_PH * c.start, HY_RC, stride=HY_PH)
            y = u_ref[p, c]
            for lt in range(nlt):
                z_ref[0, lt, dst, :] = y[:, lt * LANES:(lt + 1) * LANES]
    for c in row_chunks:
        y = jnp.concatenate([z_ref[0, lt, HALO + c.start:HALO + c.stop, :] for lt in range(nlt)], axis=1)
        o_ref[0, c, :] = (y * g_ref[c]).astype(o_ref.dtype)


def _hyena(xs, w_in, b_in, conv_w, conv_b, tab, kr, ki, ks, skip):
    _, B, L, _ = xs.shape
    M = L // HY_PH
    nct = HY_WIDTH // HY_CT
    hg = O_HGATE // HY_CT

    def col(k):
        return lambda j, b: (0, k * nct + j)

    wspec = lambda k: _const_spec((D_MODEL, HY_CT), col(k))
    bspec = lambda k: pl.BlockSpec((1, HY_CT), col(k))
    cwspec = lambda k: pl.BlockSpec((HY_SHORT_CONV, HY_CT), col(k))
    kspec = _const_spec((HY_ORDER, HY_PH, M, HY_CT), lambda j, b: (0, 0, 0, j))
    return pl.pallas_call(
        _hyena_kernel,
        grid=(nct, B),
        in_specs=[pl.BlockSpec((None, 1, L, D_MODEL), lambda j, b: (NAT, b, 0, 0)),
                  wspec(0), wspec(1), wspec(2), _const_spec((D_MODEL, HY_CT), lambda j, b: (0, hg + j)),
                  bspec(0), bspec(1), bspec(2), pl.BlockSpec((1, HY_CT), lambda j, b: (0, hg + j)),
                  cwspec(0), cwspec(1), cwspec(2), bspec(0), bspec(1), bspec(2),
                  _const_spec(tab.shape, lambda j, b: (0, 0, 0)),
                  kspec, kspec,
                  pl.BlockSpec((HY_ORDER, 4, HY_CT), lambda j, b: (0, 0, j)),
                  pl.BlockSpec((HY_ORDER, HY_CT), lambda j, b: (0, j))],
        out_specs=pl.BlockSpec((1, L, HY_CT), lambda j, b: (b, 0, j)),
        out_shape=jax.ShapeDtypeStruct((B, L, HY_WIDTH), bf16),
        scratch_shapes=[pltpu.VMEM((3, HY_CT // LANES, L + 2 * HALO, LANES), f32), pltpu.VMEM((HY_PH, M, HY_CT), f32),
                        pltpu.VMEM((HY_PH, M, HY_CT), bf16), pltpu.VMEM((2 * HY_PH, M, HY_CT), bf16),
                        pltpu.VMEM((HY_PH, M, HY_CT), f32), pltpu.VMEM((HY_PH, M, HY_CT), f32),
                        pltpu.VMEM((L, HY_CT), f32)],
        compiler_params=pltpu.CompilerParams(dimension_semantics=("arbitrary", "arbitrary"),
                                             vmem_limit_bytes=VMEM_LIMIT),
        name="hyena_mixer",
    )(xs, w_in, w_in, w_in, w_in, b_in, b_in, b_in, b_in,
      conv_w, conv_w, conv_w, conv_b, conv_b, conv_b, tab, kr, ki, ks, skip)


def _attn_kernel(xs_ref, wq_ref, wk_ref, wv_ref, bq_ref, bk_ref, bv_ref, wag_ref, bag_ref,
                 gq_ref, gk_ref, hsum_ref, sl_ref, d0_ref, d1_ref, d2_ref, o_ref,
                 qs_ref, ks_ref, vs_ref, acc_ref, mx_ref, den_ref):
    L = xs_ref.shape[2]
    gw = HEADS_PER_GROUP * HEAD_DIM
    npair = HEADS_PER_GROUP // 2
    dist_refs = (d0_ref, d1_ref, d2_ref)
    tq = Q_TILE
    first = lax.broadcasted_iota(jnp.int32, (tq, PAIR), 1) < HEAD_DIM
    nt_dims = (((1,), (1,)), ((), ()))

    def normed(x, w, b_ref, g_ref):
        z = _dot(x, w) + b_ref[...]
        z2 = (z * z).astype(bf16)
        ssq = jnp.concatenate([_dot(z2[:, c:c + MXU_DIM], hsum_ref[...]) for c in range(0, gw, MXU_DIM)], axis=1)
        return z * lax.rsqrt(ssq * (1.0 / HEAD_DIM) + NORM_EPS) * g_ref[...]

    def group(gi):
        _, d = DILATED_GROUPS[GROUP_ORDER[gi]]
        n = L // d
        w = min(2 * tq, n)
        per_class = n // tq
        dist_ref = dist_refs[GROUP_ORDER[gi]]

        wq, wk, wv = (r[...].astype(bf16) for r in (wq_ref, wk_ref, wv_ref))
        for r0 in range(0, L, AT_RC):
            rows = slice(r0, r0 + AT_RC)
            x = xs_ref[0, 0, rows, :]
            qs_ref[rows] = (normed(x, wq, bq_ref, gq_ref) * (HEAD_DIM ** -0.5)).astype(bf16)
            ks_ref[rows] = normed(x, wk, bk_ref, gk_ref).astype(bf16)
            vs_ref[rows] = (_dot(x, wv) + bv_ref[...]).astype(bf16)

        def tile(idx, carry):
            r = idx // per_class
            t = idx % per_class
            q0 = pl.multiple_of(idx * tq, tq)
            koff = jnp.clip(t * tq - BAND_HALF, 0, n - w)
            dist_t = dist_ref[(t * tq - koff) // BAND_HALF]
            k0 = pl.multiple_of(r * n + koff, BAND_HALF)
            nat = pl.ds(t * tq * d + r, tq, stride=d) if d > 1 else pl.ds(q0, tq)
            for p in range(npair):
                pc = slice(p * PAIR, (p + 1) * PAIR)
                q = qs_ref[pl.ds(q0, tq), pc]
                zero = jnp.zeros_like(q)
                qq = jnp.concatenate([jnp.where(first, q, zero), jnp.where(first, zero, q)], axis=0)
                lhs = jnp.concatenate([qq, sl_ref[p]], axis=1)
                rhs_t = jnp.concatenate([ks_ref[pl.ds(k0, w), pc], dist_t], axis=1)
                s = lax.dot_general(lhs, rhs_t, nt_dims, preferred_element_type=f32)
                m = jnp.max(s, axis=-1, keepdims=True)
                pr = jnp.exp(s - m).astype(bf16)
                rhs = jnp.concatenate([vs_ref[pl.ds(k0, w), pc], jnp.ones((w, PAIR), bf16)], axis=1)
                ov = _dot(pr, rhs)
                num = jnp.where(first, ov[0:tq, 0:PAIR], ov[tq:2 * tq, 0:PAIR])
                den = jnp.where(first, ov[0:tq, PAIR:2 * PAIR], ov[tq:2 * tq, PAIR:2 * PAIR])
                mb = jnp.where(first, m[0:tq], m[tq:2 * tq])
                if gi == 0:
                    acc_ref[p, nat, :] = num
                    den_ref[p, nat, :] = den
                    mx_ref[p, nat, :] = mb
                else:
                    m_old = mx_ref[p, nat, :]
                    m_new = jnp.maximum(m_old, mb)
                    a = jnp.exp(m_old - m_new)
                    b = jnp.exp(mb - m_new)
                    acc_ref[p, nat, :] = acc_ref[p, nat, :] * a + num * b
                    den_ref[p, nat, :] = den_ref[p, nat, :] * a + den * b
                    mx_ref[p, nat, :] = m_new
            return carry

        lax.fori_loop(0, L // tq, tile, 0, unroll=8)

    for gi in range(N_GROUPS):
        pl.when(pl.program_id(1) == gi)(functools.partial(group, gi))

    @pl.when(pl.program_id(1) == N_GROUPS - 1)
    def _():
        wag = wag_ref[...].astype(bf16)
        for r0 in range(0, L, AT_RC):
            rows = slice(r0, r0 + AT_RC)
            ag = _dot(xs_ref[0, 0, rows, :], wag) + bag_ref[...]
            o = jnp.concatenate([acc_ref[p, rows, :] / den_ref[p, rows, :] for p in range(npair)], axis=1)
            o_ref[0, rows, :] = (o * (ag * jax.nn.sigmoid(ag))).astype(o_ref.dtype)


def _attention(xs, w_in, b_in, gq, gk, hsum):
    _, B, L, _ = xs.shape
    assert PERM_DILATIONS == tuple(DILATED_GROUPS[g][1] for g in GROUP_ORDER) and PERM_DILATIONS[-1] == 1
    assert GROUP_ORDER == tuple(N_GROUPS - 1 - i for i in range(N_GROUPS))
    assert L % AT_RC == 0 and all(L % (Q_TILE * d) == 0 for _, d in DILATED_GROUPS)
    gw = HEADS_PER_GROUP * HEAD_DIM
    npair = HEADS_PER_GROUP // 2
    dists = [jnp.asarray(_attn_dist(L // d, d, window)).astype(bf16) for window, d in DILATED_GROUPS]
    sl = jnp.asarray(_slope_eye()).astype(bf16)
    agb = O_AGATE // gw

    def col(k):
        return lambda b, i: (0, (O_QKV + k * AT_QKV) // gw + (N_GROUPS - 1 - i))

    wspec = lambda k: pl.BlockSpec((D_MODEL, gw), col(k))
    bspec = lambda k: pl.BlockSpec((1, gw), col(k))
    vec = pl.BlockSpec((1, gw), lambda b, i: (0, 0))
    acc = pltpu.VMEM((npair, L, PAIR), f32)
    return pl.pallas_call(
        _attn_kernel,
        grid=(B, N_GROUPS),
        in_specs=[pl.BlockSpec((1, 1, L, D_MODEL), lambda b, i: (i, b, 0, 0)),
                  wspec(0), wspec(1), wspec(2), bspec(0), bspec(1), bspec(2),
                  _const_spec((D_MODEL, gw), lambda b, i: (0, agb)), pl.BlockSpec((1, gw), lambda b, i: (0, agb)),
                  vec, vec, _const_spec(hsum.shape, lambda b, i: (0, 0)), _const_spec(sl.shape, lambda b, i: (0, 0, 0))]
                 + [_const_spec(t.shape, lambda b, i: (0, 0, 0)) for t in dists],
        out_specs=pl.BlockSpec((1, L, gw), lambda b, i: (b, 0, 0)),
        out_shape=jax.ShapeDtypeStruct((B, L, gw), bf16),
        scratch_shapes=[pltpu.VMEM((L, gw), bf16), pltpu.VMEM((L, gw), bf16), pltpu.VMEM((L, gw), bf16),
                        acc, acc, acc],
        compiler_params=pltpu.CompilerParams(dimension_semantics=("arbitrary", "arbitrary"),
                                             vmem_limit_bytes=VMEM_LIMIT),
        name="dilated_attention",
    )(xs, w_in, w_in, w_in, b_in, b_in, b_in, w_in, b_in, gq, gk, hsum, sl, *dists)


def _final_kernel(x_ref, xn_ref, gh_ref, ga_ref, wg_ref, bg_ref, why_ref, wat_ref, wout_ref, out_ref,
                  wg_b, why_b, wat_b, wout_b):
    @pl.when(pl.program_id(0) == 0)
    def _():
        for src, dst in ((wg_ref, wg_b), (why_ref, why_b), (wat_ref, wat_b), (wout_ref, wout_b)):
            dst[...] = src[...].astype(bf16)

    gates = _dot(xn_ref[...], wg_b[...]) + bg_ref[...]
    u_h = _dot(gh_ref[...], why_b[...])
    u_a = _dot(ga_ref[...], wat_b[...])
    merged = jax.nn.sigmoid(gates[:, 0:D_MODEL]) * u_h + jax.nn.sigmoid(gates[:, D_MODEL:]) * u_a
    out_ref[...] = x_ref[...] + _dot(merged.astype(bf16), wout_b[...])


def _final(x2, xs2, gh2, ga2, w_in, b_in, why, wat, wout):
    rows = x2.shape[0]
    tm = 512
    mgw = 2 * D_MODEL
    rspec = lambda c: pl.BlockSpec((tm, c), lambda i: (i, 0))
    cspec = lambda a: _const_spec(a.shape, lambda i: (0, 0))
    return pl.pallas_call(
        _final_kernel,
        grid=(rows // tm,),
        in_specs=[rspec(D_MODEL), pl.BlockSpec((None, tm, D_MODEL), lambda i: (NAT, i, 0)), rspec(HY_WIDTH), rspec(AT_WIDTH),
                  _const_spec((D_MODEL, mgw), lambda i: (0, O_MG // mgw)),
                  pl.BlockSpec((1, mgw), lambda i: (0, O_MG // mgw)), cspec(why), cspec(wat), cspec(wout)],
        out_specs=rspec(D_MODEL),
        out_shape=jax.ShapeDtypeStruct((rows, D_MODEL), f32),
        scratch_shapes=[pltpu.VMEM((D_MODEL, mgw), bf16), pltpu.VMEM(why.shape, bf16), pltpu.VMEM(wat.shape, bf16),
                        pltpu.VMEM(wout.shape, bf16)],
        compiler_params=pltpu.CompilerParams(dimension_semantics=("arbitrary",),
                                             vmem_limit_bytes=VMEM_LIMIT),
        name="merge_output",
    )(x2, xs2, gh2, ga2, w_in, b_in, why, wat, wout)


def _layer(x, norm_g, w_in, b_in, conv_w, conv_b, hf_w1, hf_b1, hf_w2, hf_b2, hf_w3, hf_b3, hf_w4,
           hf_freq, hy_skip, q_norm_g, k_norm_g, w_hy_out, w_at_out, w_out):
    B, L, D = x.shape
    x2 = x.reshape(B * L, D)
    tab = jnp.asarray(_dft_tables(L)).astype(bf16)
    w_in = w_in.astype(f32)
    b_in2 = b_in.astype(f32).reshape(1, IN_COLS)

    xs = _prenorm(x, norm_g.astype(f32).reshape(1, D))

    kr, ki, ks = _filters(L, tab, hf_w1, hf_b1, hf_w2, hf_b2, hf_w3, hf_b3, hf_w4, hf_freq)
    gh = _hyena(xs, w_in, b_in2, conv_w.astype(f32), conv_b.astype(f32).reshape(1, -1),
                tab, kr, ki, ks, hy_skip.astype(f32))

    gq = jnp.tile(q_norm_g.astype(f32), HEADS_PER_GROUP).reshape(1, -1)
    gk = jnp.tile(k_norm_g.astype(f32), HEADS_PER_GROUP).reshape(1, -1)
    head = np.arange(MXU_DIM) // HEAD_DIM
    hsum = jnp.asarray((head[:, None] == head[None, :]).astype(np.float32)).astype(bf16)
    ga = _attention(xs, w_in, b_in2, gq, gk, hsum)

    out = _final(x2, xs.reshape(len(PERM_DILATIONS), B * L, D), gh.reshape(B * L, HY_WIDTH), ga.reshape(B * L, AT_WIDTH),
                 w_in, b_in2, w_hy_out.astype(f32), w_at_out.astype(f32), w_out.astype(f32))
    return out.reshape(B, L, D)


def kernel(x, norm_g, w_in, b_in, conv_w, conv_b, hf_w1, hf_b1, hf_w2, hf_b2, hf_w3, hf_b3, hf_w4,
           hf_freq, hy_skip, q_norm_g, k_norm_g, w_hy_out, w_at_out, w_out):
    depth = norm_g.shape[0]
    for i in range(depth):
        x = _layer(x, norm_g[i], w_in[i], b_in[i], conv_w[i], conv_b[i], hf_w1[i], hf_b1[i], hf_w2[i],
                   hf_b2[i], hf_w3[i], hf_b3[i], hf_w4[i], hf_freq[i], hy_skip[i], q_norm_g[i],
                   k_norm_g[i], w_hy_out[i], w_at_out[i], w_out[i])
    return x
```

```python
import functools
import math

import jax
import jax.numpy as jnp
import numpy as np
from jax import lax
from jax.experimental import pallas as pl
from jax.experimental.pallas import tpu as pltpu

D_MODEL = 1024
HY_WIDTH = 768
HY_ORDER = 2
HY_SHORT_CONV = 3
HY_EMB_DIM = 33
HY_FILTER_HIDDEN = 64
HY_FAST_DECAY = 0.3
HY_SLOW_DECAY = 1.5
HY_DECAY_TARGET = 1e-2
HY_MOD_SHIFT = 0.0
HEAD_DIM = 64
HEADS_PER_GROUP = 8
DILATED_GROUPS = ((128, 1), (512, 4), (2048, 16))
N_GROUPS = 3
AT_QKV = N_GROUPS * HEADS_PER_GROUP * HEAD_DIM
AT_WIDTH = HEADS_PER_GROUP * HEAD_DIM
NORM_EPS = 1e-6
NEG_INF = -1e30

O_HGATE = 3 * HY_WIDTH
O_QKV = O_HGATE + HY_WIDTH
O_AGATE = O_QKV + 3 * AT_QKV
O_MG = O_AGATE + AT_WIDTH
IN_COLS = O_MG + 2 * D_MODEL

LANES = 128
MXU_DIM = 256
VMEM_LIMIT = 56 * 1024 * 1024

HY_CT = 256
HY_RC = 512
HY_FC = 256
HY_PH = 4
RSQRT2 = math.sqrt(0.5)
HALO = 16
AT_RC = 512
EMB_PAD = 128
Q_TILE = 128
BAND_HALF = 64
GROUP_ORDER = (2, 1, 0)
PERM_DILATIONS = (16, 4, 1)
NAT = 2
PN_RC = 512
PAIR = 2 * HEAD_DIM

f32 = jnp.float32
bf16 = jnp.bfloat16


def _dot(a, b):
    return jnp.dot(a, b, preferred_element_type=f32)


def _const_spec(shape, index_map):
    return pl.BlockSpec(shape, index_map, pipeline_mode=pl.Buffered(1))


@functools.lru_cache(maxsize=None)
def _dft_tables(L):
    n = 2 * L
    f = np.arange(L // HY_PH, dtype=np.int64)[:, None]
    m = np.arange(L // HY_PH, dtype=np.int64)[None, :]
    fwd = []
    for p in range(HY_PH):
        ang = ((f * (HY_PH * m + p)) % n).astype(np.float64) * (2.0 * np.pi / n)
        fwd += [np.cos(ang), np.sin(ang)]
    return np.stack(fwd + [t.T for t in fwd]).astype(np.float32)


def _butterfly(A, B):
    e = (A[0] + A[2], A[0] - A[2], A[1] + A[3], A[1] - A[3])
    f = (B[0] + B[2], B[0] - B[2], B[1] + B[3], B[1] - B[3])
    return e, f


def _spectrum_cos(e, f):
    return (e[0] + e[2], e[1] - f[3], e[0] - e[2], e[1] + f[3])


def _spectrum_sin(e, f):
    return (f[0] + f[2], f[1] + e[3], f[2] - f[0], e[3] - f[1])


def _odd_bins(r):
    d, s = (r[1] - r[3]) * RSQRT2, (r[1] + r[3]) * RSQRT2
    return (r[0] + d, s + r[2]), (r[0] - d, s - r[2])


def _phase_major(a, L):
    return np.concatenate([a[p::HY_PH] for p in range(HY_PH)], axis=0)


@functools.lru_cache(maxsize=None)
def _filter_embedding(L):
    t = np.linspace(0.0, 1.0, L)[:, None]
    bands = (HY_EMB_DIM - 1) // 2
    w = 2.0 * np.pi * np.arange(L)[:, None] / L
    f = np.linspace(1e-4, bands - 1, bands)[None, :]
    z = np.concatenate([t, np.cos(f * w), -np.sin(f * w)], axis=-1)
    zp = np.zeros((L, EMB_PAD), np.float64)
    zp[:, :HY_EMB_DIM] = z
    return zp.astype(np.float32)


@functools.lru_cache(maxsize=None)
def _decay_rates():
    max_decay = math.log(HY_DECAY_TARGET) / HY_FAST_DECAY
    min_decay = math.log(HY_DECAY_TARGET) / HY_SLOW_DECAY
    return np.abs(np.linspace(min_decay, max_decay, HY_WIDTH))[None, :].astype(np.float32)


def _alibi_slope(h):
    return 2.0 ** (-8.0 * (h + 1) / HEADS_PER_GROUP)


@functools.lru_cache(maxsize=None)
def _attn_dist(n, dilation, window):
    half = window // (2 * dilation)
    assert half == BAND_HALF
    tq = min(Q_TILE, n)
    w = min(2 * Q_TILE, n)
    masked = NEG_INF / _alibi_slope(HEADS_PER_GROUP - 1)
    offs = sorted({q0 - min(max(q0 - half, 0), n - w) for q0 in range(0, n, tq)})
    assert offs == [BAND_HALF * i for i in range(len(offs))]
    out = np.zeros((len(offs), w, tq), np.float32)
    for ci, off in enumerate(offs):
        rel = np.arange(tq)[None, :] + off - np.arange(w)[:, None]
        out[ci] = np.where(np.abs(rel) <= half, -dilation * np.abs(rel), masked)
    return out


@functools.lru_cache(maxsize=None)
def _slope_eye():
    eye = np.eye(Q_TILE, dtype=np.float32)
    return np.stack([np.concatenate([_alibi_slope(2 * p) * eye, _alibi_slope(2 * p + 1) * eye], axis=0)
                     for p in range(HEADS_PER_GROUP // 2)])


@functools.lru_cache(maxsize=None)
def _residue_perm(n):
    p = np.zeros((n, n), np.float32)
    i = np.arange(n)
    p[(i % 4) * (n // 4) + i // 4, i] = 1.0
    return p


def _prenorm_kernel(x_ref, g_ref, perm_ref, o_ref, st_ref):
    L = x_ref.shape[1]
    nt = D_MODEL // LANES
    d4, d16 = PERM_DILATIONS.index(4), PERM_DILATIONS.index(16)
    n4, n16 = L // 4, L // 16
    for r0 in range(0, L, PN_RC):
        rows = slice(r0, r0 + PN_RC)
        x = x_ref[0, rows, :]
        ms = jnp.mean(x * x, axis=-1, keepdims=True)
        xn = x * lax.rsqrt(ms + NORM_EPS) * g_ref[...]
        o_ref[NAT, 0, rows, :] = xn.astype(o_ref.dtype)
        for c in range(nt):
            st_ref[c, rows, :] = xn[:, c * LANES:(c + 1) * LANES]

    def gather(r, carry):
        dst = pl.ds(pl.multiple_of(r * n4, n4), n4)
        for c in range(nt):
            o_ref[d4, 0, dst, c * LANES:(c + 1) * LANES] = st_ref[c, pl.ds(r, n4, stride=4), :].astype(o_ref.dtype)
        return carry

    lax.fori_loop(0, 4, gather, 0)
    for r4 in range(4):
        y = _dot(perm_ref[...], o_ref[d4, 0, r4 * n4:(r4 + 1) * n4, :]).astype(o_ref.dtype)
        for q in range(4):
            o_ref[d16, 0, (r4 + 4 * q) * n16:(r4 + 4 * q + 1) * n16, :] = y[q * n16:(q + 1) * n16]


def _prenorm(x, g):
    B, L, D = x.shape
    assert sorted(PERM_DILATIONS) == [1, 4, 16] and PERM_DILATIONS[NAT] == 1
    perm = jnp.asarray(_residue_perm(L // 4)).astype(bf16)
    return pl.pallas_call(
        _prenorm_kernel,
        grid=(B,),
        in_specs=[pl.BlockSpec((1, L, D), lambda b: (b, 0, 0)),
                  pl.BlockSpec((1, D), lambda b: (0, 0)),
                  _const_spec(perm.shape, lambda b: (0, 0))],
        out_specs=pl.BlockSpec((len(PERM_DILATIONS), 1, L, D), lambda b: (0, b, 0, 0)),
        out_shape=jax.ShapeDtypeStruct((len(PERM_DILATIONS), B, L, D), bf16),
        scratch_shapes=[pltpu.VMEM((D // LANES, L, LANES), f32)],
        compiler_params=pltpu.CompilerParams(dimension_semantics=("arbitrary",),
                                             vmem_limit_bytes=VMEM_LIMIT),
        name="prenorm",
    )(x, g, perm)


def _filters_kernel(z_ref, w1_ref, b1_ref, w2_ref, b2_ref, w3_ref, b3_ref, fr_ref, w4f_ref, w4b_ref,
                    t_ref, rate_ref, tab_ref, kr_ref, ki_ref, ks_ref, h3_ref):
    L = z_ref.shape[0]
    M = L // HY_PH
    hi = lax.Precision.HIGHEST

    @pl.when((pl.program_id(0) == 0) & (pl.program_id(1) == 0))
    def _():
        fr = fr_ref[...]
        h = jnp.sin(fr * (jnp.dot(z_ref[...], w1_ref[...], precision=hi, preferred_element_type=f32)
                          + b1_ref[...]))
        h = jnp.sin(fr * (jnp.dot(h, w2_ref[...], precision=hi, preferred_element_type=f32) + b2_ref[...]))
        h = jnp.sin(fr * (jnp.dot(h, w3_ref[...], precision=hi, preferred_element_type=f32) + b3_ref[...]))
        h3_ref[...] = h

    h3 = h3_ref[...]
    decay = jnp.exp(-t_ref[...] * rate_ref[...]) + HY_MOD_SHIFT
    hf = jnp.dot(h3, w4f_ref[...], precision=hi, preferred_element_type=f32) * decay
    hb = jnp.dot(h3, w4b_ref[...], precision=hi, preferred_element_type=f32) * decay
    hb0 = hb[0:1, :]
    hs = hf + hb
    hd = hb - hf
    n = 2 * L
    row = lax.broadcasted_iota(jnp.int32, (M, HY_CT), 0)
    sgn = jnp.where((row & 1) == 1, -1.0, 1.0).astype(f32)

    def transform(x):
        xp = [x[p * M:(p + 1) * M] for p in range(HY_PH)]
        A = [_dot(tab_ref[2 * p], xp[p].astype(bf16)) for p in range(HY_PH)]
        B = [_dot(tab_ref[2 * p + 1], xp[p].astype(bf16)) for p in range(HY_PH)]
        r = [jnp.sum(xp[p] * sgn, axis=0, keepdims=True) for p in range(HY_PH)]
        return _butterfly(A, B), _odd_bins(r)

    (es, fs), odd_s = transform(hs)
    (ed, fd), odd_d = transform(hd)
    kr = _spectrum_cos(es, fs)
    ki = _spectrum_sin(ed, fd)
    two = 2.0 / n
    edge = jnp.where(row == 0, 1.0 / n, two).astype(f32)
    once = jnp.where(row == 0, 0.0, two).astype(f32)
    for cls, wgt in enumerate((edge, two, edge, once)):
        kr_ref[0, cls] = (kr[cls] - hb0) * wgt
        ki_ref[0, cls] = ki[cls] * wgt
    for j in range(2):
        ks_ref[0, 2 * j:2 * j + 1, :] = (odd_s[j][0] - hb0) * two
        ks_ref[0, 2 * j + 1:2 * j + 2, :] = odd_d[j][1] * two


def _filters(L, tab, w1, b1, w2, b2, w3, b3, w4, freq):
    M = L // HY_PH
    z = jnp.asarray(_phase_major(_filter_embedding(L), L))
    t = jnp.asarray(_phase_major(np.linspace(0.0, 1.0, L)[:, None].astype(np.float32), L))
    rate = jnp.asarray(_decay_rates())
    w1p = jnp.zeros((EMB_PAD, HY_FILTER_HIDDEN), f32).at[:HY_EMB_DIM].set(w1.astype(f32))
    nct = HY_WIDTH // HY_CT
    row = lambda a: a.astype(f32).reshape(1, -1)
    full = lambda shape: pl.BlockSpec(shape, lambda o, j: (0,) * len(shape))
    H = HY_FILTER_HIDDEN
    kspec = pl.BlockSpec((1, HY_PH, M, HY_CT), lambda o, j: (o, 0, 0, j))
    kshape = jax.ShapeDtypeStruct((HY_ORDER, HY_PH, M, HY_WIDTH), f32)
    return pl.pallas_call(
        _filters_kernel,
        grid=(HY_ORDER, nct),
        in_specs=[full((L, EMB_PAD)), full((EMB_PAD, H)), full((1, H)), full((H, H)), full((1, H)),
                  full((H, H)), full((1, H)), full((1, H)),
                  pl.BlockSpec((H, HY_CT), lambda o, j: (0, 2 * nct * o + j)),
                  pl.BlockSpec((H, HY_CT), lambda o, j: (0, 2 * nct * o + nct + j)),
                  full((L, 1)),
                  pl.BlockSpec((1, HY_CT), lambda o, j: (0, j)),
                  _const_spec(tab.shape, lambda o, j: (0, 0, 0))],
        out_specs=[kspec, kspec, pl.BlockSpec((1, 4, HY_CT), lambda o, j: (o, 0, j))],
        out_shape=[kshape, kshape, jax.ShapeDtypeStruct((HY_ORDER, 4, HY_WIDTH), f32)],
        scratch_shapes=[pltpu.VMEM((L, H), f32)],
        compiler_params=pltpu.CompilerParams(dimension_semantics=("arbitrary", "arbitrary"),
                                             vmem_limit_bytes=VMEM_LIMIT),
        name="hyena_filters",
    )(z, w1p, row(b1), w2.astype(f32), row(b2), w3.astype(f32), row(b3), row(freq),
      w4.astype(f32), w4.astype(f32), t, rate, tab)


def _hyena_kernel(xn_ref, wv_ref, wx1_ref, wx2_ref, wg_ref, bv_ref, bx1_ref, bx2_ref, bg_ref,
                  cwv_ref, cwx1_ref, cwx2_ref, cbv_ref, cbx1_ref, cbx2_ref,
                  tab_ref, kr_ref, ki_ref, ks_ref, skip_ref, o_ref,
                  z_ref, u_ref, ub_ref, pq_ref, x1_ref, x2_ref, g_ref):
    L = xn_ref.shape[1]
    M = L // HY_PH
    nlt = HY_CT // LANES
    row_chunks = [slice(r, r + HY_RC) for r in range(0, L, HY_RC)]
    chunks = [slice(r, r + HY_RC) for r in range(0, M, HY_RC)]
    fchunks = [slice(r, r + HY_FC) for r in range(0, M, HY_FC)]
    row = lax.broadcasted_iota(jnp.int32, (HY_RC, HY_CT), 0)
    sgn = jnp.where((row & 1) == 1, -1.0, 1.0).astype(f32)
    for k in range(z_ref.shape[0]):
        for lt in range(nlt):
            z_ref[k, lt, 0:HALO] = jnp.zeros((HALO, LANES), f32)
            z_ref[k, lt, L + HALO:L + 2 * HALO] = jnp.zeros((HALO, LANES), f32)

    def proj_conv(k, dst_ref, w_ref, b_ref, cw_ref, cb_ref):
        w = w_ref[...].astype(bf16)
        for c in row_chunks:
            val = _dot(xn_ref[0, c, :], w) + b_ref[...]
            for lt in range(nlt):
                z_ref[k, lt, HALO + c.start:HALO + c.stop, :] = val[:, lt * LANES:(lt + 1) * LANES]

        def phase_rows(p, c, shift=0):
            src = pl.ds(HALO + p + shift + HY_PH * c.start, HY_RC, stride=HY_PH)
            return jnp.concatenate([z_ref[k, lt, src, :] for lt in range(nlt)], axis=1)

        for p in range(HY_PH):
            for c in chunks:
                dst_ref[p, c] = (cb_ref[...] + phase_rows(p, c, -1) * cw_ref[0:1, :]
                                 + phase_rows(p, c) * cw_ref[1:2, :] + phase_rows(p, c, 1) * cw_ref[2:3, :])

    def long_conv(o, x_ref):
        r = [jnp.zeros((1, HY_CT), f32) for _ in range(HY_PH)]
        for p in range(HY_PH):
            for c in chunks:
                u = u_ref[p, c]
                ub_ref[p, c] = u.astype(bf16)
                r[p] = r[p] + jnp.sum(u * sgn, axis=0, keepdims=True)
        for c in fchunks:
            A = [_dot(tab_ref[2 * p, c, :], ub_ref[p]) for p in range(HY_PH)]
            B = [_dot(tab_ref[2 * p + 1, c, :], ub_ref[p]) for p in range(HY_PH)]
            e, f = _butterfly(A, B)
            a, b = _spectrum_cos(e, f), _spectrum_sin(e, f)
            P, Q = [], []
            for cls in range(HY_PH):
                kr, ki = kr_ref[o, cls, c, :], ki_ref[o, cls, c, :]
                P.append(a[cls] * kr + b[cls] * ki)
                Q.append(b[cls] * kr - a[cls] * ki)
            g = (P[0] + P[2], P[0] - P[2], P[1] + P[3], P[3] - P[1])
            h = (Q[0] - Q[2], Q[0] + Q[2], Q[1] - Q[3], Q[1] + Q[3])
            X = (g[0] + g[2], g[1] + h[3], g[0] - g[2], g[1] - h[3])
            Y = (h[0] + h[2], h[1] + g[3], h[0] - h[2], h[1] - g[3])
            for p in range(HY_PH):
                pq_ref[2 * p, c] = X[p].astype(bf16)
                pq_ref[2 * p + 1, c] = Y[p].astype(bf16)
        pq_odd = []
        for j, (a_o, b_o) in enumerate(_odd_bins(r)):
            kr, ki = ks_ref[o, 2 * j:2 * j + 1, :], ks_ref[o, 2 * j + 1:2 * j + 2, :]
            pq_odd.append((a_o * kr + b_o * ki, b_o * kr - a_o * ki))
        (p1, q1), (p3, q3) = pq_odd
        odd = (p1 + p3, (p1 + q1 - p3 + q3) * RSQRT2, q1 - q3, (q1 - p1 + p3 + q3) * RSQRT2)
        skip = skip_ref[o:o + 1, :]
        nt = 2 * HY_PH
        for p in range(HY_PH):
            for c in chunks:
                y = _dot(tab_ref[nt + 2 * p, c, :], pq_ref[2 * p]) + _dot(tab_ref[nt + 2 * p + 1, c, :], pq_ref[2 * p + 1])
                u_ref[p, c] = x_ref[p, c] * (y + sgn * odd[p] + u_ref[p, c] * skip)

    proj_conv(0, u_ref, wv_ref, bv_ref, cwv_ref, cbv_ref)
    proj_conv(1, x1_ref, wx1_ref, bx1_ref, cwx1_ref, cbx1_ref)
    proj_conv(2, x2_ref, wx2_ref, bx2_ref, cwx2_ref, cbx2_ref)
    wg = wg_ref[...].astype(bf16)
    for c in row_chunks:
        g = _dot(xn_ref[0, c, :], wg) + bg_ref[...]
        g_ref[c] = g * jax.nn.sigmoid(g)
    long_conv(0, x1_ref)
    long_conv(1, x2_ref)
    for p in range(HY_PH):
        for c in chunks:
            dst = pl.ds(HALO + p + HY_PH * c.start, HY_RC, stride=HY_PH)
            y = u_ref[p, c]
            for lt in range(nlt):
                z_ref[0, lt, dst, :] = y[:, lt * LANES:(lt + 1) * LANES]
    for c in row_chunks:
        y = jnp.concatenate([z_ref[0, lt, HALO + c.start:HALO + c.stop, :] for lt in range(nlt)], axis=1)
        o_ref[0, c, :] = (y * g_ref[c]).astype(o_ref.dtype)


def _hyena(xs, w_in, b_in, conv_w, conv_b, tab, kr, ki, ks, skip):
    _, B, L, _ = xs.shape
    M = L // HY_PH
    nct = HY_WIDTH // HY_CT
    hg = O_HGATE // HY_CT

    def col(k):
        return lambda j, b: (0, k * nct + j)

    wspec = lambda k: _const_spec((D_MODEL, HY_CT), col(k))
    bspec = lambda k: pl.BlockSpec((1, HY_CT), col(k))
    cwspec = lambda k: pl.BlockSpec((HY_SHORT_CONV, HY_CT), col(k))
    kspec = _const_spec((HY_ORDER, HY_PH, M, HY_CT), lambda j, b: (0, 0, 0, j))
    return pl.pallas_call(
        _hyena_kernel,
        grid=(nct, B),
        in_specs=[pl.BlockSpec((None, 1, L, D_MODEL), lambda j, b: (NAT, b, 0, 0)),
                  wspec(0), wspec(1), wspec(2), _const_spec((D_MODEL, HY_CT), lambda j, b: (0, hg + j)),
                  bspec(0), bspec(1), bspec(2), pl.BlockSpec((1, HY_CT), lambda j, b: (0, hg + j)),
                  cwspec(0), cwspec(1), cwspec(2), bspec(0), bspec(1), bspec(2),
                  _const_spec(tab.shape, lambda j, b: (0, 0, 0)),
                  kspec, kspec,
                  pl.BlockSpec((HY_ORDER, 4, HY_CT), lambda j, b: (0, 0, j)),
                  pl.BlockSpec((HY_ORDER, HY_CT), lambda j, b: (0, j))],
        out_specs=pl.BlockSpec((1, L, HY_CT), lambda j, b: (b, 0, j)),
        out_shape=jax.ShapeDtypeStruct((B, L, HY_WIDTH), bf16),
        scratch_shapes=[pltpu.VMEM((3, HY_CT // LANES, L + 2 * HALO, LANES), f32), pltpu.VMEM((HY_PH, M, HY_CT), f32),
                        pltpu.VMEM((HY_PH, M, HY_CT), bf16), pltpu.VMEM((2 * HY_PH, M, HY_CT), bf16),
                        pltpu.VMEM((HY_PH, M, HY_CT), f32), pltpu.VMEM((HY_PH, M, HY_CT), f32),
                        pltpu.VMEM((L, HY_CT), f32)],
        compiler_params=pltpu.CompilerParams(dimension_semantics=("arbitrary", "arbitrary"),
                                             vmem_limit_bytes=VMEM_LIMIT),
        name="hyena_mixer",
    )(xs, w_in, w_in, w_in, w_in, b_in, b_in, b_in, b_in,
      conv_w, conv_w, conv_w, conv_b, conv_b, conv_b, tab, kr, ki, ks, skip)


def _attn_kernel(xs_ref, wq_ref, wk_ref, wv_ref, bq_ref, bk_ref, bv_ref, wag_ref, bag_ref,
                 gq_ref, gk_ref, hsum_ref, sl_ref, d0_ref, d1_ref, d2_ref, o_ref,
                 qs_ref, ks_ref, vs_ref, acc_ref, mx_ref, den_ref):
    L = xs_ref.shape[2]
    gw = HEADS_PER_GROUP * HEAD_DIM
    npair = HEADS_PER_GROUP // 2
    dist_refs = (d0_ref, d1_ref, d2_ref)
    tq = Q_TILE
    first = lax.broadcasted_iota(jnp.int32, (tq, PAIR), 1) < HEAD_DIM
    nt_dims = (((1,), (1,)), ((), ()))

    def normed(x, w, b_ref, g_ref):
        z = _dot(x, w) + b_ref[...]
        z2 = (z * z).astype(bf16)
        ssq = jnp.concatenate([_dot(z2[:, c:c + MXU_DIM], hsum_ref[...]) for c in range(0, gw, MXU_DIM)], axis=1)
        return z * lax.rsqrt(ssq * (1.0 / HEAD_DIM) + NORM_EPS) * g_ref[...]

    def group(gi):
        _, d = DILATED_GROUPS[GROUP_ORDER[gi]]
        n = L // d
        w = min(2 * tq, n)
        per_class = n // tq
        dist_ref = dist_refs[GROUP_ORDER[gi]]

        wq, wk, wv = (r[...].astype(bf16) for r in (wq_ref, wk_ref, wv_ref))
        for r0 in range(0, L, AT_RC):
            rows = slice(r0, r0 + AT_RC)
            x = xs_ref[0, 0, rows, :]
            qs_ref[rows] = (normed(x, wq, bq_ref, gq_ref) * (HEAD_DIM ** -0.5)).astype(bf16)
            ks_ref[rows] = normed(x, wk, bk_ref, gk_ref).astype(bf16)
            vs_ref[rows] = (_dot(x, wv) + bv_ref[...]).astype(bf16)

        def tile(idx, carry):
            r = idx // per_class
            t = idx % per_class
            q0 = pl.multiple_of(idx * tq, tq)
            koff = jnp.clip(t * tq - BAND_HALF, 0, n - w)
            dist_t = dist_ref[(t * tq - koff) // BAND_HALF]
            k0 = pl.multiple_of(r * n + koff, BAND_HALF)
            nat = pl.ds(t * tq * d + r, tq, stride=d) if d > 1 else pl.ds(q0, tq)
            for p in range(npair):
                pc = slice(p * PAIR, (p + 1) * PAIR)
                q = qs_ref[pl.ds(q0, tq), pc]
                zero = jnp.zeros_like(q)
                qq = jnp.concatenate([jnp.where(first, q, zero), jnp.where(first, zero, q)], axis=0)
                lhs = jnp.concatenate([qq, sl_ref[p]], axis=1)
                rhs_t = jnp.concatenate([ks_ref[pl.ds(k0, w), pc], dist_t], axis=1)
                s = lax.dot_general(lhs, rhs_t, nt_dims, preferred_element_type=f32)
                m = jnp.max(s, axis=-1, keepdims=True)
                pr = jnp.exp(s - m).astype(bf16)
                rhs = jnp.concatenate([vs_ref[pl.ds(k0, w), pc], jnp.ones((w, PAIR), bf16)], axis=1)
                ov = _dot(pr, rhs)
                num = jnp.where(first, ov[0:tq, 0:PAIR], ov[tq:2 * tq, 0:PAIR])
                den = jnp.where(first, ov[0:tq, PAIR:2 * PAIR], ov[tq:2 * tq, PAIR:2 * PAIR])
                mb = jnp.where(first, m[0:tq], m[tq:2 * tq])
                if gi == 0:
                    acc_ref[p, nat, :] = num
                    den_ref[p, nat, :] = den
                    mx_ref[p, nat, :] = mb
                else:
                    m_old = mx_ref[p, nat, :]
                    m_new = jnp.maximum(m_old, mb)
                    a = jnp.exp(m_old - m_new)
                    b = jnp.exp(mb - m_new)
                    acc_ref[p, nat, :] = acc_ref[p, nat, :] * a + num * b
                    den_ref[p, nat, :] = den_ref[p, nat, :] * a + den * b
                    mx_ref[p, nat, :] = m_new
            return carry

        lax.fori_loop(0, L // tq, tile, 0, unroll=8)

    for gi in range(N_GROUPS):
        pl.when(pl.program_id(1) == gi)(functools.partial(group, gi))

    @pl.when(pl.program_id(1) == N_GROUPS - 1)
    def _():
        wag = wag_ref[...].astype(bf16)
        for r0 in range(0, L, AT_RC):
            rows = slice(r0, r0 + AT_RC)
            ag = _dot(xs_ref[0, 0, rows, :], wag) + bag_ref[...]
            o = jnp.concatenate([acc_ref[p, rows, :] / den_ref[p, rows, :] for p in range(npair)], axis=1)
            o_ref[0, rows, :] = (o * (ag * jax.nn.sigmoid(ag))).astype(o_ref.dtype)


def _attention(xs, w_in, b_in, gq, gk, hsum):
    _, B, L, _ = xs.shape
    assert PERM_DILATIONS == tuple(DILATED_GROUPS[g][1] for g in GROUP_ORDER) and PERM_DILATIONS[-1] == 1
    assert GROUP_ORDER == tuple(N_GROUPS - 1 - i for i in range(N_GROUPS))
    assert L % AT_RC == 0 and all(L % (Q_TILE * d) == 0 for _, d in DILATED_GROUPS)
    gw = HEADS_PER_GROUP * HEAD_DIM
    npair = HEADS_PER_GROUP // 2
    dists = [jnp.asarray(_attn_dist(L // d, d, window)).astype(bf16) for window, d in DILATED_GROUPS]
    sl = jnp.asarray(_slope_eye()).astype(bf16)
    agb = O_AGATE // gw

    def col(k):
        return lambda b, i: (0, (O_QKV + k * AT_QKV) // gw + (N_GROUPS - 1 - i))

    wspec = lambda k: pl.BlockSpec((D_MODEL, gw), col(k))
    bspec = lambda k: pl.BlockSpec((1, gw), col(k))
    vec = pl.BlockSpec((1, gw), lambda b, i: (0, 0))
    acc = pltpu.VMEM((npair, L, PAIR), f32)
    return pl.pallas_call(
        _attn_kernel,
        grid=(B, N_GROUPS),
        in_specs=[pl.BlockSpec((1, 1, L, D_MODEL), lambda b, i: (i, b, 0, 0)),
                  wspec(0), wspec(1), wspec(2), bspec(0), bspec(1), bspec(2),
                  _const_spec((D_MODEL, gw), lambda b, i: (0, agb)), pl.BlockSpec((1, gw), lambda b, i: (0, agb)),
                  vec, vec, _const_spec(hsum.shape, lambda b, i: (0, 0)), _const_spec(sl.shape, lambda b, i: (0, 0, 0))]
                 + [_const_spec(t.shape, lambda b, i: (0, 0, 0)) for t in dists],
        out_specs=pl.BlockSpec((1, L, gw), lambda b, i: (b, 0, 0)),
        out_shape=jax.ShapeDtypeStruct((B, L, gw), bf16),
        scratch_shapes=[pltpu.VMEM((L, gw), bf16), pltpu.VMEM((L, gw), bf16), pltpu.VMEM((L, gw), bf16),
                        acc, acc, acc],
        compiler_params=pltpu.CompilerParams(dimension_semantics=("arbitrary", "arbitrary"),
                                             vmem_limit_bytes=VMEM_LIMIT),
        name="dilated_attention",
    )(xs, w_in, w_in, w_in, b_in, b_in, b_in, w_in, b_in, gq, gk, hsum, sl, *dists)


def _final_kernel(x_ref, xn_ref, gh_ref, ga_ref, wg_ref, bg_ref, why_ref, wat_ref, wout_ref, out_ref,
                  wg_b, why_b, wat_b, wout_b):
    @pl.when(pl.program_id(0) == 0)
    def _():
        for src, dst in ((wg_ref, wg_b), (why_ref, why_b), (wat_ref, wat_b), (wout_ref, wout_b)):
            dst[...] = src[...].astype(bf16)

    gates = _dot(xn_ref[...], wg_b[...]) + bg_ref[...]
    u_h = _dot(gh_ref[...], why_b[...])
    u_a = _dot(ga_ref[...], wat_b[...])
    merged = jax.nn.sigmoid(gates[:, 0:D_MODEL]) * u_h + jax.nn.sigmoid(gates[:, D_MODEL:]) * u_a
    out_ref[...] = x_ref[...] + _dot(merged.astype(bf16), wout_b[...])


def _final(x2, xs2, gh2, ga2, w_in, b_in, why, wat, wout):
    rows = x2.shape[0]
    tm = 512
    mgw = 2 * D_MODEL
    rspec = lambda c: pl.BlockSpec((tm, c), lambda i: (i, 0))
    cspec = lambda a: _const_spec(a.shape, lambda i: (0, 0))
    return pl.pallas_call(
        _final_kernel,
        grid=(rows // tm,),
        in_specs=[rspec(D_MODEL), pl.BlockSpec((None, tm, D_MODEL), lambda i: (NAT, i, 0)), rspec(HY_WIDTH), rspec(AT_WIDTH),
                  _const_spec((D_MODEL, mgw), lambda i: (0, O_MG // mgw)),
                  pl.BlockSpec((1, mgw), lambda i: (0, O_MG // mgw)), cspec(why), cspec(wat), cspec(wout)],
        out_specs=rspec(D_MODEL),
        out_shape=jax.ShapeDtypeStruct((rows, D_MODEL), f32),
        scratch_shapes=[pltpu.VMEM((D_MODEL, mgw), bf16), pltpu.VMEM(why.shape, bf16), pltpu.VMEM(wat.shape, bf16),
                        pltpu.VMEM(wout.shape, bf16)],
        compiler_params=pltpu.CompilerParams(dimension_semantics=("arbitrary",),
                                             vmem_limit_bytes=VMEM_LIMIT),
        name="merge_output",
    )(x2, xs2, gh2, ga2, w_in, b_in, why, wat, wout)


def _layer(x, norm_g, w_in, b_in, conv_w, conv_b, hf_w1, hf_b1, hf_w2, hf_b2, hf_w3, hf_b3, hf_w4,
           hf_freq, hy_skip, q_norm_g, k_norm_g, w_hy_out, w_at_out, w_out):
    B, L, D = x.shape
    x2 = x.reshape(B * L, D)
    tab = jnp.asarray(_dft_tables(L)).astype(bf16)
    w_in = w_in.astype(f32)
    b_in2 = b_in.astype(f32).reshape(1, IN_COLS)

    xs = _prenorm(x, norm_g.astype(f32).reshape(1, D))

    kr, ki, ks = _filters(L, tab, hf_w1, hf_b1, hf_w2, hf_b2, hf_w3, hf_b3, hf_w4, hf_freq)
    gh = _hyena(xs, w_in, b_in2, conv_w.astype(f32), conv_b.astype(f32).reshape(1, -1),
                tab, kr, ki, ks, hy_skip.astype(f32))

    gq = jnp.tile(q_norm_g.astype(f32), HEADS_PER_GROUP).reshape(1, -1)
    gk = jnp.tile(k_norm_g.astype(f32), HEADS_PER_GROUP).reshape(1, -1)
    head = np.arange(MXU_DIM) // HEAD_DIM
    hsum = jnp.asarray((head[:, None] == head[None, :]).astype(np.float32)).astype(bf16)
    ga = _attention(xs, w_in, b_in2, gq, gk, hsum)

    out = _final(x2, xs.reshape(len(PERM_DILATIONS), B * L, D), gh.reshape(B * L, HY_WIDTH), ga.reshape(B * L, AT_WIDTH),
                 w_in, b_in2, w_hy_out.astype(f32), w_at_out.astype(f32), w_out.astype(f32))
    return out.reshape(B, L, D)


def kernel(x, norm_g, w_in, b_in, conv_w, conv_b, hf_w1, hf_b1, hf_w2, hf_b2, hf_w3, hf_b3, hf_w4,
           hf_freq, hy_skip, q_norm_g, k_norm_g, w_hy_out, w_at_out, w_out):
    depth = norm_g.shape[0]
    for i in range(depth):
        x = _layer(x, norm_g[i], w_in[i], b_in[i], conv_w[i], conv_b[i], hf_w1[i], hf_b1[i], hf_w2[i],
                   hf_b2[i], hf_w3[i], hf_b3[i], hf_w4[i], hf_freq[i], hy_skip[i], q_norm_g[i],
                   k_norm_g[i], w_hy_out[i], w_at_out[i], w_out[i])
    return x
```

```python
import functools
import math

import jax
import jax.numpy as jnp
import numpy as np
from jax import lax
from jax.experimental import pallas as pl
from jax.experimental.pallas import tpu as pltpu

D_MODEL = 1024
HY_WIDTH = 768
HY_ORDER = 2
HY_SHORT_CONV = 3
HY_EMB_DIM = 33
HY_FILTER_HIDDEN = 64
HY_FAST_DECAY = 0.3
HY_SLOW_DECAY = 1.5
HY_DECAY_TARGET = 1e-2
HY_MOD_SHIFT = 0.0
HEAD_DIM = 64
HEADS_PER_GROUP = 8
DILATED_GROUPS = ((128, 1), (512, 4), (2048, 16))
N_GROUPS = 3
AT_QKV = N_GROUPS * HEADS_PER_GROUP * HEAD_DIM
AT_WIDTH = HEADS_PER_GROUP * HEAD_DIM
NORM_EPS = 1e-6
NEG_INF = -1e30

O_HGATE = 3 * HY_WIDTH
O_QKV = O_HGATE + HY_WIDTH
O_AGATE = O_QKV + 3 * AT_QKV
O_MG = O_AGATE + AT_WIDTH
IN_COLS = O_MG + 2 * D_MODEL

LANES = 128
MXU_DIM = 256
VMEM_LIMIT = 56 * 1024 * 1024

HY_CT = 256
HY_RC = 512
HY_FC = 256
HY_PH = 4
RSQRT2 = math.sqrt(0.5)
HALO = 16
AT_RC = 512
EMB_PAD = 128
Q_TILE = 128
BAND_HALF = 64
GROUP_ORDER = (2, 1, 0)
PERM_DILATIONS = (16, 4, 1)
NAT = 2
PN_RC = 512
PAIR = 2 * HEAD_DIM

f32 = jnp.float32
bf16 = jnp.bfloat16


def _dot(a, b):
    return jnp.dot(a, b, preferred_element_type=f32)


def _const_spec(shape, index_map):
    return pl.BlockSpec(shape, index_map, pipeline_mode=pl.Buffered(1))


@functools.lru_cache(maxsize=None)
def _dft_tables(L):
    n = 2 * L
    f = np.arange(L // HY_PH, dtype=np.int64)[:, None]
    m = np.arange(L // HY_PH, dtype=np.int64)[None, :]
    fwd = []
    for p in range(HY_PH):
        ang = ((f * (HY_PH * m + p)) % n).astype(np.float64) * (2.0 * np.pi / n)
        fwd += [np.cos(ang), np.sin(ang)]
    return np.stack(fwd + [t.T for t in fwd]).astype(np.float32)


def _butterfly(A, B):
    e = (A[0] + A[2], A[0] - A[2], A[1] + A[3], A[1] - A[3])
    f = (B[0] + B[2], B[0] - B[2], B[1] + B[3], B[1] - B[3])
    return e, f


def _spectrum_cos(e, f):
    return (e[0] + e[2], e[1] - f[3], e[0] - e[2], e[1] + f[3])


def _spectrum_sin(e, f):
    return (f[0] + f[2], f[1] + e[3], f[2] - f[0], e[3] - f[1])


def _odd_bins(r):
    d, s = (r[1] - r[3]) * RSQRT2, (r[1] + r[3]) * RSQRT2
    return (r[0] + d, s + r[2]), (r[0] - d, s - r[2])


def _phase_major(a, L):
    return np.concatenate([a[p::HY_PH] for p in range(HY_PH)], axis=0)


@functools.lru_cache(maxsize=None)
def _filter_embedding(L):
    t = np.linspace(0.0, 1.0, L)[:, None]
    bands = (HY_EMB_DIM - 1) // 2
    w = 2.0 * np.pi * np.arange(L)[:, None] / L
    f = np.linspace(1e-4, bands - 1, bands)[None, :]
    z = np.concatenate([t, np.cos(f * w), -np.sin(f * w)], axis=-1)
    zp = np.zeros((L, EMB_PAD), np.float64)
    zp[:, :HY_EMB_DIM] = z
    return zp.astype(np.float32)


@functools.lru_cache(maxsize=None)
def _decay_rates():
    max_decay = math.log(HY_DECAY_TARGET) / HY_FAST_DECAY
    min_decay = math.log(HY_DECAY_TARGET) / HY_SLOW_DECAY
    return np.abs(np.linspace(min_decay, max_decay, HY_WIDTH))[None, :].astype(np.float32)


def _alibi_slope(h):
    return 2.0 ** (-8.0 * (h + 1) / HEADS_PER_GROUP)


@functools.lru_cache(maxsize=None)
def _attn_dist(n, dilation, window):
    half = window // (2 * dilation)
    assert half == BAND_HALF
    tq = min(Q_TILE, n)
    w = min(2 * Q_TILE, n)
    masked = NEG_INF / _alibi_slope(HEADS_PER_GROUP - 1)
    offs = sorted({q0 - min(max(q0 - half, 0), n - w) for q0 in range(0, n, tq)})
    assert offs == [BAND_HALF * i for i in range(len(offs))]
    out = np.zeros((len(offs), w, tq), np.float32)
    for ci, off in enumerate(offs):
        rel = np.arange(tq)[None, :] + off - np.arange(w)[:, None]
        out[ci] = np.where(np.abs(rel) <= half, -dilation * np.abs(rel), masked)
    return out


@functools.lru_cache(maxsize=None)
def _slope_eye():
    eye = np.eye(Q_TILE, dtype=np.float32)
    return np.stack([np.concatenate([_alibi_slope(2 * p) * eye, _alibi_slope(2 * p + 1) * eye], axis=0)
                     for p in range(HEADS_PER_GROUP // 2)])


@functools.lru_cache(maxsize=None)
def _residue_perm(n):
    p = np.zeros((n, n), np.float32)
    i = np.arange(n)
    p[(i % 4) * (n // 4) + i // 4, i] = 1.0
    return p


def _prenorm_kernel(x_ref, g_ref, perm_ref, o_ref, st_ref):
    L = x_ref.shape[1]
    nt = D_MODEL // LANES
    d4, d16 = PERM_DILATIONS.index(4), PERM_DILATIONS.index(16)
    n4, n16 = L // 4, L // 16
    for r0 in range(0, L, PN_RC):
        rows = slice(r0, r0 + PN_RC)
        x = x_ref[0, rows, :]
        ms = jnp.mean(x * x, axis=-1, keepdims=True)
        xn = x * lax.rsqrt(ms + NORM_EPS) * g_ref[...]
        o_ref[NAT, 0, rows, :] = xn.astype(o_ref.dtype)
        for c in range(nt):
            st_ref[c, rows, :] = xn[:, c * LANES:(c + 1) * LANES]

    def gather(r, carry):
        dst = pl.ds(pl.multiple_of(r * n4, n4), n4)
        for c in range(nt):
            o_ref[d4, 0, dst, c * LANES:(c + 1) * LANES] = st_ref[c, pl.ds(r, n4, stride=4), :].astype(o_ref.dtype)
        return carry

    lax.fori_loop(0, 4, gather, 0)
    for r4 in range(4):
        y = _dot(perm_ref[...], o_ref[d4, 0, r4 * n4:(r4 + 1) * n4, :]).astype(o_ref.dtype)
        for q in range(4):
            o_ref[d16, 0, (r4 + 4 * q) * n16:(r4 + 4 * q + 1) * n16, :] = y[q * n16:(q + 1) * n16]


def _prenorm(x, g):
    B, L, D = x.shape
    assert sorted(PERM_DILATIONS) == [1, 4, 16] and PERM_DILATIONS[NAT] == 1
    perm = jnp.asarray(_residue_perm(L // 4)).astype(bf16)
    return pl.pallas_call(
        _prenorm_kernel,
        grid=(B,),
        in_specs=[pl.BlockSpec((1, L, D), lambda b: (b, 0, 0)),
                  pl.BlockSpec((1, D), lambda b: (0, 0)),
                  _const_spec(perm.shape, lambda b: (0, 0))],
        out_specs=pl.BlockSpec((len(PERM_DILATIONS), 1, L, D), lambda b: (0, b, 0, 0)),
        out_shape=jax.ShapeDtypeStruct((len(PERM_DILATIONS), B, L, D), bf16),
        scratch_shapes=[pltpu.VMEM((D // LANES, L, LANES), f32)],
        compiler_params=pltpu.CompilerParams(dimension_semantics=("arbitrary",),
                                             vmem_limit_bytes=VMEM_LIMIT),
        name="prenorm",
    )(x, g, perm)


def _filters_kernel(z_ref, w1_ref, b1_ref, w2_ref, b2_ref, w3_ref, b3_ref, fr_ref, w4f_ref, w4b_ref,
                    t_ref, rate_ref, tab_ref, kr_ref, ki_ref, ks_ref, h3_ref):
    L = z_ref.shape[0]
    M = L // HY_PH
    hi = lax.Precision.HIGHEST

    @pl.when((pl.program_id(0) == 0) & (pl.program_id(1) == 0))
    def _():
        fr = fr_ref[...]
        h = jnp.sin(fr * (jnp.dot(z_ref[...], w1_ref[...], precision=hi, preferred_element_type=f32)
                          + b1_ref[...]))
        h = jnp.sin(fr * (jnp.dot(h, w2_ref[...], precision=hi, preferred_element_type=f32) + b2_ref[...]))
        h = jnp.sin(fr * (jnp.dot(h, w3_ref[...], precision=hi, preferred_element_type=f32) + b3_ref[...]))
        h3_ref[...] = h

    h3 = h3_ref[...]
    decay = jnp.exp(-t_ref[...] * rate_ref[...]) + HY_MOD_SHIFT
    h_hi = h3.astype(bf16)
    h_lo = (h3 - h_hi.astype(f32)).astype(bf16)

    def last_layer(w_ref):
        w = w_ref[...]
        w_hi = w.astype(bf16)
        w_lo = (w - w_hi.astype(f32)).astype(bf16)
        return _dot(h_hi, w_hi) + (_dot(h_lo, w_hi) + _dot(h_hi, w_lo))

    hf = last_layer(w4f_ref) * decay
    hb = last_layer(w4b_ref) * decay
    hb0 = hb[0:1, :]
    hs = hf + hb
    hd = hb - hf
    n = 2 * L
    row = lax.broadcasted_iota(jnp.int32, (M, HY_CT), 0)
    sgn = jnp.where((row & 1) == 1, -1.0, 1.0).astype(f32)

    def transform(x):
        xp = [x[p * M:(p + 1) * M] for p in range(HY_PH)]
        A = [_dot(tab_ref[2 * p], xp[p].astype(bf16)) for p in range(HY_PH)]
        B = [_dot(tab_ref[2 * p + 1], xp[p].astype(bf16)) for p in range(HY_PH)]
        r = [jnp.sum(xp[p] * sgn, axis=0, keepdims=True) for p in range(HY_PH)]
        return _butterfly(A, B), _odd_bins(r)

    (es, fs), odd_s = transform(hs)
    (ed, fd), odd_d = transform(hd)
    kr = _spectrum_cos(es, fs)
    ki = _spectrum_sin(ed, fd)
    two = 2.0 / n
    edge = jnp.where(row == 0, 1.0 / n, two).astype(f32)
    once = jnp.where(row == 0, 0.0, two).astype(f32)
    for cls, wgt in enumerate((edge, two, edge, once)):
        kr_ref[0, cls] = (kr[cls] - hb0) * wgt
        ki_ref[0, cls] = ki[cls] * wgt
    for j in range(2):
        ks_ref[0, 2 * j:2 * j + 1, :] = (odd_s[j][0] - hb0) * two
        ks_ref[0, 2 * j + 1:2 * j + 2, :] = odd_d[j][1] * two


def _filters(L, tab, w1, b1, w2, b2, w3, b3, w4, freq):
    M = L // HY_PH
    z = jnp.asarray(_phase_major(_filter_embedding(L), L))
    t = jnp.asarray(_phase_major(np.linspace(0.0, 1.0, L)[:, None].astype(np.float32), L))
    rate = jnp.asarray(_decay_rates())
    w1p = jnp.zeros((EMB_PAD, HY_FILTER_HIDDEN), f32).at[:HY_EMB_DIM].set(w1.astype(f32))
    nct = HY_WIDTH // HY_CT
    row = lambda a: a.astype(f32).reshape(1, -1)
    full = lambda shape: pl.BlockSpec(shape, lambda o, j: (0,) * len(shape))
    H = HY_FILTER_HIDDEN
    kspec = pl.BlockSpec((1, HY_PH, M, HY_CT), lambda o, j: (o, 0, 0, j))
    kshape = jax.ShapeDtypeStruct((HY_ORDER, HY_PH, M, HY_WIDTH), f32)
    return pl.pallas_call(
        _filters_kernel,
        grid=(HY_ORDER, nct),
        in_specs=[full((L, EMB_PAD)), full((EMB_PAD, H)), full((1, H)), full((H, H)), full((1, H)),
                  full((H, H)), full((1, H)), full((1, H)),
                  pl.BlockSpec((H, HY_CT), lambda o, j: (0, 2 * nct * o + j)),
                  pl.BlockSpec((H, HY_CT), lambda o, j: (0, 2 * nct * o + nct + j)),
                  full((L, 1)),
                  pl.BlockSpec((1, HY_CT), lambda o, j: (0, j)),
                  _const_spec(tab.shape, lambda o, j: (0, 0, 0))],
        out_specs=[kspec, kspec, pl.BlockSpec((1, 4, HY_CT), lambda o, j: (o, 0, j))],
        out_shape=[kshape, kshape, jax.ShapeDtypeStruct((HY_ORDER, 4, HY_WIDTH), f32)],
        scratch_shapes=[pltpu.VMEM((L, H), f32)],
        compiler_params=pltpu.CompilerParams(dimension_semantics=("arbitrary", "arbitrary"),
                                             vmem_limit_bytes=VMEM_LIMIT),
        name="hyena_filters",
    )(z, w1p, row(b1), w2.astype(f32), row(b2), w3.astype(f32), row(b3), row(freq),
      w4.astype(f32), w4.astype(f32), t, rate, tab)


def _hyena_kernel(xn_ref, wv_ref, wx1_ref, wx2_ref, wg_ref, bv_ref, bx1_ref, bx2_ref, bg_ref,
                  cwv_ref, cwx1_ref, cwx2_ref, cbv_ref, cbx1_ref, cbx2_ref,
                  tab_ref, kr_ref, ki_ref, ks_ref, skip_ref, o_ref,
                  z_ref, u_ref, ub_ref, pq_ref, x1_ref, x2_ref, g_ref):
    L = xn_ref.shape[1]
    M = L // HY_PH
    nlt = HY_CT // LANES
    row_chunks = [slice(r, r + HY_RC) for r in range(0, L, HY_RC)]
    chunks = [slice(r, r + HY_RC) for r in range(0, M, HY_RC)]
    fchunks = [slice(r, r + HY_FC) for r in range(0, M, HY_FC)]
    row = lax.broadcasted_iota(jnp.int32, (HY_RC, HY_CT), 0)
    sgn = jnp.where((row & 1) == 1, -1.0, 1.0).astype(f32)
    for k in range(z_ref.shape[0]):
        for lt in range(nlt):
            z_ref[k, lt, 0:HALO] = jnp.zeros((HALO, LANES), f32)
            z_ref[k, lt, L + HALO:L + 2 * HALO] = jnp.zeros((HALO, LANES), f32)

    def proj_conv(k, dst_ref, w_ref, b_ref, cw_ref, cb_ref):
        w = w_ref[...].astype(bf16)
        for c in row_chunks:
            val = _dot(xn_ref[0, c, :], w) + b_ref[...]
            for lt in range(nlt):
                z_ref[k, lt, HALO + c.start:HALO + c.stop, :] = val[:, lt * LANES:(lt + 1) * LANES]

        def phase_rows(p, c, shift=0):
            src = pl.ds(HALO + p + shift + HY_PH * c.start, HY_RC, stride=HY_PH)
            return jnp.concatenate([z_ref[k, lt, src, :] for lt in range(nlt)], axis=1)

        for p in range(HY_PH):
            for c in chunks:
                dst_ref[p, c] = (cb_ref[...] + phase_rows(p, c, -1) * cw_ref[0:1, :]
                                 + phase_rows(p, c) * cw_ref[1:2, :] + phase_rows(p, c, 1) * cw_ref[2:3, :])

    def long_conv(o, x_ref):
        r = [jnp.zeros((1, HY_CT), f32) for _ in range(HY_PH)]
        for p in range(HY_PH):
            for c in chunks:
                u = u_ref[p, c]
                ub_ref[p, c] = u.astype(bf16)
                r[p] = r[p] + jnp.sum(u * sgn, axis=0, keepdims=True)
        for c in fchunks:
            A = [_dot(tab_ref[2 * p, c, :], ub_ref[p]) for p in range(HY_PH)]
            B = [_dot(tab_ref[2 * p + 1, c, :], ub_ref[p]) for p in range(HY_PH)]
            e, f = _butterfly(A, B)
            a, b = _spectrum_cos(e, f), _spectrum_sin(e, f)
            P, Q = [], []
            for cls in range(HY_PH):
                kr, ki = kr_ref[o, cls, c, :], ki_ref[o, cls, c, :]
                P.append(a[cls] * kr + b[cls] * ki)
                Q.append(b[cls] * kr - a[cls] * ki)
            g = (P[0] + P[2], P[0] - P[2], P[1] + P[3], P[3] - P[1])
            h = (Q[0] - Q[2], Q[0] + Q[2], Q[1] - Q[3], Q[1] + Q[3])
            X = (g[0] + g[2], g[1] + h[3], g[0] - g[2], g[1] - h[3])
            Y = (h[0] + h[2], h[1] + g[3], h[0] - h[2], h[1] - g[3])
            for p in range(HY_PH):
                pq_ref[2 * p, c] = X[p].astype(bf16)
                pq_ref[2 * p + 1, c] = Y[p].astype(bf16)
        pq_odd = []
        for j, (a_o, b_o) in enumerate(_odd_bins(r)):
            kr, ki = ks_ref[o, 2 * j:2 * j + 1, :], ks_ref[o, 2 * j + 1:2 * j + 2, :]
            pq_odd.append((a_o * kr + b_o * ki, b_o * kr - a_o * ki))
        (p1, q1), (p3, q3) = pq_odd
        odd = (p1 + p3, (p1 + q1 - p3 + q3) * RSQRT2, q1 - q3, (q1 - p1 + p3 + q3) * RSQRT2)
        skip = skip_ref[o:o + 1, :]
        nt = 2 * HY_PH
        for p in range(HY_PH):
            for c in chunks:
                y = _dot(tab_ref[nt + 2 * p, c, :], pq_ref[2 * p]) + _dot(tab_ref[nt + 2 * p + 1, c, :], pq_ref[2 * p + 1])
                u_ref[p, c] = x_ref[p, c] * (y + sgn * odd[p] + u_ref[p, c] * skip)

    proj_conv(0, u_ref, wv_ref, bv_ref, cwv_ref, cbv_ref)
    proj_conv(1, x1_ref, wx1_ref, bx1_ref, cwx1_ref, cbx1_ref)
    proj_conv(2, x2_ref, wx2_ref, bx2_ref, cwx2_ref, cbx2_ref)
    wg = wg_ref[...].astype(bf16)
    for c in row_chunks:
        g = _dot(xn_ref[0, c, :], wg) + bg_ref[...]
        g_ref[c] = g * jax.nn.sigmoid(g)
    long_conv(0, x1_ref)
    long_conv(1, x2_ref)
    for p in range(HY_PH):
        for c in chunks:
            dst = pl.ds(HALO + p + HY_PH * c.start, HY_RC, stride=HY_PH)
            y = u_ref[p, c]
            for lt in range(nlt):
                z_ref[0, lt, dst, :] = y[:, lt * LANES:(lt + 1) * LANES]
    for c in row_chunks:
        y = jnp.concatenate([z_ref[0, lt, HALO + c.start:HALO + c.stop, :] for lt in range(nlt)], axis=1)
        o_ref[0, c, :] = (y * g_ref[c]).astype(o_ref.dtype)


def _hyena(xs, w_in, b_in, conv_w, conv_b, tab, kr, ki, ks, skip):
    _, B, L, _ = xs.shape
    M = L // HY_PH
    nct = HY_WIDTH // HY_CT
    hg = O_HGATE // HY_CT

    def col(k):
        return lambda j, b: (0, k * nct + j)

    wspec = lambda k: _const_spec((D_MODEL, HY_CT), col(k))
    bspec = lambda k: pl.BlockSpec((1, HY_CT), col(k))
    cwspec = lambda k: pl.BlockSpec((HY_SHORT_CONV, HY_CT), col(k))
    kspec = _const_spec((HY_ORDER, HY_PH, M, HY_CT), lambda j, b: (0, 0, 0, j))
    return pl.pallas_call(
        _hyena_kernel,
        grid=(nct, B),
        in_specs=[pl.BlockSpec((None, 1, L, D_MODEL), lambda j, b: (NAT, b, 0, 0)),
                  wspec(0), wspec(1), wspec(2), _const_spec((D_MODEL, HY_CT), lambda j, b: (0, hg + j)),
                  bspec(0), bspec(1), bspec(2), pl.BlockSpec((1, HY_CT), lambda j, b: (0, hg + j)),
                  cwspec(0), cwspec(1), cwspec(2), bspec(0), bspec(1), bspec(2),
                  _const_spec(tab.shape, lambda j, b: (0, 0, 0)),
                  kspec, kspec,
                  pl.BlockSpec((HY_ORDER, 4, HY_CT), lambda j, b: (0, 0, j)),
                  pl.BlockSpec((HY_ORDER, HY_CT), lambda j, b: (0, j))],
        out_specs=pl.BlockSpec((1, L, HY_CT), lambda j, b: (b, 0, j)),
        out_shape=jax.ShapeDtypeStruct((B, L, HY_WIDTH), bf16),
        scratch_shapes=[pltpu.VMEM((3, HY_CT // LANES, L + 2 * HALO, LANES), f32), pltpu.VMEM((HY_PH, M, HY_CT), f32),
                        pltpu.VMEM((HY_PH, M, HY_CT), bf16), pltpu.VMEM((2 * HY_PH, M, HY_CT), bf16),
                        pltpu.VMEM((HY_PH, M, HY_CT), f32), pltpu.VMEM((HY_PH, M, HY_CT), f32),
                        pltpu.VMEM((L, HY_CT), f32)],
        compiler_params=pltpu.CompilerParams(dimension_semantics=("arbitrary", "arbitrary"),
                                             vmem_limit_bytes=VMEM_LIMIT),
        name="hyena_mixer",
    )(xs, w_in, w_in, w_in, w_in, b_in, b_in, b_in, b_in,
      conv_w, conv_w, conv_w, conv_b, conv_b, conv_b, tab, kr, ki, ks, skip)


def _attn_kernel(xs_ref, wq_ref, wk_ref, wv_ref, bq_ref, bk_ref, bv_ref, wag_ref, bag_ref,
                 gq_ref, gk_ref, hsum_ref, sl_ref, d0_ref, d1_ref, d2_ref, o_ref,
                 qs_ref, ks_ref, vs_ref, acc_ref, mx_ref, den_ref):
    L = xs_ref.shape[2]
    gw = HEADS_PER_GROUP * HEAD_DIM
    npair = HEADS_PER_GROUP // 2
    dist_refs = (d0_ref, d1_ref, d2_ref)
    tq = Q_TILE
    first = lax.broadcasted_iota(jnp.int32, (tq, PAIR), 1) < HEAD_DIM
    nt_dims = (((1,), (1,)), ((), ()))

    def normed(x, w, b_ref, g_ref):
        z = _dot(x, w) + b_ref[...]
        z2 = (z * z).astype(bf16)
        ssq = jnp.concatenate([_dot(z2[:, c:c + MXU_DIM], hsum_ref[...]) for c in range(0, gw, MXU_DIM)], axis=1)
        return z * lax.rsqrt(ssq * (1.0 / HEAD_DIM) + NORM_EPS) * g_ref[...]

    def group(gi):
        _, d = DILATED_GROUPS[GROUP_ORDER[gi]]
        n = L // d
        w = min(2 * tq, n)
        per_class = n // tq
        dist_ref = dist_refs[GROUP_ORDER[gi]]

        wq, wk, wv = (r[...].astype(bf16) for r in (wq_ref, wk_ref, wv_ref))
        for r0 in range(0, L, AT_RC):
            rows = slice(r0, r0 + AT_RC)
            x = xs_ref[0, 0, rows, :]
            qs_ref[rows] = (normed(x, wq, bq_ref, gq_ref) * (HEAD_DIM ** -0.5)).astype(bf16)
            ks_ref[rows] = normed(x, wk, bk_ref, gk_ref).astype(bf16)
            vs_ref[rows] = (_dot(x, wv) + bv_ref[...]).astype(bf16)

        def tile(idx, carry):
            r = idx // per_class
            t = idx % per_class
            q0 = pl.multiple_of(idx * tq, tq)
            koff = jnp.clip(t * tq - BAND_HALF, 0, n - w)
            dist_t = dist_ref[(t * tq - koff) // BAND_HALF]
            k0 = pl.multiple_of(r * n + koff, BAND_HALF)
            nat = pl.ds(t * tq * d + r, tq, stride=d) if d > 1 else pl.ds(q0, tq)
            for p in range(npair):
                pc = slice(p * PAIR, (p + 1) * PAIR)
                q = qs_ref[pl.ds(q0, tq), pc]
                zero = jnp.zeros_like(q)
                qq = jnp.concatenate([jnp.where(first, q, zero), jnp.where(first, zero, q)], axis=0)
                lhs = jnp.concatenate([qq, sl_ref[p]], axis=1)
                rhs_t = jnp.concatenate([ks_ref[pl.ds(k0, w), pc], dist_t], axis=1)
                s = lax.dot_general(lhs, rhs_t, nt_dims, preferred_element_type=f32)
                m = jnp.max(s, axis=-1, keepdims=True)
                pr = jnp.exp(s - m).astype(bf16)
                rhs = jnp.concatenate([vs_ref[pl.ds(k0, w), pc], jnp.ones((w, PAIR), bf16)], axis=1)
                ov = _dot(pr, rhs)
                num = jnp.where(first, ov[0:tq, 0:PAIR], ov[tq:2 * tq, 0:PAIR])
                den = jnp.where(first, ov[0:tq, PAIR:2 * PAIR], ov[tq:2 * tq, PAIR:2 * PAIR])
                mb = jnp.where(first, m[0:tq], m[tq:2 * tq])
                if gi == 0:
                    acc_ref[p, nat, :] = num
                    den_ref[p, nat, :] = den
                    mx_ref[p, nat, :] = mb
                else:
                    m_old = mx_ref[p, nat, :]
                    m_new = jnp.maximum(m_old, mb)
                    a = jnp.exp(m_old - m_new)
                    b = jnp.exp(mb - m_new)
                    acc_ref[p, nat, :] = acc_ref[p, nat, :] * a + num * b
                    den_ref[p, nat, :] = den_ref[p, nat, :] * a + den * b
                    mx_ref[p, nat, :] = m_new
            return carry

        lax.fori_loop(0, L // tq, tile, 0, unroll=8)

    for gi in range(N_GROUPS):
        pl.when(pl.program_id(1) == gi)(functools.partial(group, gi))

    @pl.when(pl.program_id(1) == N_GROUPS - 1)
    def _():
        wag = wag_ref[...].astype(bf16)
        for r0 in range(0, L, AT_RC):
            rows = slice(r0, r0 + AT_RC)
            ag = _dot(xs_ref[0, 0, rows, :], wag) + bag_ref[...]
            o = jnp.concatenate([acc_ref[p, rows, :] / den_ref[p, rows, :] for p in range(npair)], axis=1)
            o_ref[0, rows, :] = (o * (ag * jax.nn.sigmoid(ag))).astype(o_ref.dtype)


def _attention(xs, w_in, b_in, gq, gk, hsum):
    _, B, L, _ = xs.shape
    assert PERM_DILATIONS == tuple(DILATED_GROUPS[g][1] for g in GROUP_ORDER) and PERM_DILATIONS[-1] == 1
    assert GROUP_ORDER == tuple(N_GROUPS - 1 - i for i in range(N_GROUPS))
    assert L % AT_RC == 0 and all(L % (Q_TILE * d) == 0 for _, d in DILATED_GROUPS)
    gw = HEADS_PER_GROUP * HEAD_DIM
    npair = HEADS_PER_GROUP // 2
    dists = [jnp.asarray(_attn_dist(L // d, d, window)).astype(bf16) for window, d in DILATED_GROUPS]
    sl = jnp.asarray(_slope_eye()).astype(bf16)
    agb = O_AGATE // gw

    def col(k):
        return lambda b, i: (0, (O_QKV + k * AT_QKV) // gw + (N_GROUPS - 1 - i))

    wspec = lambda k: pl.BlockSpec((D_MODEL, gw), col(k))
    bspec = lambda k: pl.BlockSpec((1, gw), col(k))
    vec = pl.BlockSpec((1, gw), lambda b, i: (0, 0))
    acc = pltpu.VMEM((npair, L, PAIR), f32)
    return pl.pallas_call(
        _attn_kernel,
        grid=(B, N_GROUPS),
        in_specs=[pl.BlockSpec((1, 1, L, D_MODEL), lambda b, i: (i, b, 0, 0)),
                  wspec(0), wspec(1), wspec(2), bspec(0), bspec(1), bspec(2),
                  _const_spec((D_MODEL, gw), lambda b, i: (0, agb)), pl.BlockSpec((1, gw), lambda b, i: (0, agb)),
                  vec, vec, _const_spec(hsum.shape, lambda b, i: (0, 0)), _const_spec(sl.shape, lambda b, i: (0, 0, 0))]
                 + [_const_spec(t.shape, lambda b, i: (0, 0, 0)) for t in dists],
        out_specs=pl.BlockSpec((1, L, gw), lambda b, i: (b, 0, 0)),
        out_shape=jax.ShapeDtypeStruct((B, L, gw), bf16),
        scratch_shapes=[pltpu.VMEM((L, gw), bf16), pltpu.VMEM((L, gw), bf16), pltpu.VMEM((L, gw), bf16),
                        acc, acc, acc],
        compiler_params=pltpu.CompilerParams(dimension_semantics=("arbitrary", "arbitrary"),
                                             vmem_limit_bytes=VMEM_LIMIT),
        name="dilated_attention",
    )(xs, w_in, w_in, w_in, b_in, b_in, b_in, w_in, b_in, gq, gk, hsum, sl, *dists)


def _final_kernel(x_ref, xn_ref, gh_ref, ga_ref, wg_ref, bg_ref, why_ref, wat_ref, wout_ref, out_ref,
                  wg_b, why_b, wat_b, wout_b):
    @pl.when(pl.program_id(0) == 0)
    def _():
        for src, dst in ((wg_ref, wg_b), (why_ref, why_b), (wat_ref, wat_b), (wout_ref, wout_b)):
            dst[...] = src[...].astype(bf16)

    gates = _dot(xn_ref[...], wg_b[...]) + bg_ref[...]
    u_h = _dot(gh_ref[...], why_b[...])
    u_a = _dot(ga_ref[...], wat_b[...])
    merged = jax.nn.sigmoid(gates[:, 0:D_MODEL]) * u_h + jax.nn.sigmoid(gates[:, D_MODEL:]) * u_a
    out_ref[...] = x_ref[...] + _dot(merged.astype(bf16), wout_b[...])


def _final(x2, xs2, gh2, ga2, w_in, b_in, why, wat, wout):
    rows = x2.shape[0]
    tm = 512
    mgw = 2 * D_MODEL
    rspec = lambda c: pl.BlockSpec((tm, c), lambda i: (i, 0))
    cspec = lambda a: _const_spec(a.shape, lambda i: (0, 0))
    return pl.pallas_call(
        _final_kernel,
        grid=(rows // tm,),
        in_specs=[rspec(D_MODEL), pl.BlockSpec((None, tm, D_MODEL), lambda i: (NAT, i, 0)), rspec(HY_WIDTH), rspec(AT_WIDTH),
                  _const_spec((D_MODEL, mgw), lambda i: (0, O_MG // mgw)),
                  pl.BlockSpec((1, mgw), lambda i: (0, O_MG // mgw)), cspec(why), cspec(wat), cspec(wout)],
        out_specs=rspec(D_MODEL),
        out_shape=jax.ShapeDtypeStruct((rows, D_MODEL), f32),
        scratch_shapes=[pltpu.VMEM((D_MODEL, mgw), bf16), pltpu.VMEM(why.shape, bf16), pltpu.VMEM(wat.shape, bf16),
                        pltpu.VMEM(wout.shape, bf16)],
        compiler_params=pltpu.CompilerParams(dimension_semantics=("arbitrary",),
                                             vmem_limit_bytes=VMEM_LIMIT),
        name="merge_output",
    )(x2, xs2, gh2, ga2, w_in, b_in, why, wat, wout)


def _layer(x, norm_g, w_in, b_in, conv_w, conv_b, hf_w1, hf_b1, hf_w2, hf_b2, hf_w3, hf_b3, hf_w4,
           hf_freq, hy_skip, q_norm_g, k_norm_g, w_hy_out, w_at_out, w_out):
    B, L, D = x.shape
    x2 = x.reshape(B * L, D)
    tab = jnp.asarray(_dft_tables(L)).astype(bf16)
    w_in = w_in.astype(f32)
    b_in2 = b_in.astype(f32).reshape(1, IN_COLS)

    xs = _prenorm(x, norm_g.astype(f32).reshape(1, D))

    kr, ki, ks = _filters(L, tab, hf_w1, hf_b1, hf_w2, hf_b2, hf_w3, hf_b3, hf_w4, hf_freq)
    gh = _hyena(xs, w_in, b_in2, conv_w.astype(f32), conv_b.astype(f32).reshape(1, -1),
                tab, kr, ki, ks, hy_skip.astype(f32))

    gq = jnp.tile(q_norm_g.astype(f32), HEADS_PER_GROUP).reshape(1, -1)
    gk = jnp.tile(k_norm_g.astype(f32), HEADS_PER_GROUP).reshape(1, -1)
    head = np.arange(MXU_DIM) // HEAD_DIM
    hsum = jnp.asarray((head[:, None] == head[None, :]).astype(np.float32)).astype(bf16)
    ga = _attention(xs, w_in, b_in2, gq, gk, hsum)

    out = _final(x2, xs.reshape(len(PERM_DILATIONS), B * L, D), gh.reshape(B * L, HY_WIDTH), ga.reshape(B * L, AT_WIDTH),
                 w_in, b_in2, w_hy_out.astype(f32), w_at_out.astype(f32), w_out.astype(f32))
    return out.reshape(B, L, D)


def kernel(x, norm_g, w_in, b_in, conv_w, conv_b, hf_w1, hf_b1, hf_w2, hf_b2, hf_w3, hf_b3, hf_w4,
           hf_freq, hy_skip, q_norm_g, k_norm_g, w_hy_out, w_at_out, w_out):
    depth = norm_g.shape[0]
    for i in range(depth):
        x = _layer(x, norm_g[i], w_in[i], b_in[i], conv_w[i], conv_b[i], hf_w1[i], hf_b1[i], hf_w2[i],
                   hf_b2[i], hf_w3[i], hf_b3[i], hf_w4[i], hf_freq[i], hy_skip[i], q_norm_g[i],
                   k_norm_g[i], w_hy_out[i], w_at_out[i], w_out[i])
    return x
```

```python
import functools
import math

import jax
import jax.numpy as jnp
import numpy as np
from jax import lax
from jax.experimental import pallas as pl
from jax.experimental.pallas import tpu as pltpu

D_MODEL = 1024
HY_WIDTH = 768
HY_ORDER = 2
HY_SHORT_CONV = 3
HY_EMB_DIM = 33
HY_FILTER_HIDDEN = 64
HY_FAST_DECAY = 0.3
HY_SLOW_DECAY = 1.5
HY_DECAY_TARGET = 1e-2
HY_MOD_SHIFT = 0.0
HEAD_DIM = 64
HEADS_PER_GROUP = 8
DILATED_GROUPS = ((128, 1), (512, 4), (2048, 16))
N_GROUPS = 3
AT_QKV = N_GROUPS * HEADS_PER_GROUP * HEAD_DIM
AT_WIDTH = HEADS_PER_GROUP * HEAD_DIM
NORM_EPS = 1e-6
NEG_INF = -1e30

O_HGATE = 3 * HY_WIDTH
O_QKV = O_HGATE + HY_WIDTH
O_AGATE = O_QKV + 3 * AT_QKV
O_MG = O_AGATE + AT_WIDTH
IN_COLS = O_MG + 2 * D_MODEL

LANES = 128
MXU_DIM = 256
VMEM_LIMIT = 56 * 1024 * 1024

HY_CT = 256
HY_RC = 512
HY_FC = 256
HY_PH = 4
RSQRT2 = math.sqrt(0.5)
HALO = 16
AT_RC = 512
EMB_PAD = 128
Q_TILE = 128
BAND_HALF = 64
GROUP_ORDER = (2, 1, 0)
PERM_DILATIONS = (16, 4, 1)
NAT = 2
PN_RC = 512
PAIR = 2 * HEAD_DIM

f32 = jnp.float32
bf16 = jnp.bfloat16


def _dot(a, b):
    return jnp.dot(a, b, preferred_element_type=f32)


def _const_spec(shape, index_map):
    return pl.BlockSpec(shape, index_map, pipeline_mode=pl.Buffered(1))


@functools.lru_cache(maxsize=None)
def _dft_tables(L):
    n = 2 * L
    f = np.arange(L // HY_PH, dtype=np.int64)[:, None]
    m = np.arange(L // HY_PH, dtype=np.int64)[None, :]
    fwd = []
    for p in range(HY_PH):
        ang = ((f * (HY_PH * m + p)) % n).astype(np.float64) * (2.0 * np.pi / n)
        fwd += [np.cos(ang), np.sin(ang)]
    return np.stack(fwd + [t.T for t in fwd]).astype(np.float32)


def _butterfly(A, B):
    e = (A[0] + A[2], A[0] - A[2], A[1] + A[3], A[1] - A[3])
    f = (B[0] + B[2], B[0] - B[2], B[1] + B[3], B[1] - B[3])
    return e, f


def _spectrum_cos(e, f):
    return (e[0] + e[2], e[1] - f[3], e[0] - e[2], e[1] + f[3])


def _spectrum_sin(e, f):
    return (f[0] + f[2], f[1] + e[3], f[2] - f[0], e[3] - f[1])


def _odd_bins(r):
    d, s = (r[1] - r[3]) * RSQRT2, (r[1] + r[3]) * RSQRT2
    return (r[0] + d, s + r[2]), (r[0] - d, s - r[2])


def _phase_major(a, L):
    return np.concatenate([a[p::HY_PH] for p in range(HY_PH)], axis=0)


@functools.lru_cache(maxsize=None)
def _filter_embedding(L):
    t = np.linspace(0.0, 1.0, L)[:, None]
    bands = (HY_EMB_DIM - 1) // 2
    w = 2.0 * np.pi * np.arange(L)[:, None] / L
    f = np.linspace(1e-4, bands - 1, bands)[None, :]
    z = np.concatenate([t, np.cos(f * w), -np.sin(f * w)], axis=-1)
    zp = np.zeros((L, EMB_PAD), np.float64)
    zp[:, :HY_EMB_DIM] = z
    return zp.astype(np.float32)


@functools.lru_cache(maxsize=None)
def _decay_rates():
    max_decay = math.log(HY_DECAY_TARGET) / HY_FAST_DECAY
    min_decay = math.log(HY_DECAY_TARGET) / HY_SLOW_DECAY
    return np.abs(np.linspace(min_decay, max_decay, HY_WIDTH))[None, :].astype(np.float32)


def _alibi_slope(h):
    return 2.0 ** (-8.0 * (h + 1) / HEADS_PER_GROUP)


@functools.lru_cache(maxsize=None)
def _attn_dist(n, dilation, window):
    half = window // (2 * dilation)
    assert half == BAND_HALF
    tq = min(Q_TILE, n)
    w = min(2 * Q_TILE, n)
    masked = NEG_INF / _alibi_slope(HEADS_PER_GROUP - 1)
    offs = sorted({q0 - min(max(q0 - half, 0), n - w) for q0 in range(0, n, tq)})
    assert offs == [BAND_HALF * i for i in range(len(offs))]
    out = np.zeros((len(offs), w, tq), np.float32)
    for ci, off in enumerate(offs):
        rel = np.arange(tq)[None, :] + off - np.arange(w)[:, None]
        out[ci] = np.where(np.abs(rel) <= half, -dilation * np.abs(rel), masked)
    return out


@functools.lru_cache(maxsize=None)
def _slope_eye():
    eye = np.eye(Q_TILE, dtype=np.float32)
    return np.stack([np.concatenate([_alibi_slope(2 * p) * eye, _alibi_slope(2 * p + 1) * eye], axis=0)
                     for p in range(HEADS_PER_GROUP // 2)])


@functools.lru_cache(maxsize=None)
def _residue_perm(n):
    p = np.zeros((n, n), np.float32)
    i = np.arange(n)
    p[(i % 4) * (n // 4) + i // 4, i] = 1.0
    return p


def _prenorm_kernel(x_ref, g_ref, perm_ref, o_ref, st_ref):
    L = x_ref.shape[1]
    nt = D_MODEL // LANES
    d4, d16 = PERM_DILATIONS.index(4), PERM_DILATIONS.index(16)
    n4, n16 = L // 4, L // 16
    for r0 in range(0, L, PN_RC):
        rows = slice(r0, r0 + PN_RC)
        x = x_ref[0, rows, :]
        ms = jnp.mean(x * x, axis=-1, keepdims=True)
        xn = x * lax.rsqrt(ms + NORM_EPS) * g_ref[...]
        o_ref[NAT, 0, rows, :] = xn.astype(o_ref.dtype)
        for c in range(nt):
            st_ref[c, rows, :] = xn[:, c * LANES:(c + 1) * LANES]

    def gather(r, carry):
        dst = pl.ds(pl.multiple_of(r * n4, n4), n4)
        for c in range(nt):
            o_ref[d4, 0, dst, c * LANES:(c + 1) * LANES] = st_ref[c, pl.ds(r, n4, stride=4), :].astype(o_ref.dtype)
        return carry

    lax.fori_loop(0, 4, gather, 0)
    for r4 in range(4):
        y = _dot(perm_ref[...], o_ref[d4, 0, r4 * n4:(r4 + 1) * n4, :]).astype(o_ref.dtype)
        for q in range(4):
            o_ref[d16, 0, (r4 + 4 * q) * n16:(r4 + 4 * q + 1) * n16, :] = y[q * n16:(q + 1) * n16]


def _prenorm(x, g):
    B, L, D = x.shape
    assert sorted(PERM_DILATIONS) == [1, 4, 16] and PERM_DILATIONS[NAT] == 1
    perm = jnp.asarray(_residue_perm(L // 4)).astype(bf16)
    return pl.pallas_call(
        _prenorm_kernel,
        grid=(B,),
        in_specs=[pl.BlockSpec((1, L, D), lambda b: (b, 0, 0)),
                  pl.BlockSpec((1, D), lambda b: (0, 0)),
                  _const_spec(perm.shape, lambda b: (0, 0))],
        out_specs=pl.BlockSpec((len(PERM_DILATIONS), 1, L, D), lambda b: (0, b, 0, 0)),
        out_shape=jax.ShapeDtypeStruct((len(PERM_DILATIONS), B, L, D), bf16),
        scratch_shapes=[pltpu.VMEM((D // LANES, L, LANES), f32)],
        compiler_params=pltpu.CompilerParams(dimension_semantics=("arbitrary",),
                                             vmem_limit_bytes=VMEM_LIMIT),
        name="prenorm",
    )(x, g, perm)


def _filters_kernel(z_ref, w1_ref, b1_ref, w2_ref, b2_ref, w3_ref, b3_ref, fr_ref, w4f_ref, w4b_ref,
                    t_ref, rate_ref, tab_ref, kr_ref, ki_ref, ks_ref, h3_ref):
    L = z_ref.shape[1]
    M = L // HY_PH
    hi = lax.Precision.HIGHEST

    @pl.when((pl.program_id(0) == 0) & (pl.program_id(1) == 0))
    def _():
        fr = fr_ref[...]
        h = jnp.sin(fr * (jnp.dot(w1_ref[...], z_ref[...], precision=hi, preferred_element_type=f32)
                          + b1_ref[...]))
        h = jnp.sin(fr * (jnp.dot(w2_ref[...], h, precision=hi, preferred_element_type=f32) + b2_ref[...]))
        h = jnp.sin(fr * (jnp.dot(w3_ref[...], h, precision=hi, preferred_element_type=f32) + b3_ref[...]))
        h3_ref[...] = h.T

    h3 = h3_ref[...]
    decay = jnp.exp(-t_ref[...] * rate_ref[...]) + HY_MOD_SHIFT
    h_hi = h3.astype(bf16)
    h_lo = (h3 - h_hi.astype(f32)).astype(bf16)

    def last_layer(w_ref):
        w = w_ref[...]
        w_hi = w.astype(bf16)
        w_lo = (w - w_hi.astype(f32)).astype(bf16)
        return _dot(h_hi, w_hi) + (_dot(h_lo, w_hi) + _dot(h_hi, w_lo))

    hf = last_layer(w4f_ref) * decay
    hb = last_layer(w4b_ref) * decay
    hb0 = hb[0:1, :]
    hs = hf + hb
    hd = hb - hf
    n = 2 * L
    row = lax.broadcasted_iota(jnp.int32, (M, HY_CT), 0)
    sgn = jnp.where((row & 1) == 1, -1.0, 1.0).astype(f32)

    def transform(x):
        xp = [x[p * M:(p + 1) * M] for p in range(HY_PH)]
        A = [_dot(tab_ref[2 * p], xp[p].astype(bf16)) for p in range(HY_PH)]
        B = [_dot(tab_ref[2 * p + 1], xp[p].astype(bf16)) for p in range(HY_PH)]
        r = [jnp.sum(xp[p] * sgn, axis=0, keepdims=True) for p in range(HY_PH)]
        return _butterfly(A, B), _odd_bins(r)

    (es, fs), odd_s = transform(hs)
    (ed, fd), odd_d = transform(hd)
    kr = _spectrum_cos(es, fs)
    ki = _spectrum_sin(ed, fd)
    two = 2.0 / n
    edge = jnp.where(row == 0, 1.0 / n, two).astype(f32)
    once = jnp.where(row == 0, 0.0, two).astype(f32)
    for cls, wgt in enumerate((edge, two, edge, once)):
        kr_ref[0, cls] = (kr[cls] - hb0) * wgt
        ki_ref[0, cls] = ki[cls] * wgt
    for j in range(2):
        ks_ref[0, 2 * j:2 * j + 1, :] = (odd_s[j][0] - hb0) * two
        ks_ref[0, 2 * j + 1:2 * j + 2, :] = odd_d[j][1] * two


def _filters(L, tab, w1, b1, w2, b2, w3, b3, w4, freq):
    M = L // HY_PH
    z = jnp.asarray(_phase_major(_filter_embedding(L), L).T)
    t = jnp.asarray(_phase_major(np.linspace(0.0, 1.0, L)[:, None].astype(np.float32), L))
    rate = jnp.asarray(_decay_rates())
    w1p = jnp.zeros((EMB_PAD, HY_FILTER_HIDDEN), f32).at[:HY_EMB_DIM].set(w1.astype(f32)).T
    nct = HY_WIDTH // HY_CT
    col = lambda a: a.astype(f32).reshape(-1, 1)
    full = lambda shape: pl.BlockSpec(shape, lambda o, j: (0,) * len(shape))
    H = HY_FILTER_HIDDEN
    kspec = pl.BlockSpec((1, HY_PH, M, HY_CT), lambda o, j: (o, 0, 0, j))
    kshape = jax.ShapeDtypeStruct((HY_ORDER, HY_PH, M, HY_WIDTH), f32)
    return pl.pallas_call(
        _filters_kernel,
        grid=(HY_ORDER, nct),
        in_specs=[full((EMB_PAD, L)), full((H, EMB_PAD)), full((H, 1)), full((H, H)), full((H, 1)),
                  full((H, H)), full((H, 1)), full((H, 1)),
                  pl.BlockSpec((H, HY_CT), lambda o, j: (0, 2 * nct * o + j)),
                  pl.BlockSpec((H, HY_CT), lambda o, j: (0, 2 * nct * o + nct + j)),
                  full((L, 1)),
                  pl.BlockSpec((1, HY_CT), lambda o, j: (0, j)),
                  _const_spec(tab.shape, lambda o, j: (0, 0, 0))],
        out_specs=[kspec, kspec, pl.BlockSpec((1, 4, HY_CT), lambda o, j: (o, 0, j))],
        out_shape=[kshape, kshape, jax.ShapeDtypeStruct((HY_ORDER, 4, HY_WIDTH), f32)],
        scratch_shapes=[pltpu.VMEM((L, H), f32)],
        compiler_params=pltpu.CompilerParams(dimension_semantics=("arbitrary", "arbitrary"),
                                             vmem_limit_bytes=VMEM_LIMIT),
        name="hyena_filters",
    )(z, w1p, col(b1), w2.astype(f32).T, col(b2), w3.astype(f32).T, col(b3), col(freq),
      w4.astype(f32), w4.astype(f32), t, rate, tab)


def _hyena_kernel(xn_ref, wv_ref, wx1_ref, wx2_ref, wg_ref, bv_ref, bx1_ref, bx2_ref, bg_ref,
                  cwv_ref, cwx1_ref, cwx2_ref, cbv_ref, cbx1_ref, cbx2_ref,
                  tab_ref, kr_ref, ki_ref, ks_ref, skip_ref, o_ref,
                  z_ref, u_ref, ub_ref, pq_ref, x1_ref, x2_ref, g_ref):
    L = xn_ref.shape[1]
    M = L // HY_PH
    nlt = HY_CT // LANES
    row_chunks = [slice(r, r + HY_RC) for r in range(0, L, HY_RC)]
    chunks = [slice(r, r + HY_RC) for r in range(0, M, HY_RC)]
    fchunks = [slice(r, r + HY_FC) for r in range(0, M, HY_FC)]
    row = lax.broadcasted_iota(jnp.int32, (HY_RC, HY_CT), 0)
    sgn = jnp.where((row & 1) == 1, -1.0, 1.0).astype(f32)
    for k in range(z_ref.shape[0]):
        for lt in range(nlt):
            z_ref[k, lt, 0:HALO] = jnp.zeros((HALO, LANES), f32)
            z_ref[k, lt, L + HALO:L + 2 * HALO] = jnp.zeros((HALO, LANES), f32)

    def proj_conv(k, dst_ref, w_ref, b_ref, cw_ref, cb_ref):
        w = w_ref[...].astype(bf16)
        for c in row_chunks:
            val = _dot(xn_ref[0, c, :], w) + b_ref[...]
            for lt in range(nlt):
                z_ref[k, lt, HALO + c.start:HALO + c.stop, :] = val[:, lt * LANES:(lt + 1) * LANES]

        def phase_rows(p, c, shift=0):
            src = pl.ds(HALO + p + shift + HY_PH * c.start, HY_RC, stride=HY_PH)
            return jnp.concatenate([z_ref[k, lt, src, :] for lt in range(nlt)], axis=1)

        for p in range(HY_PH):
            for c in chunks:
                dst_ref[p, c] = (cb_ref[...] + phase_rows(p, c, -1) * cw_ref[0:1, :]
                                 + phase_rows(p, c) * cw_ref[1:2, :] + phase_rows(p, c, 1) * cw_ref[2:3, :])

    def long_conv(o, x_ref):
        r = [jnp.zeros((1, HY_CT), f32) for _ in range(HY_PH)]
        for p in range(HY_PH):
            for c in chunks:
                u = u_ref[p, c]
                ub_ref[p, c] = u.astype(bf16)
                r[p] = r[p] + jnp.sum(u * sgn, axis=0, keepdims=True)
        for c in fchunks:
            A = [_dot(tab_ref[2 * p, c, :], ub_ref[p]) for p in range(HY_PH)]
            B = [_dot(tab_ref[2 * p + 1, c, :], ub_ref[p]) for p in range(HY_PH)]
            e, f = _butterfly(A, B)
            a, b = _spectrum_cos(e, f), _spectrum_sin(e, f)
            P, Q = [], []
            for cls in range(HY_PH):
                kr, ki = kr_ref[o, cls, c, :], ki_ref[o, cls, c, :]
                P.append(a[cls] * kr + b[cls] * ki)
                Q.append(b[cls] * kr - a[cls] * ki)
            g = (P[0] + P[2], P[0] - P[2], P[1] + P[3], P[3] - P[1])
            h = (Q[0] - Q[2], Q[0] + Q[2], Q[1] - Q[3], Q[1] + Q[3])
            X = (g[0] + g[2], g[1] + h[3], g[0] - g[2], g[1] - h[3])
            Y = (h[0] + h[2], h[1] + g[3], h[0] - h[2], h[1] - g[3])
            for p in range(HY_PH):
                pq_ref[2 * p, c] = X[p].astype(bf16)
                pq_ref[2 * p + 1, c] = Y[p].astype(bf16)
        pq_odd = []
        for j, (a_o, b_o) in enumerate(_odd_bins(r)):
            kr, ki = ks_ref[o, 2 * j:2 * j + 1, :], ks_ref[o, 2 * j + 1:2 * j + 2, :]
            pq_odd.append((a_o * kr + b_o * ki, b_o * kr - a_o * ki))
        (p1, q1), (p3, q3) = pq_odd
        odd = (p1 + p3, (p1 + q1 - p3 + q3) * RSQRT2, q1 - q3, (q1 - p1 + p3 + q3) * RSQRT2)
        skip = skip_ref[o:o + 1, :]
        nt = 2 * HY_PH
        for p in range(HY_PH):
            for c in chunks:
                y = _dot(tab_ref[nt + 2 * p, c, :], pq_ref[2 * p]) + _dot(tab_ref[nt + 2 * p + 1, c, :], pq_ref[2 * p + 1])
                u_ref[p, c] = x_ref[p, c] * (y + sgn * odd[p] + u_ref[p, c] * skip)

    proj_conv(0, u_ref, wv_ref, bv_ref, cwv_ref, cbv_ref)
    proj_conv(1, x1_ref, wx1_ref, bx1_ref, cwx1_ref, cbx1_ref)
    proj_conv(2, x2_ref, wx2_ref, bx2_ref, cwx2_ref, cbx2_ref)
    wg = wg_ref[...].astype(bf16)
    for c in row_chunks:
        g = _dot(xn_ref[0, c, :], wg) + bg_ref[...]
        g_ref[c] = g * jax.nn.sigmoid(g)
    long_conv(0, x1_ref)
    long_conv(1, x2_ref)
    for p in range(HY_PH):
        for c in chunks:
            dst = pl.ds(HALO + p + HY_PH * c.start, HY_RC, stride=HY_PH)
            y = u_ref[p, c]
            for lt in range(nlt):
                z_ref[0, lt, dst, :] = y[:, lt * LANES:(lt + 1) * LANES]
    for c in row_chunks:
        y = jnp.concatenate([z_ref[0, lt, HALO + c.start:HALO + c.stop, :] for lt in range(nlt)], axis=1)
        o_ref[0, c, :] = (y * g_ref[c]).astype(o_ref.dtype)


def _hyena(xs, w_in, b_in, conv_w, conv_b, tab, kr, ki, ks, skip):
    _, B, L, _ = xs.shape
    M = L // HY_PH
    nct = HY_WIDTH // HY_CT
    hg = O_HGATE // HY_CT

    def col(k):
        return lambda j, b: (0, k * nct + j)

    wspec = lambda k: _const_spec((D_MODEL, HY_CT), col(k))
    bspec = lambda k: pl.BlockSpec((1, HY_CT), col(k))
    cwspec = lambda k: pl.BlockSpec((HY_SHORT_CONV, HY_CT), col(k))
    kspec = _const_spec((HY_ORDER, HY_PH, M, HY_CT), lambda j, b: (0, 0, 0, j))
    return pl.pallas_call(
        _hyena_kernel,
        grid=(nct, B),
        in_specs=[pl.BlockSpec((None, 1, L, D_MODEL), lambda j, b: (NAT, b, 0, 0)),
                  wspec(0), wspec(1), wspec(2), _const_spec((D_MODEL, HY_CT), lambda j, b: (0, hg + j)),
                  bspec(0), bspec(1), bspec(2), pl.BlockSpec((1, HY_CT), lambda j, b: (0, hg + j)),
                  cwspec(0), cwspec(1), cwspec(2), bspec(0), bspec(1), bspec(2),
                  _const_spec(tab.shape, lambda j, b: (0, 0, 0)),
                  kspec, kspec,
                  pl.BlockSpec((HY_ORDER, 4, HY_CT), lambda j, b: (0, 0, j)),
                  pl.BlockSpec((HY_ORDER, HY_CT), lambda j, b: (0, j))],
        out_specs=pl.BlockSpec((1, L, HY_CT), lambda j, b: (b, 0, j)),
        out_shape=jax.ShapeDtypeStruct((B, L, HY_WIDTH), bf16),
        scratch_shapes=[pltpu.VMEM((3, HY_CT // LANES, L + 2 * HALO, LANES), f32), pltpu.VMEM((HY_PH, M, HY_CT), f32),
                        pltpu.VMEM((HY_PH, M, HY_CT), bf16), pltpu.VMEM((2 * HY_PH, M, HY_CT), bf16),
                        pltpu.VMEM((HY_PH, M, HY_CT), f32), pltpu.VMEM((HY_PH, M, HY_CT), f32),
                        pltpu.VMEM((L, HY_CT), f32)],
        compiler_params=pltpu.CompilerParams(dimension_semantics=("arbitrary", "arbitrary"),
                                             vmem_limit_bytes=VMEM_LIMIT),
        name="hyena_mixer",
    )(xs, w_in, w_in, w_in, w_in, b_in, b_in, b_in, b_in,
      conv_w, conv_w, conv_w, conv_b, conv_b, conv_b, tab, kr, ki, ks, skip)


def _attn_kernel(xs_ref, wq_ref, wk_ref, wv_ref, bq_ref, bk_ref, bv_ref, wag_ref, bag_ref,
                 gq_ref, gk_ref, hsum_ref, sl_ref, d0_ref, d1_ref, d2_ref, o_ref,
                 qs_ref, ks_ref, vs_ref, acc_ref, mx_ref, den_ref):
    L = xs_ref.shape[2]
    gw = HEADS_PER_GROUP * HEAD_DIM
    npair = HEADS_PER_GROUP // 2
    dist_refs = (d0_ref, d1_ref, d2_ref)
    tq = Q_TILE
    first = lax.broadcasted_iota(jnp.int32, (tq, PAIR), 1) < HEAD_DIM
    nt_dims = (((1,), (1,)), ((), ()))

    def normed(x, w, b_ref, g_ref):
        z = _dot(x, w) + b_ref[...]
        z2 = (z * z).astype(bf16)
        ssq = jnp.concatenate([_dot(z2[:, c:c + MXU_DIM], hsum_ref[...]) for c in range(0, gw, MXU_DIM)], axis=1)
        return z * lax.rsqrt(ssq * (1.0 / HEAD_DIM) + NORM_EPS) * g_ref[...]

    def group(gi):
        _, d = DILATED_GROUPS[GROUP_ORDER[gi]]
        n = L // d
        w = min(2 * tq, n)
        per_class = n // tq
        dist_ref = dist_refs[GROUP_ORDER[gi]]

        wq, wk, wv = (r[...].astype(bf16) for r in (wq_ref, wk_ref, wv_ref))
        for r0 in range(0, L, AT_RC):
            rows = slice(r0, r0 + AT_RC)
            x = xs_ref[0, 0, rows, :]
            qs_ref[rows] = (normed(x, wq, bq_ref, gq_ref) * (HEAD_DIM ** -0.5)).astype(bf16)
            ks_ref[rows] = normed(x, wk, bk_ref, gk_ref).astype(bf16)
            vs_ref[rows] = (_dot(x, wv) + bv_ref[...]).astype(bf16)

        def tile(idx, carry):
            r = idx // per_class
            t = idx % per_class
            q0 = pl.multiple_of(idx * tq, tq)
            koff = jnp.clip(t * tq - BAND_HALF, 0, n - w)
            dist_t = dist_ref[(t * tq - koff) // BAND_HALF]
            k0 = pl.multiple_of(r * n + koff, BAND_HALF)
            nat = pl.ds(t * tq * d + r, tq, stride=d) if d > 1 else pl.ds(q0, tq)
            for p in range(npair):
                pc = slice(p * PAIR, (p + 1) * PAIR)
                q = qs_ref[pl.ds(q0, tq), pc]
                zero = jnp.zeros_like(q)
                qq = jnp.concatenate([jnp.where(first, q, zero), jnp.where(first, zero, q)], axis=0)
                lhs = jnp.concatenate([qq, sl_ref[p]], axis=1)
                rhs_t = jnp.concatenate([ks_ref[pl.ds(k0, w), pc], dist_t], axis=1)
                s = lax.dot_general(lhs, rhs_t, nt_dims, preferred_element_type=f32)
                m = jnp.max(s, axis=-1, keepdims=True)
                pr = jnp.exp(s - m).astype(bf16)
                rhs = jnp.concatenate([vs_ref[pl.ds(k0, w), pc], jnp.ones((w, PAIR), bf16)], axis=1)
                ov = _dot(pr, rhs)
                num = jnp.where(first, ov[0:tq, 0:PAIR], ov[tq:2 * tq, 0:PAIR])
                den = jnp.where(first, ov[0:tq, PAIR:2 * PAIR], ov[tq:2 * tq, PAIR:2 * PAIR])
                mb = jnp.where(first, m[0:tq], m[tq:2 * tq])
                if gi == 0:
                    acc_ref[p, nat, :] = num
                    den_ref[p, nat, :] = den
                    mx_ref[p, nat, :] = mb
                else:
                    m_old = mx_ref[p, nat, :]
                    m_new = jnp.maximum(m_old, mb)
                    a = jnp.exp(m_old - m_new)
                    b = jnp.exp(mb - m_new)
                    acc_ref[p, nat, :] = acc_ref[p, nat, :] * a + num * b
                    den_ref[p, nat, :] = den_ref[p, nat, :] * a + den * b
                    mx_ref[p, nat, :] = m_new
            return carry

        lax.fori_loop(0, L // tq, tile, 0, unroll=8)

    for gi in range(N_GROUPS):
        pl.when(pl.program_id(1) == gi)(functools.partial(group, gi))

    @pl.when(pl.program_id(1) == N_GROUPS - 1)
    def _():
        wag = wag_ref[...].astype(bf16)
        for r0 in range(0, L, AT_RC):
            rows = slice(r0, r0 + AT_RC)
            ag = _dot(xs_ref[0, 0, rows, :], wag) + bag_ref[...]
            o = jnp.concatenate([acc_ref[p, rows, :] / den_ref[p, rows, :] for p in range(npair)], axis=1)
            o_ref[0, rows, :] = (o * (ag * jax.nn.sigmoid(ag))).astype(o_ref.dtype)


def _attention(xs, w_in, b_in, gq, gk, hsum):
    _, B, L, _ = xs.shape
    assert PERM_DILATIONS == tuple(DILATED_GROUPS[g][1] for g in GROUP_ORDER) and PERM_DILATIONS[-1] == 1
    assert GROUP_ORDER == tuple(N_GROUPS - 1 - i for i in range(N_GROUPS))
    assert L % AT_RC == 0 and all(L % (Q_TILE * d) == 0 for _, d in DILATED_GROUPS)
    gw = HEADS_PER_GROUP * HEAD_DIM
    npair = HEADS_PER_GROUP // 2
    dists = [jnp.asarray(_attn_dist(L // d, d, window)).astype(bf16) for window, d in DILATED_GROUPS]
    sl = jnp.asarray(_slope_eye()).astype(bf16)
    agb = O_AGATE // gw

    def col(k):
        return lambda b, i: (0, (O_QKV + k * AT_QKV) // gw + (N_GROUPS - 1 - i))

    wspec = lambda k: pl.BlockSpec((D_MODEL, gw), col(k))
    bspec = lambda k: pl.BlockSpec((1, gw), col(k))
    vec = pl.BlockSpec((1, gw), lambda b, i: (0, 0))
    acc = pltpu.VMEM((npair, L, PAIR), f32)
    return pl.pallas_call(
        _attn_kernel,
        grid=(B, N_GROUPS),
        in_specs=[pl.BlockSpec((1, 1, L, D_MODEL), lambda b, i: (i, b, 0, 0)),
                  wspec(0), wspec(1), wspec(2), bspec(0), bspec(1), bspec(2),
                  _const_spec((D_MODEL, gw), lambda b, i: (0, agb)), pl.BlockSpec((1, gw), lambda b, i: (0, agb)),
                  vec, vec, _const_spec(hsum.shape, lambda b, i: (0, 0)), _const_spec(sl.shape, lambda b, i: (0, 0, 0))]
                 + [_const_spec(t.shape, lambda b, i: (0, 0, 0)) for t in dists],
        out_specs=pl.BlockSpec((1, L, gw), lambda b, i: (b, 0, 0)),
        out_shape=jax.ShapeDtypeStruct((B, L, gw), bf16),
        scratch_shapes=[pltpu.VMEM((L, gw), bf16), pltpu.VMEM((L, gw), bf16), pltpu.VMEM((L, gw), bf16),
                        acc, acc, acc],
        compiler_params=pltpu.CompilerParams(dimension_semantics=("arbitrary", "arbitrary"),
                                             vmem_limit_bytes=VMEM_LIMIT),
        name="dilated_attention",
    )(xs, w_in, w_in, w_in, b_in, b_in, b_in, w_in, b_in, gq, gk, hsum, sl, *dists)


def _final_kernel(x_ref, xn_ref, gh_ref, ga_ref, wg_ref, bg_ref, why_ref, wat_ref, wout_ref, out_ref,
                  wg_b, why_b, wat_b, wout_b):
    @pl.when(pl.program_id(0) == 0)
    def _():
        for src, dst in ((wg_ref, wg_b), (why_ref, why_b), (wat_ref, wat_b), (wout_ref, wout_b)):
            dst[...] = src[...].astype(bf16)

    gates = _dot(xn_ref[...], wg_b[...]) + bg_ref[...]
    u_h = _dot(gh_ref[...], why_b[...])
    u_a = _dot(ga_ref[...], wat_b[...])
    merged = jax.nn.sigmoid(gates[:, 0:D_MODEL]) * u_h + jax.nn.sigmoid(gates[:, D_MODEL:]) * u_a
    out_ref[...] = x_ref[...] + _dot(merged.astype(bf16), wout_b[...])


def _final(x2, xs2, gh2, ga2, w_in, b_in, why, wat, wout):
    rows = x2.shape[0]
    tm = 512
    mgw = 2 * D_MODEL
    rspec = lambda c: pl.BlockSpec((tm, c), lambda i: (i, 0))
    cspec = lambda a: _const_spec(a.shape, lambda i: (0, 0))
    return pl.pallas_call(
        _final_kernel,
        grid=(rows // tm,),
        in_specs=[rspec(D_MODEL), pl.BlockSpec((None, tm, D_MODEL), lambda i: (NAT, i, 0)), rspec(HY_WIDTH), rspec(AT_WIDTH),
                  _const_spec((D_MODEL, mgw), lambda i: (0, O_MG // mgw)),
                  pl.BlockSpec((1, mgw), lambda i: (0, O_MG // mgw)), cspec(why), cspec(wat), cspec(wout)],
        out_specs=rspec(D_MODEL),
        out_shape=jax.ShapeDtypeStruct((rows, D_MODEL), f32),
        scratch_shapes=[pltpu.VMEM((D_MODEL, mgw), bf16), pltpu.VMEM(why.shape, bf16), pltpu.VMEM(wat.shape, bf16),
                        pltpu.VMEM(wout.shape, bf16)],
        compiler_params=pltpu.CompilerParams(dimension_semantics=("arbitrary",),
                                             vmem_limit_bytes=VMEM_LIMIT),
        name="merge_output",
    )(x2, xs2, gh2, ga2, w_in, b_in, why, wat, wout)


def _layer(x, norm_g, w_in, b_in, conv_w, conv_b, hf_w1, hf_b1, hf_w2, hf_b2, hf_w3, hf_b3, hf_w4,
           hf_freq, hy_skip, q_norm_g, k_norm_g, w_hy_out, w_at_out, w_out):
    B, L, D = x.shape
    x2 = x.reshape(B * L, D)
    tab = jnp.asarray(_dft_tables(L)).astype(bf16)
    w_in = w_in.astype(f32)
    b_in2 = b_in.astype(f32).reshape(1, IN_COLS)

    xs = _prenorm(x, norm_g.astype(f32).reshape(1, D))

    kr, ki, ks = _filters(L, tab, hf_w1, hf_b1, hf_w2, hf_b2, hf_w3, hf_b3, hf_w4, hf_freq)
    gh = _hyena(xs, w_in, b_in2, conv_w.astype(f32), conv_b.astype(f32).reshape(1, -1),
                tab, kr, ki, ks, hy_skip.astype(f32))

    gq = jnp.tile(q_norm_g.astype(f32), HEADS_PER_GROUP).reshape(1, -1)
    gk = jnp.tile(k_norm_g.astype(f32), HEADS_PER_GROUP).reshape(1, -1)
    head = np.arange(MXU_DIM) // HEAD_DIM
    hsum = jnp.asarray((head[:, None] == head[None, :]).astype(np.float32)).astype(bf16)
    ga = _attention(xs, w_in, b_in2, gq, gk, hsum)

    out = _final(x2, xs.reshape(len(PERM_DILATIONS), B * L, D), gh.reshape(B * L, HY_WIDTH), ga.reshape(B * L, AT_WIDTH),
                 w_in, b_in2, w_hy_out.astype(f32), w_at_out.astype(f32), w_out.astype(f32))
    return out.reshape(B, L, D)


def kernel(x, norm_g, w_in, b_in, conv_w, conv_b, hf_w1, hf_b1, hf_w2, hf_b2, hf_w3, hf_b3, hf_w4,
           hf_freq, hy_skip, q_norm_g, k_norm_g, w_hy_out, w_at_out, w_out):
    depth = norm_g.shape[0]
    for i in range(depth):
        x = _layer(x, norm_g[i], w_in[i], b_in[i], conv_w[i], conv_b[i], hf_w1[i], hf_b1[i], hf_w2[i],
                   hf_b2[i], hf_w3[i], hf_b3[i], hf_w4[i], hf_freq[i], hy_skip[i], q_norm_g[i],
                   k_norm_g[i], w_hy_out[i], w_at_out[i], w_out[i])
    return x
```

```python
import functools
import math

import jax
import jax.numpy as jnp
import numpy as np
from jax import lax
from jax.experimental import pallas as pl
from jax.experimental.pallas import tpu as pltpu

D_MODEL = 1024
HY_WIDTH = 768
HY_ORDER = 2
HY_SHORT_CONV = 3
HY_EMB_DIM = 33
HY_FILTER_HIDDEN = 64
HY_FAST_DECAY = 0.3
HY_SLOW_DECAY = 1.5
HY_DECAY_TARGET = 1e-2
HY_MOD_SHIFT = 0.0
HEAD_DIM = 64
HEADS_PER_GROUP = 8
DILATED_GROUPS = ((128, 1), (512, 4), (2048, 16))
N_GROUPS = 3
AT_QKV = N_GROUPS * HEADS_PER_GROUP * HEAD_DIM
AT_WIDTH = HEADS_PER_GROUP * HEAD_DIM
NORM_EPS = 1e-6
NEG_INF = -1e30

O_HGATE = 3 * HY_WIDTH
O_QKV = O_HGATE + HY_WIDTH
O_AGATE = O_QKV + 3 * AT_QKV
O_MG = O_AGATE + AT_WIDTH
IN_COLS = O_MG + 2 * D_MODEL

LANES = 128
MXU_DIM = 256
VMEM_LIMIT = 56 * 1024 * 1024

HY_CT = 256
HY_RC = 512
HY_FC = 256
HY_PH = 4
RSQRT2 = math.sqrt(0.5)
HALO = 16
AT_RC = 512
EMB_PAD = 128
Q_TILE = 128
BAND_HALF = 64
GROUP_ORDER = (2, 1, 0)
PERM_DILATIONS = (16, 4, 1)
NAT = 2
PN_RC = 512
PAIR = 2 * HEAD_DIM

f32 = jnp.float32
bf16 = jnp.bfloat16


def _dot(a, b):
    return jnp.dot(a, b, preferred_element_type=f32)


def _const_spec(shape, index_map):
    return pl.BlockSpec(shape, index_map, pipeline_mode=pl.Buffered(1))


@functools.lru_cache(maxsize=None)
def _dft_tables(L):
    n = 2 * L
    f = np.arange(L // HY_PH, dtype=np.int64)[:, None]
    m = np.arange(L // HY_PH, dtype=np.int64)[None, :]
    fwd = []
    for p in range(HY_PH):
        ang = ((f * (HY_PH * m + p)) % n).astype(np.float64) * (2.0 * np.pi / n)
        fwd += [np.cos(ang), np.sin(ang)]
    return np.stack(fwd + [t.T for t in fwd]).astype(np.float32)


def _butterfly(A, B):
    e = (A[0] + A[2], A[0] - A[2], A[1] + A[3], A[1] - A[3])
    f = (B[0] + B[2], B[0] - B[2], B[1] + B[3], B[1] - B[3])
    return e, f


def _spectrum_cos(e, f):
    return (e[0] + e[2], e[1] - f[3], e[0] - e[2], e[1] + f[3])


def _spectrum_sin(e, f):
    return (f[0] + f[2], f[1] + e[3], f[2] - f[0], e[3] - f[1])


def _odd_bins(r):
    d, s = (r[1] - r[3]) * RSQRT2, (r[1] + r[3]) * RSQRT2
    return (r[0] + d, s + r[2]), (r[0] - d, s - r[2])


def _phase_major(a, L):
    return np.concatenate([a[p::HY_PH] for p in range(HY_PH)], axis=0)


@functools.lru_cache(maxsize=None)
def _filter_embedding(L):
    t = np.linspace(0.0, 1.0, L)[:, None]
    bands = (HY_EMB_DIM - 1) // 2
    w = 2.0 * np.pi * np.arange(L)[:, None] / L
    f = np.linspace(1e-4, bands - 1, bands)[None, :]
    z = np.concatenate([t, np.cos(f * w), -np.sin(f * w)], axis=-1)
    zp = np.zeros((L, EMB_PAD), np.float64)
    zp[:, :HY_EMB_DIM] = z
    return zp.astype(np.float32)


@functools.lru_cache(maxsize=None)
def _decay_rates():
    max_decay = math.log(HY_DECAY_TARGET) / HY_FAST_DECAY
    min_decay = math.log(HY_DECAY_TARGET) / HY_SLOW_DECAY
    return np.abs(np.linspace(min_decay, max_decay, HY_WIDTH))[None, :].astype(np.float32)


def _alibi_slope(h):
    return 2.0 ** (-8.0 * (h + 1) / HEADS_PER_GROUP)


@functools.lru_cache(maxsize=None)
def _attn_dist(n, dilation, window):
    half = window // (2 * dilation)
    assert half == BAND_HALF
    tq = min(Q_TILE, n)
    w = min(2 * Q_TILE, n)
    masked = NEG_INF / _alibi_slope(HEADS_PER_GROUP - 1)
    offs = sorted({q0 - min(max(q0 - half, 0), n - w) for q0 in range(0, n, tq)})
    assert offs == [BAND_HALF * i for i in range(len(offs))]
    out = np.zeros((len(offs), w, tq), np.float32)
    for ci, off in enumerate(offs):
        rel = np.arange(tq)[None, :] + off - np.arange(w)[:, None]
        out[ci] = np.where(np.abs(rel) <= half, -dilation * np.abs(rel), masked)
    return out


@functools.lru_cache(maxsize=None)
def _slope_eye():
    eye = np.eye(Q_TILE, dtype=np.float32)
    return np.stack([np.concatenate([_alibi_slope(2 * p) * eye, _alibi_slope(2 * p + 1) * eye], axis=0)
                     for p in range(HEADS_PER_GROUP // 2)])


@functools.lru_cache(maxsize=None)
def _residue_perm(n):
    p = np.zeros((n, n), np.float32)
    i = np.arange(n)
    p[(i % 4) * (n // 4) + i // 4, i] = 1.0
    return p


def _prenorm_kernel(x_ref, g_ref, perm_ref, o_ref, st_ref):
    L = x_ref.shape[1]
    nt = D_MODEL // LANES
    d4, d16 = PERM_DILATIONS.index(4), PERM_DILATIONS.index(16)
    n4, n16 = L // 4, L // 16
    for r0 in range(0, L, PN_RC):
        rows = slice(r0, r0 + PN_RC)
        x = x_ref[0, rows, :]
        ms = jnp.mean(x * x, axis=-1, keepdims=True)
        xn = x * lax.rsqrt(ms + NORM_EPS) * g_ref[...]
        o_ref[NAT, 0, rows, :] = xn.astype(o_ref.dtype)
        for c in range(nt):
            st_ref[c, rows, :] = xn[:, c * LANES:(c + 1) * LANES]

    def gather(r, carry):
        dst = pl.ds(pl.multiple_of(r * n4, n4), n4)
        for c in range(nt):
            o_ref[d4, 0, dst, c * LANES:(c + 1) * LANES] = st_ref[c, pl.ds(r, n4, stride=4), :].astype(o_ref.dtype)
        return carry

    lax.fori_loop(0, 4, gather, 0)
    for r4 in range(4):
        y = _dot(perm_ref[...], o_ref[d4, 0, r4 * n4:(r4 + 1) * n4, :]).astype(o_ref.dtype)
        for q in range(4):
            o_ref[d16, 0, (r4 + 4 * q) * n16:(r4 + 4 * q + 1) * n16, :] = y[q * n16:(q + 1) * n16]


def _prenorm(x, g):
    B, L, D = x.shape
    assert sorted(PERM_DILATIONS) == [1, 4, 16] and PERM_DILATIONS[NAT] == 1
    perm = jnp.asarray(_residue_perm(L // 4)).astype(bf16)
    return pl.pallas_call(
        _prenorm_kernel,
        grid=(B,),
        in_specs=[pl.BlockSpec((1, L, D), lambda b: (b, 0, 0)),
                  pl.BlockSpec((1, D), lambda b: (0, 0)),
                  _const_spec(perm.shape, lambda b: (0, 0))],
        out_specs=pl.BlockSpec((len(PERM_DILATIONS), 1, L, D), lambda b: (0, b, 0, 0)),
        out_shape=jax.ShapeDtypeStruct((len(PERM_DILATIONS), B, L, D), bf16),
        scratch_shapes=[pltpu.VMEM((D // LANES, L, LANES), f32)],
        compiler_params=pltpu.CompilerParams(dimension_semantics=("arbitrary",),
                                             vmem_limit_bytes=VMEM_LIMIT),
        name="prenorm",
    )(x, g, perm)


def _filters_kernel(z_ref, w1_ref, b1_ref, w2_ref, b2_ref, w3_ref, b3_ref, fr_ref, w4f_ref, w4b_ref,
                    t_ref, rate_ref, tab_ref, kr_ref, ki_ref, ks_ref, h3_ref):
    L = z_ref.shape[0]
    M = L // HY_PH

    def split(a):
        a_hi = a.astype(bf16)
        return a_hi, (a - a_hi.astype(f32)).astype(bf16)

    def dot3(a, b):
        (a_hi, a_lo), (b_hi, b_lo) = a, b
        return _dot(a_hi, b_hi) + (_dot(a_lo, b_hi) + _dot(a_hi, b_lo))

    @pl.when((pl.program_id(0) == 0) & (pl.program_id(1) == 0))
    def _():
        fr = fr_ref[...]
        h = jnp.sin(fr * (dot3(split(z_ref[...]), split(w1_ref[...])) + b1_ref[...]))
        h = jnp.sin(fr * (dot3(split(h), split(w2_ref[...])) + b2_ref[...]))
        h = jnp.sin(fr * (dot3(split(h), split(w3_ref[...])) + b3_ref[...]))
        h3_ref[...] = h

    h3 = split(h3_ref[...])
    decay = jnp.exp(-t_ref[...] * rate_ref[...]) + HY_MOD_SHIFT
    hf = dot3(h3, split(w4f_ref[...])) * decay
    hb = dot3(h3, split(w4b_ref[...])) * decay
    hb0 = hb[0:1, :]
    hs = hf + hb
    hd = hb - hf
    n = 2 * L
    row = lax.broadcasted_iota(jnp.int32, (M, HY_CT), 0)
    sgn = jnp.where((row & 1) == 1, -1.0, 1.0).astype(f32)

    def transform(x):
        xp = [x[p * M:(p + 1) * M] for p in range(HY_PH)]
        A = [_dot(tab_ref[2 * p], xp[p].astype(bf16)) for p in range(HY_PH)]
        B = [_dot(tab_ref[2 * p + 1], xp[p].astype(bf16)) for p in range(HY_PH)]
        r = [jnp.sum(xp[p] * sgn, axis=0, keepdims=True) for p in range(HY_PH)]
        return _butterfly(A, B), _odd_bins(r)

    (es, fs), odd_s = transform(hs)
    (ed, fd), odd_d = transform(hd)
    kr = _spectrum_cos(es, fs)
    ki = _spectrum_sin(ed, fd)
    two = 2.0 / n
    edge = jnp.where(row == 0, 1.0 / n, two).astype(f32)
    once = jnp.where(row == 0, 0.0, two).astype(f32)
    for cls, wgt in enumerate((edge, two, edge, once)):
        kr_ref[0, cls] = (kr[cls] - hb0) * wgt
        ki_ref[0, cls] = ki[cls] * wgt
    for j in range(2):
        ks_ref[0, 2 * j:2 * j + 1, :] = (odd_s[j][0] - hb0) * two
        ks_ref[0, 2 * j + 1:2 * j + 2, :] = odd_d[j][1] * two


def _filters(L, tab, w1, b1, w2, b2, w3, b3, w4, freq):
    M = L // HY_PH
    z = jnp.asarray(_phase_major(_filter_embedding(L), L))
    t = jnp.asarray(_phase_major(np.linspace(0.0, 1.0, L)[:, None].astype(np.float32), L))
    rate = jnp.asarray(_decay_rates())
    w1p = jnp.zeros((EMB_PAD, HY_FILTER_HIDDEN), f32).at[:HY_EMB_DIM].set(w1.astype(f32))
    nct = HY_WIDTH // HY_CT
    row = lambda a: a.astype(f32).reshape(1, -1)
    full = lambda shape: pl.BlockSpec(shape, lambda o, j: (0,) * len(shape))
    H = HY_FILTER_HIDDEN
    kspec = pl.BlockSpec((1, HY_PH, M, HY_CT), lambda o, j: (o, 0, 0, j))
    kshape = jax.ShapeDtypeStruct((HY_ORDER, HY_PH, M, HY_WIDTH), f32)
    return pl.pallas_call(
        _filters_kernel,
        grid=(HY_ORDER, nct),
        in_specs=[full((L, EMB_PAD)), full((EMB_PAD, H)), full((1, H)), full((H, H)), full((1, H)),
                  full((H, H)), full((1, H)), full((1, H)),
                  pl.BlockSpec((H, HY_CT), lambda o, j: (0, 2 * nct * o + j)),
                  pl.BlockSpec((H, HY_CT), lambda o, j: (0, 2 * nct * o + nct + j)),
                  full((L, 1)),
                  pl.BlockSpec((1, HY_CT), lambda o, j: (0, j)),
                  _const_spec(tab.shape, lambda o, j: (0, 0, 0))],
        out_specs=[kspec, kspec, pl.BlockSpec((1, 4, HY_CT), lambda o, j: (o, 0, j))],
        out_shape=[kshape, kshape, jax.ShapeDtypeStruct((HY_ORDER, 4, HY_WIDTH), f32)],
        scratch_shapes=[pltpu.VMEM((L, H), f32)],
        compiler_params=pltpu.CompilerParams(dimension_semantics=("arbitrary", "arbitrary"),
                                             vmem_limit_bytes=VMEM_LIMIT),
        name="hyena_filters",
    )(z, w1p, row(b1), w2.astype(f32), row(b2), w3.astype(f32), row(b3), row(freq),
      w4.astype(f32), w4.astype(f32), t, rate, tab)


def _hyena_kernel(xn_ref, wv_ref, wx1_ref, wx2_ref, wg_ref, bv_ref, bx1_ref, bx2_ref, bg_ref,
                  cwv_ref, cwx1_ref, cwx2_ref, cbv_ref, cbx1_ref, cbx2_ref,
                  tab_ref, kr_ref, ki_ref, ks_ref, skip_ref, o_ref,
                  z_ref, u_ref, ub_ref, pq_ref, x1_ref, x2_ref, g_ref):
    L = xn_ref.shape[1]
    M = L // HY_PH
    nlt = HY_CT // LANES
    row_chunks = [slice(r, r + HY_RC) for r in range(0, L, HY_RC)]
    chunks = [slice(r, r + HY_RC) for r in range(0, M, HY_RC)]
    fchunks = [slice(r, r + HY_FC) for r in range(0, M, HY_FC)]
    row = lax.broadcasted_iota(jnp.int32, (HY_RC, HY_CT), 0)
    sgn = jnp.where((row & 1) == 1, -1.0, 1.0).astype(f32)
    for k in range(z_ref.shape[0]):
        for lt in range(nlt):
            z_ref[k, lt, 0:HALO] = jnp.zeros((HALO, LANES), f32)
            z_ref[k, lt, L + HALO:L + 2 * HALO] = jnp.zeros((HALO, LANES), f32)

    def proj_conv(k, dst_ref, w_ref, b_ref, cw_ref, cb_ref):
        w = w_ref[...].astype(bf16)
        for c in row_chunks:
            val = _dot(xn_ref[0, c, :], w) + b_ref[...]
            for lt in range(nlt):
                z_ref[k, lt, HALO + c.start:HALO + c.stop, :] = val[:, lt * LANES:(lt + 1) * LANES]

        def phase_rows(p, c, shift=0):
            src = pl.ds(HALO + p + shift + HY_PH * c.start, HY_RC, stride=HY_PH)
            return jnp.concatenate([z_ref[k, lt, src, :] for lt in range(nlt)], axis=1)

        for p in range(HY_PH):
            for c in chunks:
                dst_ref[p, c] = (cb_ref[...] + phase_rows(p, c, -1) * cw_ref[0:1, :]
                                 + phase_rows(p, c) * cw_ref[1:2, :] + phase_rows(p, c, 1) * cw_ref[2:3, :])

    def long_conv(o, x_ref):
        r = [jnp.zeros((1, HY_CT), f32) for _ in range(HY_PH)]
        for p in range(HY_PH):
            for c in chunks:
                u = u_ref[p, c]
                ub_ref[p, c] = u.astype(bf16)
                r[p] = r[p] + jnp.sum(u * sgn, axis=0, keepdims=True)
        for c in fchunks:
            A = [_dot(tab_ref[2 * p, c, :], ub_ref[p]) for p in range(HY_PH)]
            B = [_dot(tab_ref[2 * p + 1, c, :], ub_ref[p]) for p in range(HY_PH)]
            e, f = _butterfly(A, B)
            a, b = _spectrum_cos(e, f), _spectrum_sin(e, f)
            P, Q = [], []
            for cls in range(HY_PH):
                kr, ki = kr_ref[o, cls, c, :], ki_ref[o, cls, c, :]
                P.append(a[cls] * kr + b[cls] * ki)
                Q.append(b[cls] * kr - a[cls] * ki)
            g = (P[0] + P[2], P[0] - P[2], P[1] + P[3], P[3] - P[1])
            h = (Q[0] - Q[2], Q[0] + Q[2], Q[1] - Q[3], Q[1] + Q[3])
            X = (g[0] + g[2], g[1] + h[3], g[0] - g[2], g[1] - h[3])
            Y = (h[0] + h[2], h[1] + g[3], h[0] - h[2], h[1] - g[3])
            for p in range(HY_PH):
                pq_ref[2 * p, c] = X[p].astype(bf16)
                pq_ref[2 * p + 1, c] = Y[p].astype(bf16)
        pq_odd = []
        for j, (a_o, b_o) in enumerate(_odd_bins(r)):
            kr, ki = ks_ref[o, 2 * j:2 * j + 1, :], ks_ref[o, 2 * j + 1:2 * j + 2, :]
            pq_odd.append((a_o * kr + b_o * ki, b_o * kr - a_o * ki))
        (p1, q1), (p3, q3) = pq_odd
        odd = (p1 + p3, (p1 + q1 - p3 + q3) * RSQRT2, q1 - q3, (q1 - p1 + p3 + q3) * RSQRT2)
        skip = skip_ref[o:o + 1, :]
        nt = 2 * HY_PH
        for p in range(HY_PH):
            for c in chunks:
                y = _dot(tab_ref[nt + 2 * p, c, :], pq_ref[2 * p]) + _dot(tab_ref[nt + 2 * p + 1, c, :], pq_ref[2 * p + 1])
                u_ref[p, c] = x_ref[p, c] * (y + sgn * odd[p] + u_ref[p, c] * skip)

    proj_conv(0, u_ref, wv_ref, bv_ref, cwv_ref, cbv_ref)
    proj_conv(1, x1_ref, wx1_ref, bx1_ref, cwx1_ref, cbx1_ref)
    proj_conv(2, x2_ref, wx2_ref, bx2_ref, cwx2_ref, cbx2_ref)
    wg = wg_ref[...].astype(bf16)
    for c in row_chunks:
        g = _dot(xn_ref[0, c, :], wg) + bg_ref[...]
        g_ref[c] = g * jax.nn.sigmoid(g)
    long_conv(0, x1_ref)
    long_conv(1, x2_ref)
    for p in range(HY_PH):
        for c in chunks:
            dst = pl.ds(HALO + p + HY_PH * c.start, HY_RC, stride=HY_PH)
            y = u_ref[p, c]
            for lt in range(nlt):
                z_ref[0, lt, dst, :] = y[:, lt * LANES:(lt + 1) * LANES]
    for c in row_chunks:
        y = jnp.concatenate([z_ref[0, lt, HALO + c.start:HALO + c.stop, :] for lt in range(nlt)], axis=1)
        o_ref[0, c, :] = (y * g_ref[c]).astype(o_ref.dtype)


def _hyena(xs, w_in, b_in, conv_w, conv_b, tab, kr, ki, ks, skip):
    _, B, L, _ = xs.shape
    M = L // HY_PH
    nct = HY_WIDTH // HY_CT
    hg = O_HGATE // HY_CT

    def col(k):
        return lambda j, b: (0, k * nct + j)

    wspec = lambda k: _const_spec((D_MODEL, HY_CT), col(k))
    bspec = lambda k: pl.BlockSpec((1, HY_CT), col(k))
    cwspec = lambda k: pl.BlockSpec((HY_SHORT_CONV, HY_CT), col(k))
    kspec = _const_spec((HY_ORDER, HY_PH, M, HY_CT), lambda j, b: (0, 0, 0, j))
    return pl.pallas_call(
        _hyena_kernel,
        grid=(nct, B),
        in_specs=[pl.BlockSpec((None, 1, L, D_MODEL), lambda j, b: (NAT, b, 0, 0)),
                  wspec(0), wspec(1), wspec(2), _const_spec((D_MODEL, HY_CT), lambda j, b: (0, hg + j)),
                  bspec(0), bspec(1), bspec(2), pl.BlockSpec((1, HY_CT), lambda j, b: (0, hg + j)),
                  cwspec(0), cwspec(1), cwspec(2), bspec(0), bspec(1), bspec(2),
                  _const_spec(tab.shape, lambda j, b: (0, 0, 0)),
                  kspec, kspec,
                  pl.BlockSpec((HY_ORDER, 4, HY_CT), lambda j, b: (0, 0, j)),
                  pl.BlockSpec((HY_ORDER, HY_CT), lambda j, b: (0, j))],
        out_specs=pl.BlockSpec((1, L, HY_CT), lambda j, b: (b, 0, j)),
        out_shape=jax.ShapeDtypeStruct((B, L, HY_WIDTH), bf16),
        scratch_shapes=[pltpu.VMEM((3, HY_CT // LANES, L + 2 * HALO, LANES), f32), pltpu.VMEM((HY_PH, M, HY_CT), f32),
                        pltpu.VMEM((HY_PH, M, HY_CT), bf16), pltpu.VMEM((2 * HY_PH, M, HY_CT), bf16),
                        pltpu.VMEM((HY_PH, M, HY_CT), f32), pltpu.VMEM((HY_PH, M, HY_CT), f32),
                        pltpu.VMEM((L, HY_CT), f32)],
        compiler_params=pltpu.CompilerParams(dimension_semantics=("arbitrary", "arbitrary"),
                                             vmem_limit_bytes=VMEM_LIMIT),
        name="hyena_mixer",
    )(xs, w_in, w_in, w_in, w_in, b_in, b_in, b_in, b_in,
      conv_w, conv_w, conv_w, conv_b, conv_b, conv_b, tab, kr, ki, ks, skip)


def _attn_kernel(xs_ref, wq_ref, wk_ref, wv_ref, bq_ref, bk_ref, bv_ref, wag_ref, bag_ref,
                 gq_ref, gk_ref, hsum_ref, sl_ref, d0_ref, d1_ref, d2_ref, o_ref,
                 qs_ref, ks_ref, vs_ref, acc_ref, mx_ref, den_ref):
    L = xs_ref.shape[2]
    gw = HEADS_PER_GROUP * HEAD_DIM
    npair = HEADS_PER_GROUP // 2
    dist_refs = (d0_ref, d1_ref, d2_ref)
    tq = Q_TILE
    first = lax.broadcasted_iota(jnp.int32, (tq, PAIR), 1) < HEAD_DIM
    nt_dims = (((1,), (1,)), ((), ()))

    def normed(x, w, b_ref, g_ref):
        z = _dot(x, w) + b_ref[...]
        z2 = (z * z).astype(bf16)
        ssq = jnp.concatenate([_dot(z2[:, c:c + MXU_DIM], hsum_ref[...]) for c in range(0, gw, MXU_DIM)], axis=1)
        return z * lax.rsqrt(ssq * (1.0 / HEAD_DIM) + NORM_EPS) * g_ref[...]

    def group(gi):
        _, d = DILATED_GROUPS[GROUP_ORDER[gi]]
        n = L // d
        w = min(2 * tq, n)
        per_class = n // tq
        dist_ref = dist_refs[GROUP_ORDER[gi]]

        wq, wk, wv = (r[...].astype(bf16) for r in (wq_ref, wk_ref, wv_ref))
        for r0 in range(0, L, AT_RC):
            rows = slice(r0, r0 + AT_RC)
            x = xs_ref[0, 0, rows, :]
            qs_ref[rows] = (normed(x, wq, bq_ref, gq_ref) * (HEAD_DIM ** -0.5)).astype(bf16)
            ks_ref[rows] = normed(x, wk, bk_ref, gk_ref).astype(bf16)
            vs_ref[rows] = (_dot(x, wv) + bv_ref[...]).astype(bf16)

        def tile(idx, carry):
            r = idx // per_class
            t = idx % per_class
            q0 = pl.multiple_of(idx * tq, tq)
            koff = jnp.clip(t * tq - BAND_HALF, 0, n - w)
            dist_t = dist_ref[(t * tq - koff) // BAND_HALF]
            k0 = pl.multiple_of(r * n + koff, BAND_HALF)
            nat = pl.ds(t * tq * d + r, tq, stride=d) if d > 1 else pl.ds(q0, tq)
            for p in range(npair):
                pc = slice(p * PAIR, (p + 1) * PAIR)
                q = qs_ref[pl.ds(q0, tq), pc]
                zero = jnp.zeros_like(q)
                qq = jnp.concatenate([jnp.where(first, q, zero), jnp.where(first, zero, q)], axis=0)
                lhs = jnp.concatenate([qq, sl_ref[p]], axis=1)
                rhs_t = jnp.concatenate([ks_ref[pl.ds(k0, w), pc], dist_t], axis=1)
                s = lax.dot_general(lhs, rhs_t, nt_dims, preferred_element_type=f32)
                m = jnp.max(s, axis=-1, keepdims=True)
                pr = jnp.exp(s - m).astype(bf16)
                rhs = jnp.concatenate([vs_ref[pl.ds(k0, w), pc], jnp.ones((w, PAIR), bf16)], axis=1)
                ov = _dot(pr, rhs)
                num = jnp.where(first, ov[0:tq, 0:PAIR], ov[tq:2 * tq, 0:PAIR])
                den = jnp.where(first, ov[0:tq, PAIR:2 * PAIR], ov[tq:2 * tq, PAIR:2 * PAIR])
                mb = jnp.where(first, m[0:tq], m[tq:2 * tq])
                if gi == 0:
                    acc_ref[p, nat, :] = num
                    den_ref[p, nat, :] = den
                    mx_ref[p, nat, :] = mb
                else:
                    m_old = mx_ref[p, nat, :]
                    m_new = jnp.maximum(m_old, mb)
                    a = jnp.exp(m_old - m_new)
                    b = jnp.exp(mb - m_new)
                    acc_ref[p, nat, :] = acc_ref[p, nat, :] * a + num * b
                    den_ref[p, nat, :] = den_ref[p, nat, :] * a + den * b
                    mx_ref[p, nat, :] = m_new
            return carry

        lax.fori_loop(0, L // tq, tile, 0, unroll=8)

    for gi in range(N_GROUPS):
        pl.when(pl.program_id(1) == gi)(functools.partial(group, gi))

    @pl.when(pl.program_id(1) == N_GROUPS - 1)
    def _():
        wag = wag_ref[...].astype(bf16)
        for r0 in range(0, L, AT_RC):
            rows = slice(r0, r0 + AT_RC)
            ag = _dot(xs_ref[0, 0, rows, :], wag) + bag_ref[...]
            o = jnp.concatenate([acc_ref[p, rows, :] / den_ref[p, rows, :] for p in range(npair)], axis=1)
            o_ref[0, rows, :] = (o * (ag * jax.nn.sigmoid(ag))).astype(o_ref.dtype)


def _attention(xs, w_in, b_in, gq, gk, hsum):
    _, B, L, _ = xs.shape
    assert PERM_DILATIONS == tuple(DILATED_GROUPS[g][1] for g in GROUP_ORDER) and PERM_DILATIONS[-1] == 1
    assert GROUP_ORDER == tuple(N_GROUPS - 1 - i for i in range(N_GROUPS))
    assert L % AT_RC == 0 and all(L % (Q_TILE * d) == 0 for _, d in DILATED_GROUPS)
    gw = HEADS_PER_GROUP * HEAD_DIM
    npair = HEADS_PER_GROUP // 2
    dists = [jnp.asarray(_attn_dist(L // d, d, window)).astype(bf16) for window, d in DILATED_GROUPS]
    sl = jnp.asarray(_slope_eye()).astype(bf16)
    agb = O_AGATE // gw

    def col(k):
        return lambda b, i: (0, (O_QKV + k * AT_QKV) // gw + (N_GROUPS - 1 - i))

    wspec = lambda k: pl.BlockSpec((D_MODEL, gw), col(k))
    bspec = lambda k: pl.BlockSpec((1, gw), col(k))
    vec = pl.BlockSpec((1, gw), lambda b, i: (0, 0))
    acc = pltpu.VMEM((npair, L, PAIR), f32)
    return pl.pallas_call(
        _attn_kernel,
        grid=(B, N_GROUPS),
        in_specs=[pl.BlockSpec((1, 1, L, D_MODEL), lambda b, i: (i, b, 0, 0)),
                  wspec(0), wspec(1), wspec(2), bspec(0), bspec(1), bspec(2),
                  _const_spec((D_MODEL, gw), lambda b, i: (0, agb)), pl.BlockSpec((1, gw), lambda b, i: (0, agb)),
                  vec, vec, _const_spec(hsum.shape, lambda b, i: (0, 0)), _const_spec(sl.shape, lambda b, i: (0, 0, 0))]
                 + [_const_spec(t.shape, lambda b, i: (0, 0, 0)) for t in dists],
        out_specs=pl.BlockSpec((1, L, gw), lambda b, i: (b, 0, 0)),
        out_shape=jax.ShapeDtypeStruct((B, L, gw), bf16),
        scratch_shapes=[pltpu.VMEM((L, gw), bf16), pltpu.VMEM((L, gw), bf16), pltpu.VMEM((L, gw), bf16),
                        acc, acc, acc],
        compiler_params=pltpu.CompilerParams(dimension_semantics=("arbitrary", "arbitrary"),
                                             vmem_limit_bytes=VMEM_LIMIT),
        name="dilated_attention",
    )(xs, w_in, w_in, w_in, b_in, b_in, b_in, w_in, b_in, gq, gk, hsum, sl, *dists)


def _final_kernel(x_ref, xn_ref, gh_ref, ga_ref, wg_ref, bg_ref, why_ref, wat_ref, wout_ref, out_ref,
                  wg_b, why_b, wat_b, wout_b):
    @pl.when(pl.program_id(0) == 0)
    def _():
        for src, dst in ((wg_ref, wg_b), (why_ref, why_b), (wat_ref, wat_b), (wout_ref, wout_b)):
            dst[...] = src[...].astype(bf16)

    gates = _dot(xn_ref[...], wg_b[...]) + bg_ref[...]
    u_h = _dot(gh_ref[...], why_b[...])
    u_a = _dot(ga_ref[...], wat_b[...])
    merged = jax.nn.sigmoid(gates[:, 0:D_MODEL]) * u_h + jax.nn.sigmoid(gates[:, D_MODEL:]) * u_a
    out_ref[...] = x_ref[...] + _dot(merged.astype(bf16), wout_b[...])


def _final(x2, xs2, gh2, ga2, w_in, b_in, why, wat, wout):
    rows = x2.shape[0]
    tm = 512
    mgw = 2 * D_MODEL
    rspec = lambda c: pl.BlockSpec((tm, c), lambda i: (i, 0))
    cspec = lambda a: _const_spec(a.shape, lambda i: (0, 0))
    return pl.pallas_call(
        _final_kernel,
        grid=(rows // tm,),
        in_specs=[rspec(D_MODEL), pl.BlockSpec((None, tm, D_MODEL), lambda i: (NAT, i, 0)), rspec(HY_WIDTH), rspec(AT_WIDTH),
                  _const_spec((D_MODEL, mgw), lambda i: (0, O_MG // mgw)),
                  pl.BlockSpec((1, mgw), lambda i: (0, O_MG // mgw)), cspec(why), cspec(wat), cspec(wout)],
        out_specs=rspec(D_MODEL),
        out_shape=jax.ShapeDtypeStruct((rows, D_MODEL), f32),
        scratch_shapes=[pltpu.VMEM((D_MODEL, mgw), bf16), pltpu.VMEM(why.shape, bf16), pltpu.VMEM(wat.shape, bf16),
                        pltpu.VMEM(wout.shape, bf16)],
        compiler_params=pltpu.CompilerParams(dimension_semantics=("arbitrary",),
                                             vmem_limit_bytes=VMEM_LIMIT),
        name="merge_output",
    )(x2, xs2, gh2, ga2, w_in, b_in, why, wat, wout)


def _layer(x, norm_g, w_in, b_in, conv_w, conv_b, hf_w1, hf_b1, hf_w2, hf_b2, hf_w3, hf_b3, hf_w4,
           hf_freq, hy_skip, q_norm_g, k_norm_g, w_hy_out, w_at_out, w_out):
    B, L, D = x.shape
    x2 = x.reshape(B * L, D)
    tab = jnp.asarray(_dft_tables(L)).astype(bf16)
    w_in = w_in.astype(f32)
    b_in2 = b_in.astype(f32).reshape(1, IN_COLS)

    xs = _prenorm(x, norm_g.astype(f32).reshape(1, D))

    kr, ki, ks = _filters(L, tab, hf_w1, hf_b1, hf_w2, hf_b2, hf_w3, hf_b3, hf_w4, hf_freq)
    gh = _hyena(xs, w_in, b_in2, conv_w.astype(f32), conv_b.astype(f32).reshape(1, -1),
                tab, kr, ki, ks, hy_skip.astype(f32))

    gq = jnp.tile(q_norm_g.astype(f32), HEADS_PER_GROUP).reshape(1, -1)
    gk = jnp.tile(k_norm_g.astype(f32), HEADS_PER_GROUP).reshape(1, -1)
    head = np.arange(MXU_DIM) // HEAD_DIM
    hsum = jnp.asarray((head[:, None] == head[None, :]).astype(np.float32)).astype(bf16)
    ga = _attention(xs, w_in, b_in2, gq, gk, hsum)

    out = _final(x2, xs.reshape(len(PERM_DILATIONS), B * L, D), gh.reshape(B * L, HY_WIDTH), ga.reshape(B * L, AT_WIDTH),
                 w_in, b_in2, w_hy_out.astype(f32), w_at_out.astype(f32), w_out.astype(f32))
    return out.reshape(B, L, D)


def kernel(x, norm_g, w_in, b_in, conv_w, conv_b, hf_w1, hf_b1, hf_w2, hf_b2, hf_w3, hf_b3, hf_w4,
           hf_freq, hy_skip, q_norm_g, k_norm_g, w_hy_out, w_at_out, w_out):
    depth = norm_g.shape[0]
    for i in range(depth):
        x = _layer(x, norm_g[i], w_in[i], b_in[i], conv_w[i], conv_b[i], hf_w1[i], hf_b1[i], hf_w2[i],
                   hf_b2[i], hf_w3[i], hf_b3[i], hf_w4[i], hf_freq[i], hy_skip[i], q_norm_g[i],
                   k_norm_g[i], w_hy_out[i], w_at_out[i], w_out[i])
    return x
```
